```python
import math
import jax
import jax.numpy as jnp
from jax import lax
import numpy as np

D_MODEL = 4096
BATCH = 2
SEQ = 4096
DEPTH = 2

GRID_W = 64
MIX_WIDTH = D_MODEL
N_DIRS = 2

NA_HEAD_DIM = 128
NA_WIDTH = 3 * MIX_WIDTH // 8
NA_HEADS = NA_WIDTH // NA_HEAD_DIM
NA_WIN_ROWS = 8
NA_WIN_COLS = 16

GDN_HEAD_DIM = 128
GDN_WIDTH = 3 * MIX_WIDTH // 8
GDN_HEADS = GDN_WIDTH // GDN_HEAD_DIM
GDN_CONV = 5
GDN_CHUNK = 64

RWKV_HEAD_DIM = 64
RWKV_WIDTH = MIX_WIDTH - NA_WIDTH - GDN_WIDTH
RWKV_HEADS = RWKV_WIDTH // RWKV_HEAD_DIM
RWKV_DECAY_LORA = 64
RWKV_AAA_LORA = 64
RWKV_GATE_LORA = 128

NA_COLS = 3 * NA_WIDTH
GDN_COLS = 4 * GDN_WIDTH + 2 * N_DIRS * GDN_HEADS
RWKV_COLS = 3 * RWKV_WIDTH + RWKV_DECAY_LORA + RWKV_AAA_LORA + RWKV_GATE_LORA
IN_COLS = NA_COLS + GDN_COLS + RWKV_COLS

FFN_DENSE = 11008
MOE_EXPERTS = 8
MOE_TOP_K = 2
MOE_FF = 4096

RMS_EPS = 1e-6
L2_EPS = 1e-6
RWKV_GN_EPS = 64e-5

kernel_name = "hybrid_natten_gdn_rwkv7_moe_encoder"


def rmsnorm(x, gain, eps=RMS_EPS):
    xf = x.astype(jnp.float32)
    y = xf * lax.rsqrt(jnp.mean(xf * xf, axis=-1, keepdims=True) + eps)
    return (y * gain.astype(jnp.float32)).astype(x.dtype)


def l2norm(x, eps=L2_EPS):
    xf = x.astype(jnp.float32)
    return xf * lax.rsqrt(jnp.sum(xf * xf, axis=-1, keepdims=True) + eps)


def swiglu(u, w_gate, w_up, w_down):
    return (jax.nn.silu(u @ w_gate) * (u @ w_up)) @ w_down


def centred_depthwise_conv(x, w):
    pad = w.shape[0] // 2
    return lax.conv_general_dilated(
        x, w[:, None, :].astype(x.dtype), window_strides=(1,), padding=((pad, pad),),
        dimension_numbers=("NWC", "WIO", "NWC"), feature_group_count=x.shape[-1])


def neighbourhood_attention(q, k, v, rpb):
    bsz, seq = q.shape[0], q.shape[1]
    rows = seq // GRID_W
    win_r = min(NA_WIN_ROWS, rows)
    qg = q.reshape(bsz, rows, GRID_W, NA_HEADS, NA_HEAD_DIM)
    kg = k.reshape(bsz, rows, GRID_W, NA_HEADS, NA_HEAD_DIM)
    vg = v.reshape(bsz, rows, GRID_W, NA_HEADS, NA_HEAD_DIM)
    cols = jnp.arange(GRID_W)
    col_start = jnp.clip(cols - NA_WIN_COLS // 2, 0, GRID_W - NA_WIN_COLS)
    col_idx = col_start[:, None] + jnp.arange(NA_WIN_COLS)[None, :]
    dc_idx = col_idx - cols[:, None] + (NA_WIN_COLS - 1)
    scale = NA_HEAD_DIM ** -0.5

    def row_block(r):
        sr = jnp.clip(r - win_r // 2, 0, rows - win_r)
        q_r = lax.dynamic_index_in_dim(qg, r, axis=1, keepdims=False)
        k_r = lax.dynamic_slice_in_dim(kg, sr, win_r, axis=1)[:, :, col_idx]
        v_r = lax.dynamic_slice_in_dim(vg, sr, win_r, axis=1)[:, :, col_idx]
        dr_idx = sr + jnp.arange(win_r) - r + (NA_WIN_ROWS - 1)
        bias = rpb[:, dr_idx[:, None, None], dc_idx[None, :, :]]
        s = (jnp.einsum("bqhd,brqchd->bhqrc", q_r, k_r).astype(jnp.float32) * scale
             + bias.transpose(0, 2, 1, 3).astype(jnp.float32)[None])
        p = jax.nn.softmax(s.reshape(bsz, NA_HEADS, GRID_W, win_r * NA_WIN_COLS), axis=-1)
        p = p.reshape(s.shape).astype(v.dtype)
        return jnp.einsum("bhqrc,brqchd->bqhd", p, v_r)

    out = lax.map(row_block, jnp.arange(rows))
    return out.transpose(1, 0, 2, 3, 4).reshape(bsz, seq, NA_HEADS, NA_HEAD_DIM)


def na_branch(p, rpb, gain):
    bsz, seq, _ = p.shape
    q, k, v = (t.reshape(bsz, seq, NA_HEADS, NA_HEAD_DIM) for t in jnp.split(p, 3, axis=-1))
    o = neighbourhood_attention(q, k, v, rpb)
    return rmsnorm(o, gain.reshape(NA_HEADS, NA_HEAD_DIM)).reshape(bsz, seq, NA_WIDTH)


def gated_delta_chunked(q, k, v, g, beta):
    bsz, seq, nh, dk = q.shape
    dv = v.shape[-1]
    n = seq // GDN_CHUNK

    def chunks(t):
        t = t.astype(jnp.float32).reshape((bsz, n, GDN_CHUNK, nh) + t.shape[3:])
        return jnp.moveaxis(t, 3, 1)

    qc, kc, vc, bc = chunks(q), chunks(k), chunks(v), chunks(beta)
    gc = jnp.cumsum(chunks(g), axis=-1)
    idx = jnp.arange(GDN_CHUNK)
    incl = idx[:, None] >= idx[None, :]
    strict = idx[:, None] > idx[None, :]
    decay = jnp.exp(jnp.where(incl, gc[..., :, None] - gc[..., None, :], -jnp.inf))
    kb = kc * bc[..., None]
    m = jnp.where(strict, jnp.einsum("bhnid,bhnjd->bhnij", kb, kc) * decay, 0.0)
    eye = jnp.eye(GDN_CHUNK, dtype=jnp.float32)
    t_inv = lax.linalg.triangular_solve(eye + m, jnp.broadcast_to(eye, m.shape),
                                        left_side=True, lower=True, unit_diagonal=True)
    u = t_inv @ (vc * bc[..., None])
    w = t_inv @ (kb * jnp.exp(gc)[..., None])
    a_qk = jnp.einsum("bhnid,bhnjd->bhnij", qc, kc) * decay

    def step(state, inp):
        q_i, k_i, u_i, w_i, g_i, a_i = inp
        v_new = u_i - w_i @ state
        o_i = (q_i * jnp.exp(g_i)[..., None]) @ state + a_i @ v_new
        g_last = g_i[..., -1:]
        state = (state * jnp.exp(g_last)[..., None]
                 + jnp.einsum("bhcd,bhce->bhde", k_i * jnp.exp(g_last - g_i)[..., None], v_new))
        return state, o_i

    xs = tuple(jnp.moveaxis(t, 2, 0) for t in (qc, kc, u, w, gc, a_qk))
    s0 = jnp.zeros((bsz, nh, dk, dv), jnp.float32)
    _, o = lax.scan(step, s0, xs)
    return o.transpose(1, 0, 3, 2, 4).reshape(bsz, seq, nh, dv)


def gdn_branch(p, conv_w, a_log, dt_bias, gain):
    bsz, seq, _ = p.shape
    qkv, z, alpha, beta_raw = jnp.split(
        p, [3 * GDN_WIDTH, 4 * GDN_WIDTH, 4 * GDN_WIDTH + N_DIRS * GDN_HEADS], axis=-1)
    qkv = jax.nn.silu(centred_depthwise_conv(qkv, conv_w))
    q, k, v = (t.reshape(bsz, seq, GDN_HEADS, GDN_HEAD_DIM) for t in jnp.split(qkv, 3, axis=-1))
    q = l2norm(q) * GDN_HEAD_DIM ** -0.5
    k = l2norm(k)
    alpha = alpha.reshape(bsz, seq, N_DIRS, GDN_HEADS).astype(jnp.float32)
    beta = jax.nn.sigmoid(beta_raw.reshape(bsz, seq, N_DIRS, GDN_HEADS).astype(jnp.float32))
    log_decay = -jnp.exp(a_log.astype(jnp.float32)) * jax.nn.softplus(alpha + dt_bias.astype(jnp.float32))
    o_fwd = gated_delta_chunked(q, k, v, log_decay[:, :, 0], beta[:, :, 0])
    o_bwd = jnp.flip(gated_delta_chunked(jnp.flip(q, 1), jnp.flip(k, 1), jnp.flip(v, 1),
                                         jnp.flip(log_decay[:, :, 1], 1), jnp.flip(beta[:, :, 1], 1)), 1)
    z = z.reshape(bsz, seq, GDN_HEADS, GDN_HEAD_DIM).astype(jnp.float32)
    o = rmsnorm(o_fwd + o_bwd, gain) * jax.nn.silu(z)
    return o.reshape(bsz, seq, GDN_WIDTH).astype(p.dtype)


def rwkv7_scan(r, decay, k, v, kk, a, reverse):
    bsz, _, nh, nd = r.shape
    xs = tuple(jnp.moveaxis(t, 1, 0) for t in (r, decay, k, v, kk, a))

    def step(state, inp):
        r_t, w_t, k_t, v_t, kk_t, a_t = inp
        sa = jnp.einsum("bhvk,bhk->bhv", state, kk_t)
        state = (state * w_t[:, :, None, :] - sa[..., None] * (kk_t * a_t)[:, :, None, :]
                 + v_t[..., None] * k_t[:, :, None, :])
        return state, jnp.einsum("bhvk,bhk->bhv", state, r_t)

    s0 = jnp.zeros((bsz, nh, nd, nd), jnp.float32)
    _, y = lax.scan(step, s0, xs, reverse=reverse)
    return jnp.moveaxis(y, 0, 1)


def rwkv_direction(r_h, v_h, kk, k, xw, xa, w0, w_up, a0, a_up, k_a, r_k, reverse):
    bsz, seq, _ = k.shape
    heads = lambda t: t.reshape(bsz, seq, RWKV_HEADS, RWKV_HEAD_DIM).astype(jnp.float32)
    log_w = -jax.nn.softplus(-(w0 + jnp.tanh(xw) @ w_up)) - 0.5
    decay = jnp.exp(-jnp.exp(heads(log_w)))
    a = jax.nn.sigmoid(a0 + xa @ a_up)
    k_d = heads(k * (1 + (a - 1) * k_a))
    y = rwkv7_scan(r_h, decay, k_d, v_h, kk, heads(a), reverse)
    bonus = jnp.sum(r_h * k_d * r_k.astype(jnp.float32), axis=-1, keepdims=True) * v_h
    return y, bonus


def rwkv_branch(p, mu, w0, w_up, a0, a_up, g_up, k_k, k_a, r_k, ln_w, ln_b):
    bsz, seq, _ = p.shape
    padded = jnp.pad(p, ((0, 0), (1, 1), (0, 0)))
    shifted = 0.5 * (padded[:, :-2] + padded[:, 2:])
    p = p + (shifted - p) * mu
    r, k, v, xw, xa, xg = jnp.split(
        p, [RWKV_WIDTH, 2 * RWKV_WIDTH, 3 * RWKV_WIDTH,
            3 * RWKV_WIDTH + RWKV_DECAY_LORA, 3 * RWKV_WIDTH + RWKV_DECAY_LORA + RWKV_AAA_LORA], axis=-1)
    heads = lambda t: t.reshape(bsz, seq, RWKV_HEADS, RWKV_HEAD_DIM).astype(jnp.float32)
    gate = jax.nn.sigmoid(xg) @ g_up
    kk = l2norm(heads(k * k_k))
    r_h, v_h = heads(r), heads(v)
    y_f, bonus_f = rwkv_direction(r_h, v_h, kk, k, xw, xa, w0[0], w_up[0], a0[0], a_up[0], k_a, r_k, False)
    y_b, bonus_b = rwkv_direction(r_h, v_h, kk, k, xw, xa, w0[1], w_up[1], a0[1], a_up[1], k_a, r_k, True)
    y = y_f + y_b
    mean = jnp.mean(y, axis=-1, keepdims=True)
    var = jnp.var(y, axis=-1, keepdims=True)
    y = ((y - mean) * lax.rsqrt(var + RWKV_GN_EPS)).reshape(bsz, seq, RWKV_WIDTH)
    y = y * ln_w.astype(jnp.float32) + ln_b.astype(jnp.float32) + (bonus_f + bonus_b).reshape(bsz, seq, RWKV_WIDTH)
    return (y * gate.astype(jnp.float32)).astype(p.dtype)


def moe_swiglu(u, router, w_gate, w_up, w_down):
    logits = (u @ router).astype(jnp.float32)
    top_vals, top_idx = lax.top_k(logits, MOE_TOP_K)
    top_w = jax.nn.softmax(top_vals, axis=-1)
    combine = jnp.sum(jnp.where(top_idx[..., None] == jnp.arange(MOE_EXPERTS), top_w[..., None], 0.0), axis=-2)
    out = jnp.zeros_like(u)
    for e in range(MOE_EXPERTS):
        out = out + combine[..., e:e + 1].astype(u.dtype) * swiglu(u, w_gate[e], w_up[e], w_down[e])
    return out


def setup_inputs(seed: int = 0) -> dict:
    key = jax.random.key(seed)
    ks = jax.random.split(key, 32)
    f32 = jnp.float32

    def normal(k, shape, scale):
        return jax.random.normal(k, shape, f32) * scale

    def gain(k, shape):
        return 1.0 + 0.02 * jax.random.normal(k, shape, f32)

    n_dense = (DEPTH + 1) // 2
    n_moe = DEPTH // 2
    dt = jnp.exp(jax.random.uniform(ks[7], (DEPTH, N_DIRS, GDN_HEADS), f32, math.log(1e-3), math.log(1e-1)))
    return {
        "x": normal(ks[0], (BATCH, SEQ, D_MODEL), 1.0),
        "norm_mix": gain(ks[1], (DEPTH, D_MODEL)),
        "w_in": normal(ks[2], (DEPTH, D_MODEL, IN_COLS), D_MODEL ** -0.5),
        "w_out": normal(ks[3], (DEPTH, MIX_WIDTH, D_MODEL), MIX_WIDTH ** -0.5),
        "na_rpb": normal(ks[4], (DEPTH, NA_HEADS, 2 * NA_WIN_ROWS - 1, 2 * NA_WIN_COLS - 1), 0.5),
        "na_gain": gain(ks[5], (DEPTH, NA_WIDTH)),
        "gdn_conv_w": normal(ks[6], (DEPTH, GDN_CONV, 3 * GDN_WIDTH), GDN_CONV ** -0.5),
        "gdn_a_log": jnp.log(jax.random.uniform(ks[8], (DEPTH, N_DIRS, GDN_HEADS), f32, 1.0, 16.0)),
        "gdn_dt_bias": dt + jnp.log(-jnp.expm1(-dt)),
        "gdn_gain": gain(ks[9], (DEPTH, GDN_HEAD_DIM)),
        "rwkv_mu": jax.random.uniform(ks[10], (DEPTH, RWKV_COLS), f32),
        "rwkv_w0": jax.random.uniform(ks[11], (DEPTH, N_DIRS, RWKV_WIDTH), f32, -6.0, -1.0),
        "rwkv_w_up": normal(ks[12], (DEPTH, N_DIRS, RWKV_DECAY_LORA, RWKV_WIDTH), 0.1 * RWKV_DECAY_LORA ** -0.5),
        "rwkv_a0": normal(ks[13], (DEPTH, N_DIRS, RWKV_WIDTH), 0.1),
        "rwkv_a_up": normal(ks[14], (DEPTH, N_DIRS, RWKV_AAA_LORA, RWKV_WIDTH), 0.1 * RWKV_AAA_LORA ** -0.5),
        "rwkv_g_up": normal(ks[15], (DEPTH, RWKV_GATE_LORA, RWKV_WIDTH), RWKV_GATE_LORA ** -0.5),
        "rwkv_k_k": 0.85 + 0.02 * jax.random.normal(ks[16], (DEPTH, RWKV_WIDTH), f32),
        "rwkv_k_a": gain(ks[17], (DEPTH, RWKV_WIDTH)),
        "rwkv_r_k": normal(ks[18], (DEPTH, RWKV_HEADS, RWKV_HEAD_DIM), 0.1),
        "rwkv_ln_w": gain(ks[19], (DEPTH, RWKV_WIDTH)),
        "rwkv_ln_b": normal(ks[20], (DEPTH, RWKV_WIDTH), 0.02),
        "norm_ffn": gain(ks[21], (DEPTH, D_MODEL)),
        "ffn_w_gate": normal(ks[22], (n_dense, D_MODEL, FFN_DENSE), D_MODEL ** -0.5),
        "ffn_w_up": normal(ks[23], (n_dense, D_MODEL, FFN_DENSE), D_MODEL ** -0.5),
        "ffn_w_down": normal(ks[24], (n_dense, FFN_DENSE, D_MODEL), FFN_DENSE ** -0.5),
        "moe_router": normal(ks[25], (n_moe, D_MODEL, MOE_EXPERTS), D_MODEL ** -0.5),
        "moe_w_gate": normal(ks[26], (n_moe, MOE_EXPERTS, D_MODEL, MOE_FF), D_MODEL ** -0.5),
        "moe_w_up": normal(ks[27], (n_moe, MOE_EXPERTS, D_MODEL, MOE_FF), D_MODEL ** -0.5),
        "moe_w_down": normal(ks[28], (n_moe, MOE_EXPERTS, MOE_FF, D_MODEL), MOE_FF ** -0.5),
        "norm_final": gain(ks[29], (D_MODEL,)),
    }


def reference(x, norm_mix, w_in, w_out, na_rpb, na_gain, gdn_conv_w, gdn_a_log, gdn_dt_bias, gdn_gain,
              rwkv_mu, rwkv_w0, rwkv_w_up, rwkv_a0, rwkv_a_up, rwkv_g_up, rwkv_k_k, rwkv_k_a, rwkv_r_k,
              rwkv_ln_w, rwkv_ln_b, norm_ffn, ffn_w_gate, ffn_w_up, ffn_w_down,
              moe_router, moe_w_gate, moe_w_up, moe_w_down, norm_final):
    h = x
    for layer in range(DEPTH):
        u = rmsnorm(h, norm_mix[layer])
        proj = u @ w_in[layer]
        p_a, p_b, p_c = jnp.split(proj, [NA_COLS, NA_COLS + GDN_COLS], axis=-1)
        y_a = na_branch(p_a, na_rpb[layer], na_gain[layer])
        y_b = gdn_branch(p_b, gdn_conv_w[layer], gdn_a_log[layer], gdn_dt_bias[layer], gdn_gain[layer])
        y_c = rwkv_branch(p_c, rwkv_mu[layer], rwkv_w0[layer], rwkv_w_up[layer], rwkv_a0[layer],
                          rwkv_a_up[layer], rwkv_g_up[layer], rwkv_k_k[layer], rwkv_k_a[layer],
                          rwkv_r_k[layer], rwkv_ln_w[layer], rwkv_ln_b[layer])
        h = h + jnp.concatenate([y_a, y_b, y_c], axis=-1) @ w_out[layer]
        u = rmsnorm(h, norm_ffn[layer])
        i = layer // 2
        if layer % 2 == 0:
            h = h + swiglu(u, ffn_w_gate[i], ffn_w_up[i], ffn_w_down[i])
        else:
            h = h + moe_swiglu(u, moe_router[i], moe_w_gate[i], moe_w_up[i], moe_w_down[i])
    return rmsnorm(h, norm_final)
```

```python
import functools
import math

import jax
import jax.numpy as jnp
from jax import lax
from jax.experimental import pallas as pl
from jax.experimental.pallas import tpu as pltpu

f32 = jnp.float32
bf16 = jnp.bfloat16
HI = lax.Precision.HIGHEST

LANES = 128
VMEM_LIMIT_BYTES = 56 * 1024 * 1024

D_MODEL = 4096
GRID_W = 64
NA_HEADS, NA_HD = 12, 128
NA_WIDTH = NA_HEADS * NA_HD
NA_WIN_ROWS, NA_WIN_COLS = 8, 16
GDN_HEADS, GDN_HD = 12, 128
GDN_WIDTH = GDN_HEADS * GDN_HD
GDN_CONV = 5
CHUNK = 64
RWKV_HEADS, RWKV_HD = 16, 64
RWKV_WIDTH = RWKV_HEADS * RWKV_HD
RWKV_PAIRS = RWKV_HEADS // 2
N_DIRS = 2
MAIN_COLS = 3 * NA_WIDTH + 4 * GDN_WIDTH
RWKV_COLS = 3 * RWKV_WIDTH + 64 + 64 + 128
SHIFT_COLS = RWKV_COLS
assert SHIFT_COLS % LANES == 0
TAIL_COLS = 3584
GATE_COL_BLOCK = SHIFT_COLS // LANES
MOE_EXPERTS = 8
RMS_EPS = 1e-6
L2_EPS = 1e-6
RWKV_GN_EPS = 64e-5
NEG = -1e30


def _cparams(n_axes):
    return pltpu.CompilerParams(dimension_semantics=("arbitrary",) * n_axes,
                                vmem_limit_bytes=VMEM_LIMIT_BYTES)


def _nt(a, b, precision=None):
    return lax.dot_general(a, b, (((1,), (1,)), ((), ())), precision=precision,
                           preferred_element_type=f32)


def _tn(a, b, precision=None):
    return lax.dot_general(a, b, (((0,), (0,)), ((), ())), precision=precision,
                           preferred_element_type=f32)


def _mm(a, b, precision=None):
    return jnp.dot(a, b, precision=precision, preferred_element_type=f32)


def _sigmoid(x):
    return 1.0 / (1.0 + jnp.exp(-x))


def _softplus(x):
    return jnp.maximum(x, 0.0) + jnp.log(1.0 + jnp.exp(-jnp.abs(x)))


def _iota2(shape, axis):
    return lax.broadcasted_iota(jnp.int32, shape, axis)


def _rmsnorm_kernel(x_ref, g_ref, o_ref):
    x = x_ref[...]
    ms = jnp.mean(x * x, axis=-1, keepdims=True)
    o_ref[...] = (x * lax.rsqrt(ms + RMS_EPS) * g_ref[...]).astype(o_ref.dtype)


def rmsnorm(x, gain, out_dtype):
    n, d = x.shape
    tm = min(256, n)
    return pl.pallas_call(
        _rmsnorm_kernel,
        out_shape=jax.ShapeDtypeStruct((n, d), out_dtype),
        grid=(n // tm,),
        in_specs=[pl.BlockSpec((tm, d), lambda i: (i, 0)), pl.BlockSpec((1, d), lambda i: (0, 0))],
        out_specs=pl.BlockSpec((tm, d), lambda i: (i, 0)),
        compiler_params=_cparams(1),
        name="rmsnorm",
    )(x, gain.reshape(1, d))


def _rmsnorm_router_kernel(x_ref, g_ref, r_ref, o_ref, c_ref):
    x = x_ref[...]
    ms = jnp.mean(x * x, axis=-1, keepdims=True)
    u = x * lax.rsqrt(ms + RMS_EPS) * g_ref[...]
    o_ref[...] = u.astype(o_ref.dtype)
    logits = _mm(u, r_ref[...], HI)
    lane = _iota2(logits.shape, 1)
    lg = jnp.where(lane < MOE_EXPERTS, logits, NEG)
    m1 = jnp.max(lg, axis=-1, keepdims=True)
    i1 = jnp.min(jnp.where(lg == m1, lane, LANES), axis=-1, keepdims=True)
    lg2 = jnp.where(lane == i1, NEG, lg)
    m2 = jnp.max(lg2, axis=-1, keepdims=True)
    i2 = jnp.min(jnp.where(lg2 == m2, lane, LANES), axis=-1, keepdims=True)
    e2 = jnp.exp(m2 - m1)
    w1 = 1.0 / (1.0 + e2)
    w2 = e2 / (1.0 + e2)
    c_ref[...] = jnp.where(lane == i1, w1, 0.0) + jnp.where(lane == i2, w2, 0.0)


def rmsnorm_router(x, gain, router):
    n, d = x.shape
    tm = min(256, n)
    r_pad = jnp.pad(router, ((0, 0), (0, LANES - router.shape[1])))
    return pl.pallas_call(
        _rmsnorm_router_kernel,
        out_shape=[jax.ShapeDtypeStruct((n, d), bf16), jax.ShapeDtypeStruct((n, LANES), f32)],
        grid=(n // tm,),
        in_specs=[pl.BlockSpec((tm, d), lambda i: (i, 0)), pl.BlockSpec((1, d), lambda i: (0, 0)),
                  pl.BlockSpec((d, LANES), lambda i: (0, 0))],
        out_specs=[pl.BlockSpec((tm, d), lambda i: (i, 0)), pl.BlockSpec((tm, LANES), lambda i: (i, 0))],
        compiler_params=_cparams(1),
        name="rmsnorm_router",
    )(x, gain.reshape(1, d), r_pad)


def _proj_kernel(x_ref, w_ref, o_ref, wb_ref):
    @pl.when(pl.program_id(1) == 0)
    def _():
        wb_ref[...] = w_ref[...].astype(bf16)

    o_ref[...] = _mm(x_ref[...], wb_ref[...]).astype(o_ref.dtype)


def proj(x, w, w_index, n_cols, tn, out_dtype=f32, tm=1024):
    n, k = x.shape
    tm = min(tm, n)
    assert n_cols % tn == 0 and n % tm == 0
    return pl.pallas_call(
        _proj_kernel,
        out_shape=jax.ShapeDtypeStruct((n, n_cols), out_dtype),
        grid=(n_cols // tn, n // tm),
        in_specs=[pl.BlockSpec((tm, k), lambda j, i: (i, 0)),
                  pl.BlockSpec((None, k, tn), lambda j, i: (w_index, 0, j))],
        out_specs=pl.BlockSpec((tm, tn), lambda j, i: (i, j)),
        scratch_shapes=[pltpu.VMEM((k, tn), bf16)],
        compiler_params=_cparams(2),
        name="proj",
    )(x, w)


def _outproj_kernel(xa_ref, xb_ref, xc_ref, w_ref, res_ref, o_ref, wb_ref):
    @pl.when(pl.program_id(1) == 0)
    def _():
        wb_ref[...] = w_ref[...].astype(bf16)

    acc = _mm(xa_ref[...], wb_ref[0:NA_WIDTH, :])
    acc += _mm(xb_ref[...], wb_ref[NA_WIDTH:NA_WIDTH + GDN_WIDTH, :])
    acc += _mm(xc_ref[...], wb_ref[NA_WIDTH + GDN_WIDTH:, :])
    o_ref[...] = res_ref[...] + acc


def outproj(ya, yb, yc, w, layer, res, tn=512, tm=1024):
    n = ya.shape[0]
    d = w.shape[-1]
    tm = min(tm, n)
    return pl.pallas_call(
        _outproj_kernel,
        out_shape=jax.ShapeDtypeStruct((n, d), f32),
        grid=(d // tn, n // tm),
        in_specs=[pl.BlockSpec((tm, NA_WIDTH), lambda j, i: (i, 0)),
                  pl.BlockSpec((tm, GDN_WIDTH), lambda j, i: (i, 0)),
                  pl.BlockSpec((tm, RWKV_WIDTH), lambda j, i: (i, 0)),
                  pl.BlockSpec((None, D_MODEL, tn), lambda j, i: (layer, 0, j)),
                  pl.BlockSpec((tm, tn), lambda j, i: (i, j))],
        out_specs=pl.BlockSpec((tm, tn), lambda j, i: (i, j)),
        scratch_shapes=[pltpu.VMEM((D_MODEL, tn), bf16)],
        compiler_params=_cparams(2),
        name="outproj",
    )(ya, yb, yc, w, res)


def _gateup_kernel(x_ref, wg_ref, wu_ref, o_ref, wgb_ref, wub_ref):
    @pl.when(pl.program_id(1) == 0)
    def _():
        wgb_ref[...] = wg_ref[...].astype(bf16)
        wub_ref[...] = wu_ref[...].astype(bf16)

    x = x_ref[...]
    g = _mm(x, wgb_ref[...])
    u = _mm(x, wub_ref[...])
    o_ref[...] = (g * _sigmoid(g) * u).astype(o_ref.dtype)


def gateup(x, w_gate, w_up, w_index, tn=256, tm=1024):
    n, k = x.shape
    ff = w_gate.shape[-1]
    tm = min(tm, n)
    return pl.pallas_call(
        _gateup_kernel,
        out_shape=jax.ShapeDtypeStruct((n, ff), bf16),
        grid=(ff // tn, n // tm),
        in_specs=[pl.BlockSpec((tm, k), lambda j, i: (i, 0)),
                  pl.BlockSpec((None, k, tn), lambda j, i: (w_index, 0, j)),
                  pl.BlockSpec((None, k, tn), lambda j, i: (w_index, 0, j))],
        out_specs=pl.BlockSpec((tm, tn), lambda j, i: (i, j)),
        scratch_shapes=[pltpu.VMEM((k, tn), bf16), pltpu.VMEM((k, tn), bf16)],
        compiler_params=_cparams(2),
        name="gateup",
    )(x, w_gate, w_up)


def _moe_gateup_kernel(x_ref, c_ref, wg_ref, wu_ref, o_ref, wgb_ref, wub_ref):
    @pl.when(pl.program_id(2) == 0)
    def _():
        wgb_ref[...] = wg_ref[...].astype(bf16)
        wub_ref[...] = wu_ref[...].astype(bf16)

    e = pl.program_id(0)
    c = c_ref[...]
    ce = jnp.sum(jnp.where(_iota2(c.shape, 1) == e, c, 0.0), axis=-1, keepdims=True)
    x = x_ref[...]
    g = _mm(x, wgb_ref[...])
    u = _mm(x, wub_ref[...])
    o_ref[...] = (g * _sigmoid(g) * u * ce).astype(o_ref.dtype)


def moe_gateup(x, combine, w_gate, w_up, w_index, tn=256, tm=1024):
    n, k = x.shape
    ne, ff = w_gate.shape[1], w_gate.shape[-1]
    tm = min(tm, n)
    nj = ff // tn
    return pl.pallas_call(
        _moe_gateup_kernel,
        out_shape=jax.ShapeDtypeStruct((n, ne * ff), bf16),
        grid=(ne, nj, n // tm),
        in_specs=[pl.BlockSpec((tm, k), lambda e, j, i: (i, 0)),
                  pl.BlockSpec((tm, LANES), lambda e, j, i: (i, 0)),
                  pl.BlockSpec((None, None, k, tn), lambda e, j, i: (w_index, e, 0, j)),
                  pl.BlockSpec((None, None, k, tn), lambda e, j, i: (w_index, e, 0, j))],
        out_specs=pl.BlockSpec((tm, tn), lambda e, j, i: (i, e * nj + j)),
        scratch_shapes=[pltpu.VMEM((k, tn), bf16), pltpu.VMEM((k, tn), bf16)],
        compiler_params=_cparams(3),
        name="moe_gateup",
    )(x, combine, w_gate, w_up)


def _down_kernel(x_ref, w_ref, res_ref, o_ref, acc_ref, *, nk):
    kk = pl.program_id(2)
    part = _mm(x_ref[...], w_ref[...].astype(bf16))

    @pl.when(kk == 0)
    def _():
        acc_ref[...] = res_ref[...] + part

    @pl.when(kk > 0)
    def _():
        acc_ref[...] += part

    @pl.when(kk == nk - 1)
    def _():
        o_ref[...] = acc_ref[...]


def down(x, w, res, tk, tn=256, tm=1024):
    n, kdim = x.shape
    d = w.shape[-1]
    tm = min(tm, n)
    nk = kdim // tk
    assert kdim % tk == 0
    return pl.pallas_call(
        functools.partial(_down_kernel, nk=nk),
        out_shape=jax.ShapeDtypeStruct((n, d), f32),
        grid=(d // tn, n // tm, nk),
        in_specs=[pl.BlockSpec((tm, tk), lambda j, i, k: (i, k)),
                  pl.BlockSpec((tk, tn), lambda j, i, k: (k, j)),
                  pl.BlockSpec((tm, tn), lambda j, i, k: (i, j))],
        out_specs=pl.BlockSpec((tm, tn), lambda j, i, k: (i, j)),
        scratch_shapes=[pltpu.VMEM((tm, tn), f32)],
        compiler_params=_cparams(3),
        name="down",
    )(x, w, res)


def _na_bias_table(rpb):
    off = jnp.arange(NA_WIN_ROWS)[:, None, None, None]
    c = jnp.arange(GRID_W)[None, :, None, None]
    j = jnp.arange(NA_WIN_ROWS)[None, None, :, None]
    kc = jnp.arange(GRID_W)[None, None, None, :]
    cs = jnp.clip(c - NA_WIN_COLS // 2, 0, GRID_W - NA_WIN_COLS)
    valid = (kc >= cs) & (kc < cs + NA_WIN_COLS)
    dr = jnp.broadcast_to(j - off + (NA_WIN_ROWS - 1), (NA_WIN_ROWS, GRID_W, NA_WIN_ROWS, GRID_W))
    dc = jnp.clip(jnp.broadcast_to(kc - c + (NA_WIN_COLS - 1), dr.shape), 0, 2 * NA_WIN_COLS - 2)
    tab = jnp.where(valid[None], rpb[:, dr, dc], NEG)
    return tab.reshape(rpb.shape[0], NA_WIN_ROWS, GRID_W, NA_WIN_ROWS * GRID_W).astype(f32)


def _na_kernel(q_ref, k_ref, v_ref, bias_ref, gain_ref, o_ref, kb_ref, vb_ref, *, rows, rows_per_step):
    g = pl.program_id(2)

    @pl.when(g == 0)
    def _():
        kb_ref[...] = k_ref[...].astype(bf16)
        vb_ref[...] = v_ref[...].astype(bf16)

    scale = NA_HD ** -0.5
    win = NA_WIN_ROWS * GRID_W

    def body(i, carry):
        r = g * rows_per_step + i
        sr = jnp.clip(r - NA_WIN_ROWS // 2, 0, rows - NA_WIN_ROWS)
        off = r - sr
        qrows = pl.ds(pl.multiple_of(i * GRID_W, GRID_W), GRID_W)
        krows = pl.ds(pl.multiple_of(sr * GRID_W, GRID_W), win)
        q = q_ref[qrows, :].astype(bf16)
        s = _nt(q, kb_ref[krows, :]) * scale + bias_ref[off]
        m = jnp.max(s, axis=-1, keepdims=True)
        p = jnp.exp(s - m)
        l = jnp.sum(p, axis=-1, keepdims=True)
        p = p / l
        ov = _mm(p.astype(bf16), vb_ref[krows, :])
        ms = jnp.mean(ov * ov, axis=-1, keepdims=True)
        o_ref[qrows, :] = (ov * lax.rsqrt(ms + RMS_EPS) * gain_ref[...]).astype(o_ref.dtype)
        return carry

    lax.fori_loop(0, rows_per_step, body, 0)


def na_branch(pm, rpb, gain, rows_per_step=8):
    b, t, _ = pm.shape
    rows = t // GRID_W
    assert rows >= NA_WIN_ROWS and rows % rows_per_step == 0
    tq = rows_per_step * GRID_W
    bias = _na_bias_table(rpb)
    return pl.pallas_call(
        functools.partial(_na_kernel, rows=rows, rows_per_step=rows_per_step),
        out_shape=jax.ShapeDtypeStruct((b, t, NA_WIDTH), bf16),
        grid=(b, NA_HEADS, rows // rows_per_step),
        in_specs=[pl.BlockSpec((None, tq, NA_HD), lambda bi, h, g: (bi, g, h)),
                  pl.BlockSpec((None, t, NA_HD), lambda bi, h, g: (bi, 0, NA_HEADS + h)),
                  pl.BlockSpec((None, t, NA_HD), lambda bi, h, g: (bi, 0, 2 * NA_HEADS + h)),
                  pl.BlockSpec((None, NA_WIN_ROWS, GRID_W, NA_WIN_ROWS * GRID_W), lambda bi, h, g: (h, 0, 0, 0)),
                  pl.BlockSpec((None, 1, NA_HD), lambda bi, h, g: (h, 0, 0))],
        out_specs=pl.BlockSpec((None, tq, NA_HD), lambda bi, h, g: (bi, g, h)),
        scratch_shapes=[pltpu.VMEM((t, NA_HD), bf16), pltpu.VMEM((t, NA_HD), bf16)],
        compiler_params=_cparams(3),
        name="na_attention",
    )(pm, pm, pm, bias, gain.reshape(NA_HEADS, 1, NA_HD))


def _gdn_prep_kernel(x_ref, w_ref, o_ref, xp_ref, *, t, tc):
    c = pl.program_id(1)
    pad = 8
    xp_ref[0:pad, :] = jnp.zeros((pad, LANES), f32)
    xp_ref[t + pad:t + 2 * pad, :] = jnp.zeros((pad, LANES), f32)
    xp_ref[pad:t + pad, :] = x_ref[...]
    half = GDN_CONV // 2
    for ci in range(t // tc):
        acc = jnp.zeros((tc, LANES), f32)
        for j in range(GDN_CONV):
            s0 = ci * tc + pad - half + j
            acc = acc + xp_ref[s0:s0 + tc, :] * w_ref[j:j + 1, :]
        y = acc * _sigmoid(acc)
        inv = lax.rsqrt(jnp.sum(y * y, axis=-1, keepdims=True) + L2_EPS)
        fac = jnp.where(c < GDN_HEADS, inv * GDN_HD ** -0.5, jnp.where(c < 2 * GDN_HEADS, inv, 1.0))
        o_ref[ci * tc:(ci + 1) * tc, :] = y * fac


def gdn_prep(pm, conv_w, layer):
    b, t, _ = pm.shape
    tc = min(512, t)
    col0 = 3 * NA_WIDTH // LANES
    nblk = 3 * GDN_WIDTH // LANES
    return pl.pallas_call(
        functools.partial(_gdn_prep_kernel, t=t, tc=tc),
        out_shape=jax.ShapeDtypeStruct((b, t, 3 * GDN_WIDTH), f32),
        grid=(b, nblk),
        in_specs=[pl.BlockSpec((None, t, LANES), lambda bi, c: (bi, 0, col0 + c)),
                  pl.BlockSpec((None, GDN_CONV, LANES), lambda bi, c: (layer, 0, c))],
        out_specs=pl.BlockSpec((None, t, LANES), lambda bi, c: (bi, 0, c)),
        scratch_shapes=[pltpu.VMEM((t + 16, LANES), f32)],
        compiler_params=_cparams(2),
        name="gdn_prep",
    )(pm, conv_w)


def _level_masks(ii, jj, size):
    masks = []
    s = 1
    while s < size:
        masks.append(((ii // (2 * s)) == (jj // (2 * s))) & ((ii // s) != (jj // s)))
        s *= 2
    return masks


def _tri_inverse(m, eye, masks):
    x = eye - jnp.where(masks[0], m, 0.0)
    for mask in masks[1:]:
        xc = _mm(x, jnp.where(mask, m, 0.0), HI)
        x = x - _mm(xc, x, HI)
    return x


def _gdn_kernel(q_ref, k_ref, v_ref, z_ref, ab_ref, alog_ref, dtb_ref, gain_ref, o_ref,
                u_s, w_s, qg_s, kg_s, a_s, el_s, oacc, *, t):
    h = pl.program_id(1)
    n = t // CHUNK
    ii = _iota2((CHUNK, CHUNK), 0)
    jj = _iota2((CHUNK, CHUNK), 1)
    eye = (ii == jj).astype(f32)
    lvl = _level_masks(ii, jj, CHUNK)
    sel_k = _iota2((LANES, LANES), 0)
    sel_row = _iota2((CHUNK, LANES), 1)

    def prep(c, carry):
        rows = pl.ds(pl.multiple_of(c * CHUNK, CHUNK), CHUNK)
        q = q_ref[rows, :]
        k = k_ref[rows, :]
        v = v_ref[rows, :]
        ab = ab_ref[rows, :]
        g_all = -jnp.exp(alog_ref[...]) * _softplus(ab + dtb_ref[...])
        b_all = _sigmoid(ab)
        qk = _nt(q.astype(bf16), k.astype(bf16))
        for d in range(N_DIRS):
            la = d * GDN_HEADS + h
            lb = N_DIRS * GDN_HEADS + la
            if d == 0:
                incl, strict = ii >= jj, ii > jj
            else:
                incl, strict = ii <= jj, ii < jj
            cs = _mm(incl.astype(f32), g_all, HI)
            gcol = _mm(cs, (sel_k == la).astype(f32), HI)
            bcol = _mm(b_all, (sel_k == lb).astype(f32), HI)
            grow = _nt((sel_row == la).astype(f32), cs, HI)
            gtot = gcol[CHUNK - 1:CHUNK, :] if d == 0 else gcol[0:1, :]
            dec = jnp.exp(jnp.where(incl, gcol[:, :CHUNK] - grow, NEG))
            kb = k * bcol
            m = jnp.where(strict, _nt(kb.astype(bf16), k.astype(bf16)) * dec, 0.0)
            tinv = _tri_inverse(m, eye, lvl)
            eg = jnp.exp(gcol)
            rhs = jnp.concatenate([v * bcol, kb * eg], axis=1)
            uw = _mm(tinv, rhs, HI)
            u_s[d, rows, :] = uw[:, :GDN_HD]
            w_s[d, rows, :] = uw[:, GDN_HD:].astype(bf16)
            a_s[d, rows, :] = jnp.where(incl, qk * dec, 0.0).astype(bf16)
            qg_s[d, rows, :] = (q * eg).astype(bf16)
            kg_s[d, rows, :] = (k * jnp.exp(gtot - gcol)).astype(bf16)
            el_s[d, pl.ds(pl.multiple_of(c * 8, 8), 8), :] = jnp.broadcast_to(jnp.exp(gtot), (8, LANES))
        return carry

    lax.fori_loop(0, n, prep, 0)
    oacc[...] = jnp.zeros(oacc.shape, f32)

    def scan(i, states):
        new = []
        for d in range(N_DIRS):
            c = i if d == 0 else n - 1 - i
            rows = pl.ds(pl.multiple_of(c * CHUNK, CHUNK), CHUNK)
            s = states[d]
            sb = s.astype(bf16)
            vnew = u_s[d, rows, :] - _mm(w_s[d, rows, :], sb)
            vnb = vnew.astype(bf16)
            o = _mm(qg_s[d, rows, :], sb) + _mm(a_s[d, rows, :], vnb)
            el = el_s[d, pl.ds(pl.multiple_of(c * 8, 8), 8), :][0:1, :]
            new.append(s * el + _tn(kg_s[d, rows, :], vnb))
            oacc[rows, :] += o
        return tuple(new)

    s0 = jnp.zeros((GDN_HD, GDN_HD), f32)
    lax.fori_loop(0, n, scan, (s0, s0))

    tb = min(512, t)
    for ci in range(t // tb):
        rows = slice(ci * tb, (ci + 1) * tb)
        o = oacc[rows, :]
        z = z_ref[rows, :]
        ms = jnp.mean(o * o, axis=-1, keepdims=True)
        o_ref[rows, :] = (o * lax.rsqrt(ms + RMS_EPS) * gain_ref[...] * (z * _sigmoid(z))).astype(o_ref.dtype)


def gdn_branch(pm, pt, qkv, a_log, dt_bias, gain):
    b, t, _ = pm.shape
    n = t // CHUNK
    zcol0 = (3 * NA_WIDTH + 3 * GDN_WIDTH) // LANES
    npad = LANES - N_DIRS * GDN_HEADS
    alog = jnp.pad(a_log.reshape(1, -1), ((0, 0), (0, npad)))
    dtb = jnp.pad(dt_bias.reshape(1, -1), ((0, 0), (0, npad)))
    hd = lambda off: (lambda bi, h: (bi, 0, off + h))
    vec = pl.BlockSpec((1, LANES), lambda bi, h: (0, 0))
    return pl.pallas_call(
        functools.partial(_gdn_kernel, t=t),
        out_shape=jax.ShapeDtypeStruct((b, t, GDN_WIDTH), bf16),
        grid=(b, GDN_HEADS),
        in_specs=[pl.BlockSpec((None, t, GDN_HD), hd(0)),
                  pl.BlockSpec((None, t, GDN_HD), hd(GDN_HEADS)),
                  pl.BlockSpec((None, t, GDN_HD), hd(2 * GDN_HEADS)),
                  pl.BlockSpec((None, t, GDN_HD), hd(zcol0)),
                  pl.BlockSpec((None, t, LANES), lambda bi, h: (bi, 0, GATE_COL_BLOCK)),
                  vec, vec, vec],
        out_specs=pl.BlockSpec((None, t, GDN_HD), hd(0)),
        scratch_shapes=[pltpu.VMEM((N_DIRS, t, GDN_HD), f32),
                        pltpu.VMEM((N_DIRS, t, GDN_HD), bf16),
                        pltpu.VMEM((N_DIRS, t, GDN_HD), bf16),
                        pltpu.VMEM((N_DIRS, t, GDN_HD), bf16),
                        pltpu.VMEM((N_DIRS, t, CHUNK), bf16),
                        pltpu.VMEM((N_DIRS, n * 8, LANES), f32),
                        pltpu.VMEM((t, GDN_HD), f32)],
        compiler_params=_cparams(2),
        name="gdn_scan",
    )(qkv, qkv, qkv, pm, pt, alog, dtb, gain.reshape(1, GDN_HD))


def _rwkv_prep_kernel(cur_ref, prev_ref, next_ref, mu_ref, wup_ref, aup_ref, gup_ref, w0_ref, a0_ref,
                      rkv_ref, gate_ref, lw_ref, eta_ref, buf, *, tb):
    i = pl.program_id(1)
    nt = pl.num_programs(1)
    buf[8:8 + tb, :] = cur_ref[...]
    buf[7:8, :] = jnp.where(i > 0, prev_ref[7:8, :], 0.0)
    buf[8 + tb:9 + tb, :] = jnp.where(i < nt - 1, next_ref[0:1, :], 0.0)

    def mixed(c0, c1):
        x = buf[8:8 + tb, c0:c1]
        sh = 0.5 * (buf[7:7 + tb, c0:c1] + buf[9:9 + tb, c0:c1])
        return x + (sh - x) * mu_ref[:, c0:c1]

    for gi in range(3):
        c0 = gi * RWKV_WIDTH
        rkv_ref[:, c0:c0 + RWKV_WIDTH] = mixed(c0, c0 + RWKV_WIDTH)
    xwa = mixed(3 * RWKV_WIDTH, 3 * RWKV_WIDTH + LANES)
    xg = mixed(3 * RWKV_WIDTH + LANES, 3 * RWKV_WIDTH + 2 * LANES)
    gate_ref[...] = _mm(_sigmoid(xg).astype(bf16), gup_ref[...].astype(bf16))
    txw = jnp.tanh(xwa)
    for d in range(N_DIRS):
        log_w = -_softplus(-(w0_ref[d] + _mm(txw, wup_ref[d], HI))) - 0.5
        lw_ref[d] = -jnp.exp(log_w)
        eta_ref[d] = _sigmoid(a0_ref[d] + _mm(xwa, aup_ref[d], HI))


def rwkv_prep(pt, mu, w0, w_up, a0, a_up, g_up, tb=256):
    b, t, _ = pt.shape
    tb = min(tb, t)
    nb8 = tb // 8
    mu_p = mu.reshape(1, SHIFT_COLS)
    lora = w_up.shape[1]
    wup_p = jnp.pad(w_up, ((0, 0), (0, LANES - lora), (0, 0)))
    aup_p = jnp.pad(a_up, ((0, 0), (lora, LANES - lora - a_up.shape[1]), (0, 0)))
    out_sds = lambda *lead: jax.ShapeDtypeStruct(lead + (b, t, RWKV_WIDTH), f32)
    full = lambda shape: pl.BlockSpec(shape, lambda bi, i: (0,) * len(shape))
    return pl.pallas_call(
        functools.partial(_rwkv_prep_kernel, tb=tb),
        out_shape=[jax.ShapeDtypeStruct((b, t, 3 * RWKV_WIDTH), f32), out_sds(), out_sds(N_DIRS), out_sds(N_DIRS)],
        grid=(b, t // tb),
        in_specs=[pl.BlockSpec((None, tb, SHIFT_COLS), lambda bi, i: (bi, i, 0)),
                  pl.BlockSpec((None, 8, SHIFT_COLS), lambda bi, i: (bi, jnp.maximum(i * nb8 - 1, 0), 0)),
                  pl.BlockSpec((None, 8, SHIFT_COLS), lambda bi, i: (bi, jnp.minimum((i + 1) * nb8, t // 8 - 1), 0)),
                  full((1, SHIFT_COLS)), full((N_DIRS, LANES, RWKV_WIDTH)), full((N_DIRS, LANES, RWKV_WIDTH)),
                  full((LANES, RWKV_WIDTH)), full((N_DIRS, 1, RWKV_WIDTH)), full((N_DIRS, 1, RWKV_WIDTH))],
        out_specs=[pl.BlockSpec((None, tb, 3 * RWKV_WIDTH), lambda bi, i: (bi, i, 0)),
                   pl.BlockSpec((None, tb, RWKV_WIDTH), lambda bi, i: (bi, i, 0)),
                   pl.BlockSpec((N_DIRS, None, tb, RWKV_WIDTH), lambda bi, i: (0, bi, i, 0)),
                   pl.BlockSpec((N_DIRS, None, tb, RWKV_WIDTH), lambda bi, i: (0, bi, i, 0))],
        scratch_shapes=[pltpu.VMEM((tb + 16, SHIFT_COLS), f32)],
        compiler_params=_cparams(2),
        name="rwkv_prep",
    )(pt, pt, pt, mu_p, wup_p, aup_p, g_up, w0.reshape(N_DIRS, 1, -1), a0.reshape(N_DIRS, 1, -1))


def _stack2(x, m0, m1):
    return jnp.concatenate([x * m0, x * m1], axis=0)


def _rwkv_kernel(r_ref, k_ref, v_ref, gate_ref, lw0_ref, lw1_ref, eta0_ref, eta1_ref,
                 kk_ref, ka_ref, rk_ref, lnw_ref, lnb_ref, o_ref,
                 wt_s, uv_s, yv_s, arb_s, rt_s, bh_s, sk_s, eg_s, yacc, *, t, seg):
    n = t // CHUNK
    nseg = n // seg
    c2 = 2 * CHUNK
    lane = _iota2((1, LANES), 1)
    m0 = (lane < RWKV_HD).astype(f32)
    m1 = 1.0 - m0
    bd = (_iota2((LANES, LANES), 0) // RWKV_HD == _iota2((LANES, LANES), 1) // RWKV_HD).astype(f32)
    ii = _iota2((c2, c2), 0)
    jj = _iota2((c2, c2), 1)
    same = (ii // CHUNK) == (jj // CHUNK)
    ti, tj = ii % CHUNK, jj % CHUNK
    eye = (ii == jj).astype(f32)
    lvl = _level_masks(ti, tj, CHUNK)
    ci_ = _iota2((CHUNK, CHUNK), 0)
    cj_ = _iota2((CHUNK, CHUNK), 1)
    lw_refs = (lw0_ref, lw1_ref)
    eta_refs = (eta0_ref, eta1_ref)
    kkw = kk_ref[...]
    kaw = ka_ref[...]

    def prep_chunk(d, c, slot):
        rows = pl.ds(pl.multiple_of(c * CHUNK, CHUNK), CHUNK)
        r = r_ref[rows, :]
        k = k_ref[rows, :]
        v = v_ref[rows, :]
        lw = lw_refs[d][rows, :]
        eta = eta_refs[d][rows, :]
        kkr = k * kkw
        kk = kkr * lax.rsqrt(_mm(kkr * kkr, bd, HI) + L2_EPS)
        kd = k * (1.0 + (eta - 1.0) * kaw)
        bvec = -kk * eta
        if d == 0:
            tri, strict, incl = (ci_ >= cj_), same & (ti > tj), same & (ti >= tj)
        else:
            tri, strict, incl = (ci_ <= cj_), same & (ti < tj), same & (ti <= tj)
        gcum = _mm(tri.astype(f32), lw, HI)
        gtot = gcum[CHUNK - 1:CHUNK, :] if d == 0 else gcum[0:1, :]
        eneg = jnp.exp(-gcum)
        erem = jnp.exp(gtot - gcum)
        at2 = _stack2(kk * jnp.exp(gcum - lw), m0, m1)
        rt2 = _stack2(r * jnp.exp(gcum), m0, m1)
        kt2 = _stack2(kd * eneg, m0, m1)
        bt2 = _stack2(bvec * eneg, m0, m1)
        v2 = _stack2(v, m0, m1)
        gg = _nt(jnp.concatenate([at2, rt2], axis=0).astype(bf16),
                 jnp.concatenate([kt2, bt2], axis=0).astype(bf16))
        a_ak = jnp.where(strict, gg[0:c2, 0:c2], 0.0)
        a_ab = jnp.where(strict, gg[0:c2, c2:2 * c2], 0.0)
        a_rk = jnp.where(incl, gg[c2:2 * c2, 0:c2], 0.0)
        a_rb = jnp.where(incl, gg[c2:2 * c2, c2:2 * c2], 0.0)
        tinv = _tri_inverse(-a_ab, eye, lvl)
        v2b = v2.astype(bf16)
        rhs = jnp.concatenate([at2, _mm(a_ak.astype(bf16), v2b)], axis=1)
        wu = _mm(tinv, rhs, HI)
        base = pl.multiple_of(slot * c2, c2)
        srows = pl.ds(base, c2)
        wt_s[d, srows, :] = wu[:, :LANES].astype(bf16)
        uv_s[d, srows, :] = wu[:, LANES:]
        yv_s[d, srows, :] = _mm(a_rk.astype(bf16), v2b)
        arb_s[d, srows, :] = a_rb.astype(bf16)
        rt_s[d, srows, :] = rt2.astype(bf16)
        bh_s[d, srows, :] = _stack2(bvec * erem, m0, m1).astype(bf16)
        sk_s[d, srows, :] = _tn(v2b, _stack2(kd * erem, m0, m1).astype(bf16))
        eg_s[d, pl.ds(pl.multiple_of(slot * 8, 8), 8), :] = jnp.broadcast_to(jnp.exp(gtot), (8, LANES))

    def step(d, c, slot, s):
        rows = pl.ds(pl.multiple_of(c * CHUNK, CHUNK), CHUNK)
        srows = pl.ds(pl.multiple_of(slot * c2, c2), c2)
        sb = s.astype(bf16)
        u2 = uv_s[d, srows, :] + _nt(wt_s[d, srows, :], sb)
        u2b = u2.astype(bf16)
        y2 = yv_s[d, srows, :] + _nt(rt_s[d, srows, :], sb) + _mm(arb_s[d, srows, :], u2b)
        eg = eg_s[d, pl.ds(pl.multiple_of(slot * 8, 8), 8), :][0:1, :]
        s_new = s * eg + sk_s[d, srows, :] + _tn(u2b, bh_s[d, srows, :])
        yacc[rows, :] += y2[0:CHUNK, :] + y2[CHUNK:c2, :]
        return s_new

    yacc[...] = jnp.zeros(yacc.shape, f32)

    def segment(sg, states):
        def prep(j, carry):
            prep_chunk(0, sg * seg + j, j)
            prep_chunk(1, n - 1 - (sg * seg + j), j)
            return carry

        lax.fori_loop(0, seg, prep, 0)

        def scan(j, st):
            return (step(0, sg * seg + j, j, st[0]), step(1, n - 1 - (sg * seg + j), j, st[1]))

        return lax.fori_loop(0, seg, scan, states)

    s0 = jnp.zeros((LANES, LANES), f32)
    lax.fori_loop(0, nseg, segment, (s0, s0))

    tb = min(512, t)
    inv_hd = 1.0 / RWKV_HD
    for ci in range(t // tb):
        rows = slice(ci * tb, (ci + 1) * tb)
        y = yacc[rows, :]
        mean = _mm(y, bd, HI) * inv_hd
        dlt = y - mean
        var = _mm(dlt * dlt, bd, HI) * inv_hd
        yn = dlt * lax.rsqrt(var + RWKV_GN_EPS) * lnw_ref[...] + lnb_ref[...]
        k = k_ref[rows, :]
        kd_sum = k * (2.0 + (eta0_ref[rows, :] + eta1_ref[rows, :] - 2.0) * kaw)
        bonus = _mm(r_ref[rows, :] * kd_sum * rk_ref[...], bd, HI) * v_ref[rows, :]
        o_ref[rows, :] = ((yn + bonus) * gate_ref[rows, :]).astype(o_ref.dtype)


def rwkv_branch(rkv, gate, lw, eta, k_k, k_a, r_k, ln_w, ln_b, seg=16):
    b, t, _ = gate.shape
    n = t // CHUNK
    seg = min(seg, n)
    assert n % seg == 0
    c2 = 2 * CHUNK
    blk = lambda off: pl.BlockSpec((None, t, LANES), lambda bi, p: (bi, 0, off + p))
    dblk = lambda d: pl.BlockSpec((None, None, t, LANES), lambda bi, p: (d, bi, 0, p))
    vec = pl.BlockSpec((1, LANES), lambda bi, p: (0, p))
    row = lambda x: x.reshape(1, RWKV_WIDTH)
    seg_f32 = pltpu.VMEM((N_DIRS, seg * c2, LANES), f32)
    seg_b16 = pltpu.VMEM((N_DIRS, seg * c2, LANES), bf16)
    return pl.pallas_call(
        functools.partial(_rwkv_kernel, t=t, seg=seg),
        out_shape=jax.ShapeDtypeStruct((b, t, RWKV_WIDTH), bf16),
        grid=(b, RWKV_PAIRS),
        in_specs=[blk(0), blk(RWKV_PAIRS), blk(2 * RWKV_PAIRS), blk(0),
                  dblk(0), dblk(1), dblk(0), dblk(1), vec, vec, vec, vec, vec],
        out_specs=blk(0),
        scratch_shapes=[seg_b16,
                        seg_f32,
                        seg_f32,
                        seg_b16,
                        seg_b16,
                        seg_b16,
                        seg_f32,
                        pltpu.VMEM((N_DIRS, seg * 8, LANES), f32),
                        pltpu.VMEM((t, LANES), f32)],
        compiler_params=_cparams(2),
        name="rwkv_scan",
    )(rkv, rkv, rkv, gate, lw, lw, eta, eta, row(k_k), row(k_a), row(r_k), row(ln_w), row(ln_b))


def _tail_weights(w_in_layer):
    gdn_gate = w_in_layer[:, MAIN_COLS:MAIN_COLS + 2 * N_DIRS * GDN_HEADS]
    rwkv = w_in_layer[:, MAIN_COLS + 2 * N_DIRS * GDN_HEADS:]
    pad = jnp.zeros((w_in_layer.shape[0], TAIL_COLS - SHIFT_COLS - gdn_gate.shape[1]), f32)
    return jnp.concatenate([rwkv, gdn_gate, pad], axis=1)


def kernel(x, norm_mix, w_in, w_out, na_rpb, na_gain, gdn_conv_w, gdn_a_log, gdn_dt_bias, gdn_gain, rwkv_mu, rwkv_w0, rwkv_w_up, rwkv_a0, rwkv_a_up, rwkv_g_up, rwkv_k_k, rwkv_k_a, rwkv_r_k, rwkv_ln_w, rwkv_ln_b, norm_ffn, ffn_w_gate, ffn_w_up, ffn_w_down, moe_router, moe_w_gate, moe_w_up, moe_w_down, norm_final):
    b, t, d = x.shape
    n = b * t
    depth = w_in.shape[0]
    h = x.reshape(n, d)
    for layer in range(depth):
        u = rmsnorm(h, norm_mix[layer], bf16)
        pm = proj(u, w_in, layer, MAIN_COLS, tn=512).reshape(b, t, MAIN_COLS)
        w_tail = _tail_weights(w_in[layer])[None]
        pt = proj(u, w_tail, 0, TAIL_COLS, tn=512).reshape(b, t, TAIL_COLS)

        y_a = na_branch(pm, na_rpb[layer], na_gain[layer])
        qkv = gdn_prep(pm, gdn_conv_w, layer)
        y_b = gdn_branch(pm, pt, qkv, gdn_a_log[layer], gdn_dt_bias[layer], gdn_gain[layer])
        rkv, gate, lw, eta = rwkv_prep(pt, rwkv_mu[layer], rwkv_w0[layer], rwkv_w_up[layer],
                                       rwkv_a0[layer], rwkv_a_up[layer], rwkv_g_up[layer])
        y_c = rwkv_branch(rkv, gate, lw, eta, rwkv_k_k[layer], rwkv_k_a[layer], rwkv_r_k[layer],
                          rwkv_ln_w[layer], rwkv_ln_b[layer])
        h = outproj(y_a.reshape(n, -1), y_b.reshape(n, -1), y_c.reshape(n, -1), w_out, layer, h)

        i = layer // 2
        if layer % 2 == 0:
            u = rmsnorm(h, norm_ffn[layer], bf16)
            act = gateup(u, ffn_w_gate, ffn_w_up, i)
            h = down(act, ffn_w_down[i], h, tk=ffn_w_down.shape[1], tn=256, tm=512)
        else:
            u, combine = rmsnorm_router(h, norm_ffn[layer], moe_router[i])
            act = moe_gateup(u, combine, moe_w_gate, moe_w_up, i)
            ne, ff = moe_w_down.shape[1], moe_w_down.shape[2]
            h = down(act, moe_w_down[i].reshape(ne * ff, d), h, tk=ff, tn=512, tm=1024)
    return rmsnorm(h, norm_final, f32).reshape(b, t, d)
```

```python
import functools

import jax
import jax.numpy as jnp
import numpy as np
from jax import lax
from jax.experimental import pallas as pl
from jax.experimental.pallas import tpu as pltpu

f32 = jnp.float32
bf16 = jnp.bfloat16
HI = lax.Precision.HIGHEST

LANES = 128
VMEM_LIMIT_BYTES = 56 * 1024 * 1024

D_MODEL = 4096
GRID_W = 64
NA_HEADS, NA_HD = 12, 128
NA_WIDTH = NA_HEADS * NA_HD
NA_WIN_ROWS, NA_WIN_COLS = 8, 16
GDN_HEADS, GDN_HD = 12, 128
GDN_WIDTH = GDN_HEADS * GDN_HD
GDN_CONV = 5
CHUNK = 64
RWKV_HEADS, RWKV_HD = 16, 64
RWKV_WIDTH = RWKV_HEADS * RWKV_HD
RWKV_PAIRS = RWKV_HEADS // 2
N_DIRS = 2
MAIN_COLS = 3 * NA_WIDTH + 4 * GDN_WIDTH
RWKV_COLS = 3 * RWKV_WIDTH + 64 + 64 + 128
SHIFT_COLS = RWKV_COLS
assert SHIFT_COLS % LANES == 0
TAIL_COLS = 3584
GATE_COL_BLOCK = SHIFT_COLS // LANES
MOE_EXPERTS = 8
RMS_EPS = 1e-6
L2_EPS = 1e-6
RWKV_GN_EPS = 64e-5
NEG = -1e30


def _cparams(n_axes):
    return pltpu.CompilerParams(dimension_semantics=("arbitrary",) * n_axes,
                                vmem_limit_bytes=VMEM_LIMIT_BYTES)


def _nt(a, b, precision=None):
    return lax.dot_general(a, b, (((1,), (1,)), ((), ())), precision=precision,
                           preferred_element_type=f32)


def _tn(a, b, precision=None):
    return lax.dot_general(a, b, (((0,), (0,)), ((), ())), precision=precision,
                           preferred_element_type=f32)


def _mm(a, b, precision=None):
    return jnp.dot(a, b, precision=precision, preferred_element_type=f32)


def _sigmoid(x):
    return 1.0 / (1.0 + jnp.exp(-x))


def _softplus(x):
    return jnp.maximum(x, 0.0) + jnp.log(1.0 + jnp.exp(-jnp.abs(x)))


def _iota2(shape, axis):
    return lax.broadcasted_iota(jnp.int32, shape, axis)


def _rmsnorm_kernel(x_ref, g_ref, o_ref):
    x = x_ref[...]
    ms = jnp.mean(x * x, axis=-1, keepdims=True)
    o_ref[...] = (x * lax.rsqrt(ms + RMS_EPS) * g_ref[...]).astype(o_ref.dtype)


def rmsnorm(x, gain, out_dtype):
    n, d = x.shape
    tm = min(256, n)
    return pl.pallas_call(
        _rmsnorm_kernel,
        out_shape=jax.ShapeDtypeStruct((n, d), out_dtype),
        grid=(n // tm,),
        in_specs=[pl.BlockSpec((tm, d), lambda i: (i, 0)), pl.BlockSpec((1, d), lambda i: (0, 0))],
        out_specs=pl.BlockSpec((tm, d), lambda i: (i, 0)),
        compiler_params=_cparams(1),
        name="rmsnorm",
    )(x, gain.reshape(1, d))


def _rmsnorm_router_kernel(x_ref, g_ref, r_ref, o_ref, c_ref):
    x = x_ref[...]
    ms = jnp.mean(x * x, axis=-1, keepdims=True)
    u = x * lax.rsqrt(ms + RMS_EPS) * g_ref[...]
    o_ref[...] = u.astype(o_ref.dtype)
    logits = _mm(u, r_ref[...], HI)
    lane = _iota2(logits.shape, 1)
    lg = jnp.where(lane < MOE_EXPERTS, logits, NEG)
    m1 = jnp.max(lg, axis=-1, keepdims=True)
    i1 = jnp.min(jnp.where(lg == m1, lane, LANES), axis=-1, keepdims=True)
    lg2 = jnp.where(lane == i1, NEG, lg)
    m2 = jnp.max(lg2, axis=-1, keepdims=True)
    i2 = jnp.min(jnp.where(lg2 == m2, lane, LANES), axis=-1, keepdims=True)
    e2 = jnp.exp(m2 - m1)
    w1 = 1.0 / (1.0 + e2)
    w2 = e2 / (1.0 + e2)
    c_ref[...] = jnp.where(lane == i1, w1, 0.0) + jnp.where(lane == i2, w2, 0.0)


def rmsnorm_router(x, gain, router):
    n, d = x.shape
    tm = min(256, n)
    r_pad = jnp.pad(router, ((0, 0), (0, LANES - router.shape[1])))
    return pl.pallas_call(
        _rmsnorm_router_kernel,
        out_shape=[jax.ShapeDtypeStruct((n, d), bf16), jax.ShapeDtypeStruct((n, LANES), f32)],
        grid=(n // tm,),
        in_specs=[pl.BlockSpec((tm, d), lambda i: (i, 0)), pl.BlockSpec((1, d), lambda i: (0, 0)),
                  pl.BlockSpec((d, LANES), lambda i: (0, 0))],
        out_specs=[pl.BlockSpec((tm, d), lambda i: (i, 0)), pl.BlockSpec((tm, LANES), lambda i: (i, 0))],
        compiler_params=_cparams(1),
        name="rmsnorm_router",
    )(x, gain.reshape(1, d), r_pad)


def _proj_kernel(x_ref, w_ref, o_ref, wb_ref):
    @pl.when(pl.program_id(1) == 0)
    def _():
        wb_ref[...] = w_ref[...].astype(bf16)

    o_ref[...] = _mm(x_ref[...], wb_ref[...]).astype(o_ref.dtype)


def proj(x, w, w_index, n_cols, tn, out_dtype=f32, tm=1024):
    n, k = x.shape
    tm = min(tm, n)
    assert n_cols % tn == 0 and n % tm == 0
    return pl.pallas_call(
        _proj_kernel,
        out_shape=jax.ShapeDtypeStruct((n, n_cols), out_dtype),
        grid=(n_cols // tn, n // tm),
        in_specs=[pl.BlockSpec((tm, k), lambda j, i: (i, 0)),
                  pl.BlockSpec((None, k, tn), lambda j, i: (w_index, 0, j))],
        out_specs=pl.BlockSpec((tm, tn), lambda j, i: (i, j)),
        scratch_shapes=[pltpu.VMEM((k, tn), bf16)],
        compiler_params=_cparams(2),
        name="proj",
    )(x, w)


def _outproj_kernel(xa_ref, xb_ref, xc_ref, w_ref, res_ref, o_ref, wb_ref):
    @pl.when(pl.program_id(1) == 0)
    def _():
        wb_ref[...] = w_ref[...].astype(bf16)

    acc = _mm(xa_ref[...], wb_ref[0:NA_WIDTH, :])
    acc += _mm(xb_ref[...], wb_ref[NA_WIDTH:NA_WIDTH + GDN_WIDTH, :])
    acc += _mm(xc_ref[...], wb_ref[NA_WIDTH + GDN_WIDTH:, :])
    o_ref[...] = res_ref[...] + acc


def outproj(ya, yb, yc, w, layer, res, tn=512, tm=1024):
    n = ya.shape[0]
    d = w.shape[-1]
    tm = min(tm, n)
    return pl.pallas_call(
        _outproj_kernel,
        out_shape=jax.ShapeDtypeStruct((n, d), f32),
        grid=(d // tn, n // tm),
        in_specs=[pl.BlockSpec((tm, NA_WIDTH), lambda j, i: (i, 0)),
                  pl.BlockSpec((tm, GDN_WIDTH), lambda j, i: (i, 0)),
                  pl.BlockSpec((tm, RWKV_WIDTH), lambda j, i: (i, 0)),
                  pl.BlockSpec((None, D_MODEL, tn), lambda j, i: (layer, 0, j)),
                  pl.BlockSpec((tm, tn), lambda j, i: (i, j))],
        out_specs=pl.BlockSpec((tm, tn), lambda j, i: (i, j)),
        scratch_shapes=[pltpu.VMEM((D_MODEL, tn), bf16)],
        compiler_params=_cparams(2),
        name="outproj",
    )(ya, yb, yc, w, res)


def _gateup_kernel(x_ref, wg_ref, wu_ref, o_ref, wgb_ref, wub_ref):
    @pl.when(pl.program_id(1) == 0)
    def _():
        wgb_ref[...] = wg_ref[...].astype(bf16)
        wub_ref[...] = wu_ref[...].astype(bf16)

    x = x_ref[...]
    g = _mm(x, wgb_ref[...])
    u = _mm(x, wub_ref[...])
    o_ref[...] = (g * _sigmoid(g) * u).astype(o_ref.dtype)


def gateup(x, w_gate, w_up, w_index, tn=256, tm=1024):
    n, k = x.shape
    ff = w_gate.shape[-1]
    tm = min(tm, n)
    return pl.pallas_call(
        _gateup_kernel,
        out_shape=jax.ShapeDtypeStruct((n, ff), bf16),
        grid=(ff // tn, n // tm),
        in_specs=[pl.BlockSpec((tm, k), lambda j, i: (i, 0)),
                  pl.BlockSpec((None, k, tn), lambda j, i: (w_index, 0, j)),
                  pl.BlockSpec((None, k, tn), lambda j, i: (w_index, 0, j))],
        out_specs=pl.BlockSpec((tm, tn), lambda j, i: (i, j)),
        scratch_shapes=[pltpu.VMEM((k, tn), bf16), pltpu.VMEM((k, tn), bf16)],
        compiler_params=_cparams(2),
        name="gateup",
    )(x, w_gate, w_up)


def _moe_gateup_kernel(x_ref, c_ref, wg_ref, wu_ref, o_ref, wgb_ref, wub_ref):
    @pl.when(pl.program_id(2) == 0)
    def _():
        wgb_ref[...] = wg_ref[...].astype(bf16)
        wub_ref[...] = wu_ref[...].astype(bf16)

    e = pl.program_id(0)
    c = c_ref[...]
    ce = jnp.sum(jnp.where(_iota2(c.shape, 1) == e, c, 0.0), axis=-1, keepdims=True)
    x = x_ref[...]
    g = _mm(x, wgb_ref[...])
    u = _mm(x, wub_ref[...])
    o_ref[...] = (g * _sigmoid(g) * u * ce).astype(o_ref.dtype)


def moe_gateup(x, combine, w_gate, w_up, w_index, tn=256, tm=1024):
    n, k = x.shape
    ne, ff = w_gate.shape[1], w_gate.shape[-1]
    tm = min(tm, n)
    nj = ff // tn
    return pl.pallas_call(
        _moe_gateup_kernel,
        out_shape=jax.ShapeDtypeStruct((n, ne * ff), bf16),
        grid=(ne, nj, n // tm),
        in_specs=[pl.BlockSpec((tm, k), lambda e, j, i: (i, 0)),
                  pl.BlockSpec((tm, LANES), lambda e, j, i: (i, 0)),
                  pl.BlockSpec((None, None, k, tn), lambda e, j, i: (w_index, e, 0, j)),
                  pl.BlockSpec((None, None, k, tn), lambda e, j, i: (w_index, e, 0, j))],
        out_specs=pl.BlockSpec((tm, tn), lambda e, j, i: (i, e * nj + j)),
        scratch_shapes=[pltpu.VMEM((k, tn), bf16), pltpu.VMEM((k, tn), bf16)],
        compiler_params=_cparams(3),
        name="moe_gateup",
    )(x, combine, w_gate, w_up)


def _down_kernel(x_ref, w_ref, res_ref, o_ref, acc_ref, *, nk):
    kk = pl.program_id(2)
    part = _mm(x_ref[...], w_ref[...].astype(bf16))

    @pl.when(kk == 0)
    def _():
        acc_ref[...] = res_ref[...] + part

    @pl.when(kk > 0)
    def _():
        acc_ref[...] += part

    @pl.when(kk == nk - 1)
    def _():
        o_ref[...] = acc_ref[...]


def down(x, w, res, tk, tn=256, tm=1024):
    n, kdim = x.shape
    d = w.shape[-1]
    tm = min(tm, n)
    nk = kdim // tk
    assert kdim % tk == 0
    return pl.pallas_call(
        functools.partial(_down_kernel, nk=nk),
        out_shape=jax.ShapeDtypeStruct((n, d), f32),
        grid=(d // tn, n // tm, nk),
        in_specs=[pl.BlockSpec((tm, tk), lambda j, i, k: (i, k)),
                  pl.BlockSpec((tk, tn), lambda j, i, k: (k, j)),
                  pl.BlockSpec((tm, tn), lambda j, i, k: (i, j))],
        out_specs=pl.BlockSpec((tm, tn), lambda j, i, k: (i, j)),
        scratch_shapes=[pltpu.VMEM((tm, tn), f32)],
        compiler_params=_cparams(3),
        name="down",
    )(x, w, res)


def _na_bias_table(rpb):
    c = np.arange(GRID_W)[:, None]
    kc = np.arange(GRID_W)[None, :]
    cs = np.clip(c - NA_WIN_COLS // 2, 0, GRID_W - NA_WIN_COLS)
    valid = (kc >= cs) & (kc < cs + NA_WIN_COLS)
    dc = kc - c + (NA_WIN_COLS - 1)
    onehot = (valid[None] & (dc[None] == np.arange(2 * NA_WIN_COLS - 1)[:, None, None])).astype(np.float32)
    cols = jnp.einsum("hrm,mck->hrck", rpb.astype(f32), jnp.asarray(onehot), precision=HI)
    lo = NA_WIN_ROWS - 1
    tab = jnp.stack([cols[:, lo - off:lo - off + NA_WIN_ROWS] for off in range(NA_WIN_ROWS)], axis=1)
    tab = tab.transpose(0, 1, 3, 2, 4) + jnp.asarray(np.where(valid, 0.0, NEG).astype(np.float32))[:, None, :]
    return tab.reshape(rpb.shape[0], NA_WIN_ROWS, GRID_W, NA_WIN_ROWS * GRID_W)


def _na_kernel(q_ref, k_ref, v_ref, bias_ref, gain_ref, o_ref, kb_ref, vb_ref, *, rows, rows_per_step):
    g = pl.program_id(2)

    @pl.when(g == 0)
    def _():
        kb_ref[...] = k_ref[...].astype(bf16)
        vb_ref[...] = v_ref[...].astype(bf16)

    scale = NA_HD ** -0.5
    win = NA_WIN_ROWS * GRID_W

    def body(i, carry):
        r = g * rows_per_step + i
        sr = jnp.clip(r - NA_WIN_ROWS // 2, 0, rows - NA_WIN_ROWS)
        off = r - sr
        qrows = pl.ds(pl.multiple_of(i * GRID_W, GRID_W), GRID_W)
        krows = pl.ds(pl.multiple_of(sr * GRID_W, GRID_W), win)
        q = q_ref[qrows, :].astype(bf16)
        s = _nt(q, kb_ref[krows, :]) * scale + bias_ref[off]
        m = jnp.max(s, axis=-1, keepdims=True)
        p = jnp.exp(s - m)
        l = jnp.sum(p, axis=-1, keepdims=True)
        p = p / l
        ov = _mm(p.astype(bf16), vb_ref[krows, :])
        ms = jnp.mean(ov * ov, axis=-1, keepdims=True)
        o_ref[qrows, :] = (ov * lax.rsqrt(ms + RMS_EPS) * gain_ref[...]).astype(o_ref.dtype)
        return carry

    lax.fori_loop(0, rows_per_step, body, 0)


def na_branch(pm, rpb, gain, rows_per_step=8):
    b, t, _ = pm.shape
    rows = t // GRID_W
    assert rows >= NA_WIN_ROWS and rows % rows_per_step == 0
    tq = rows_per_step * GRID_W
    bias = _na_bias_table(rpb)
    return pl.pallas_call(
        functools.partial(_na_kernel, rows=rows, rows_per_step=rows_per_step),
        out_shape=jax.ShapeDtypeStruct((b, t, NA_WIDTH), bf16),
        grid=(b, NA_HEADS, rows // rows_per_step),
        in_specs=[pl.BlockSpec((None, tq, NA_HD), lambda bi, h, g: (bi, g, h)),
                  pl.BlockSpec((None, t, NA_HD), lambda bi, h, g: (bi, 0, NA_HEADS + h)),
                  pl.BlockSpec((None, t, NA_HD), lambda bi, h, g: (bi, 0, 2 * NA_HEADS + h)),
                  pl.BlockSpec((None, NA_WIN_ROWS, GRID_W, NA_WIN_ROWS * GRID_W), lambda bi, h, g: (h, 0, 0, 0)),
                  pl.BlockSpec((None, 1, NA_HD), lambda bi, h, g: (h, 0, 0))],
        out_specs=pl.BlockSpec((None, tq, NA_HD), lambda bi, h, g: (bi, g, h)),
        scratch_shapes=[pltpu.VMEM((t, NA_HD), bf16), pltpu.VMEM((t, NA_HD), bf16)],
        compiler_params=_cparams(3),
        name="na_attention",
    )(pm, pm, pm, bias, gain.reshape(NA_HEADS, 1, NA_HD))


def _gdn_prep_kernel(x_ref, w_ref, o_ref, xp_ref, *, t, tc):
    c = pl.program_id(1)
    pad = 8
    xp_ref[0:pad, :] = jnp.zeros((pad, LANES), f32)
    xp_ref[t + pad:t + 2 * pad, :] = jnp.zeros((pad, LANES), f32)
    xp_ref[pad:t + pad, :] = x_ref[...]
    half = GDN_CONV // 2
    for ci in range(t // tc):
        acc = jnp.zeros((tc, LANES), f32)
        for j in range(GDN_CONV):
            s0 = ci * tc + pad - half + j
            acc = acc + xp_ref[s0:s0 + tc, :] * w_ref[j:j + 1, :]
        y = acc * _sigmoid(acc)
        inv = lax.rsqrt(jnp.sum(y * y, axis=-1, keepdims=True) + L2_EPS)
        fac = jnp.where(c < GDN_HEADS, inv * GDN_HD ** -0.5, jnp.where(c < 2 * GDN_HEADS, inv, 1.0))
        o_ref[ci * tc:(ci + 1) * tc, :] = y * fac


def gdn_prep(pm, conv_w, layer):
    b, t, _ = pm.shape
    tc = min(512, t)
    col0 = 3 * NA_WIDTH // LANES
    nblk = 3 * GDN_WIDTH // LANES
    return pl.pallas_call(
        functools.partial(_gdn_prep_kernel, t=t, tc=tc),
        out_shape=jax.ShapeDtypeStruct((b, t, 3 * GDN_WIDTH), f32),
        grid=(b, nblk),
        in_specs=[pl.BlockSpec((None, t, LANES), lambda bi, c: (bi, 0, col0 + c)),
                  pl.BlockSpec((None, GDN_CONV, LANES), lambda bi, c: (layer, 0, c))],
        out_specs=pl.BlockSpec((None, t, LANES), lambda bi, c: (bi, 0, c)),
        scratch_shapes=[pltpu.VMEM((t + 16, LANES), f32)],
        compiler_params=_cparams(2),
        name="gdn_prep",
    )(pm, conv_w)


def _level_masks(ii, jj, size):
    masks = []
    s = 1
    while s < size:
        masks.append(((ii // (2 * s)) == (jj // (2 * s))) & ((ii // s) != (jj // s)))
        s *= 2
    return masks


def _split2(x):
    hi = x.astype(bf16)
    return hi, (x - hi.astype(f32)).astype(bf16)


def _split3(x):
    x1 = x.astype(bf16)
    r = x - x1.astype(f32)
    x2 = r.astype(bf16)
    return x1, x2, (r - x2.astype(f32)).astype(bf16)


def _mm_f32(a, b):
    a1, a2 = _split2(a)
    b1, b2 = _split2(b)
    return _mm(jnp.concatenate([a1, a2, a1], axis=1), jnp.concatenate([b1, b1, b2], axis=0))


def _mm_sel(sel, x):
    x1, x2, x3 = _split3(x)
    return _mm(jnp.concatenate([sel, sel, sel], axis=1), jnp.concatenate([x1, x2, x3], axis=0))


def _mm_rsel(x, sel):
    x1, x2, x3 = _split3(x)
    return _mm(jnp.concatenate([x1, x2, x3], axis=1), jnp.concatenate([sel, sel, sel], axis=0))


def _nt_sel(sel, x):
    x1, x2, x3 = _split3(x)
    return _nt(jnp.concatenate([sel, sel, sel], axis=1), jnp.concatenate([x1, x2, x3], axis=1))


def _tri_inverse(ms, eye, masks):
    xs = [eye - jnp.where(masks[0], m, 0.0) for m in ms]
    for mask in masks[1:]:
        xcs = [_mm_f32(x, jnp.where(mask, m, 0.0)) for x, m in zip(xs, ms)]
        xs = [x - _mm_f32(xc, x) for x, xc in zip(xs, xcs)]
    return xs


def _pair_masks():
    ii = _iota2((LANES, LANES), 0)
    jj = _iota2((LANES, LANES), 1)
    same = (ii // CHUNK) == (jj // CHUNK)
    ti, tj = ii % CHUNK, jj % CHUNK
    return ii, jj, same, ti, tj


def _gdn_gates_kernel(ab_ref, alog_ref, dtb_ref, gt_ref, grow_ref, *, tb):
    ii, jj, same, ti, tj = _pair_masks()
    tri = ((same & (ti >= tj)).astype(bf16), (same & (ti <= tj)).astype(bf16))
    eye = (ii == jj).astype(bf16)
    for p in range(tb // LANES):
        rows = slice(p * LANES, (p + 1) * LANES)
        ab = ab_ref[rows, :]
        g = -jnp.exp(alog_ref[...]) * _softplus(ab + dtb_ref[...])
        fwd = _mm_sel(tri[0], g)
        bwd = _mm_sel(tri[1], g)
        gt = jnp.where(jj < GDN_HEADS, fwd, jnp.where(jj < N_DIRS * GDN_HEADS, bwd, _sigmoid(ab)))
        gt_ref[rows, :] = gt
        grow_ref[p] = _nt_sel(eye, gt)


def gdn_gates(pt, a_log, dt_bias, tb=512):
    b, t, _ = pt.shape
    tb = min(tb, t)
    npad = LANES - N_DIRS * GDN_HEADS
    alog = jnp.pad(a_log.reshape(1, -1), ((0, 0), (0, npad)))
    dtb = jnp.pad(dt_bias.reshape(1, -1), ((0, 0), (0, npad)))
    vec = pl.BlockSpec((1, LANES), lambda bi, i: (0, 0))
    return pl.pallas_call(
        functools.partial(_gdn_gates_kernel, tb=tb),
        out_shape=[jax.ShapeDtypeStruct((b, t, LANES), f32),
                   jax.ShapeDtypeStruct((b, t // LANES, LANES, LANES), f32)],
        grid=(b, t // tb),
        in_specs=[pl.BlockSpec((None, tb, LANES), lambda bi, i: (bi, i, GATE_COL_BLOCK)), vec, vec],
        out_specs=[pl.BlockSpec((None, tb, LANES), lambda bi, i: (bi, i, 0)),
                   pl.BlockSpec((None, tb // LANES, LANES, LANES), lambda bi, i: (bi, i, 0, 0))],
        compiler_params=_cparams(2),
        name="gdn_gates",
    )(pt, alog, dtb)


def _gdn_kernel(q_ref, k_ref, v_ref, z_ref, gt_ref, grow_ref, gain_ref, o_ref,
                u_s, w_s, qg_s, kg_s, a_s, el_s, oacc, *, t, group):
    h = pl.program_id(1)
    npairs = t // LANES
    assert npairs % group == 0 and npairs % 2 == 0
    ii, jj, same, ti, tj = _pair_masks()
    eye = (ii == jj).astype(f32)
    lvl = _level_masks(ti, tj, CHUNK)
    masks = ((same & (ti >= tj), same & (ti > tj)), (same & (ti <= tj), same & (ti < tj)))

    def prep(it, carry):
        tiles = []
        for gi in range(group):
            p = it * group + gi
            rows = pl.ds(pl.multiple_of(p * LANES, LANES), LANES)
            q = q_ref[rows, :]
            k = k_ref[rows, :]
            kb16 = k.astype(bf16)
            tiles.append(dict(p=p, rows=rows, q=q, k=k, v=v_ref[rows, :], gt=gt_ref[rows, :],
                              qk=_nt(q.astype(bf16), kb16), kk=_nt(kb16, kb16)))
        probs = []
        for tl in tiles:
            for d in range(N_DIRS):
                la = d * GDN_HEADS + h
                lb = N_DIRS * GDN_HEADS + la
                incl, strict = masks[d]
                gcol = jnp.sum(jnp.where(jj == la, tl["gt"], 0.0), axis=-1, keepdims=True)
                bcol = jnp.sum(jnp.where(jj == lb, tl["gt"], 0.0), axis=-1, keepdims=True)
                grow = grow_ref[tl["p"], pl.ds(la, 1), :]
                dec = jnp.exp(jnp.where(incl, gcol - grow, NEG))
                probs.append(dict(tl=tl, d=d, gcol=gcol, bcol=bcol, dec=dec,
                                  m=jnp.where(strict, tl["kk"] * bcol * dec, 0.0)))
        tinvs = _tri_inverse([pb["m"] for pb in probs], eye, lvl)
        uws = []
        for pb, tinv in zip(probs, tinvs):
            tl, gcol, bcol = pb["tl"], pb["gcol"], pb["bcol"]
            pb["eg"] = jnp.exp(gcol)
            rhs = jnp.concatenate([tl["v"] * bcol, tl["k"] * (bcol * pb["eg"])], axis=1).astype(bf16)
            t1, t2 = _split2(tinv)
            uws.append(_mm(jnp.concatenate([t1, t2], axis=1), jnp.concatenate([rhs, rhs], axis=0)))
        for pb, uw in zip(probs, uws):
            tl, d, gcol, rows = pb["tl"], pb["d"], pb["gcol"], pb["tl"]["rows"]
            ends = (CHUNK - 1, 2 * CHUNK - 1) if d == 0 else (0, CHUNK)
            gl = [gcol[e:e + 1, :] for e in ends]
            gtot = jnp.concatenate([jnp.broadcast_to(g, (CHUNK, 1)) for g in gl], axis=0)
            a2 = jnp.where(masks[d][0], tl["qk"] * pb["dec"], 0.0)
            u_s[d, rows, :] = uw[:, :GDN_HD]
            w_s[d, rows, :] = uw[:, GDN_HD:].astype(bf16)
            a_s[d, rows, :] = jnp.concatenate([a2[:CHUNK, :CHUNK], a2[CHUNK:, CHUNK:]], axis=0).astype(bf16)
            qg_s[d, rows, :] = (tl["q"] * pb["eg"]).astype(bf16)
            kg_s[d, rows, :] = (tl["k"] * jnp.exp(gtot - gcol)).astype(bf16)
            el_s[d, pl.ds(pl.multiple_of(tl["p"] * 16, 16), 16), :] = jnp.concatenate(
                [jnp.broadcast_to(jnp.exp(g), (8, LANES)) for g in gl], axis=0)
        return carry

    lax.fori_loop(0, npairs // group, prep, 0)
    oacc[...] = jnp.zeros(oacc.shape, f32)

    def scan(i, states):
        prs = (i, npairs - 1 - i)
        halves = ((0, 1), (1, 0))
        s = list(states)
        for step in range(2):
            rows = [pl.ds(pl.multiple_of(prs[d] * LANES + halves[d][step] * CHUNK, CHUNK), CHUNK)
                    for d in range(N_DIRS)]
            sb = [s[d].astype(bf16) for d in range(N_DIRS)]
            ws = [_mm(w_s[d, rows[d], :], sb[d]) for d in range(N_DIRS)]
            qs = [_mm(qg_s[d, rows[d], :], sb[d]) for d in range(N_DIRS)]
            vnb = [(u_s[d, rows[d], :] - ws[d]).astype(bf16) for d in range(N_DIRS)]
            av = [_mm(a_s[d, rows[d], :], vnb[d]) for d in range(N_DIRS)]
            kv = [_tn(kg_s[d, rows[d], :], vnb[d]) for d in range(N_DIRS)]
            for d in range(N_DIRS):
                el = el_s[d, pl.ds(pl.multiple_of(prs[d] * 16 + halves[d][step] * 8, 8), 8), :][0:1, :]
                s[d] = s[d] * el + kv[d]
                oacc[rows[d], :] += qs[d] + av[d]
        return tuple(s)

    s0 = jnp.zeros((GDN_HD, GDN_HD), f32)
    lax.fori_loop(0, npairs, scan, (s0, s0))

    tb = min(512, t)
    for ci in range(t // tb):
        rows = slice(ci * tb, (ci + 1) * tb)
        o = oacc[rows, :]
        z = z_ref[rows, :]
        ms = jnp.mean(o * o, axis=-1, keepdims=True)
        o_ref[rows, :] = (o * lax.rsqrt(ms + RMS_EPS) * gain_ref[...] * (z * _sigmoid(z))).astype(o_ref.dtype)


def gdn_branch(pm, pt, qkv, a_log, dt_bias, gain):
    b, t, _ = pm.shape
    n = t // CHUNK
    assert t % LANES == 0
    zcol0 = (3 * NA_WIDTH + 3 * GDN_WIDTH) // LANES
    gt, grow = gdn_gates(pt, a_log, dt_bias)
    hd = lambda off: (lambda bi, h: (bi, 0, off + h))
    return pl.pallas_call(
        functools.partial(_gdn_kernel, t=t, group=2),
        out_shape=jax.ShapeDtypeStruct((b, t, GDN_WIDTH), bf16),
        grid=(b, GDN_HEADS),
        in_specs=[pl.BlockSpec((None, t, GDN_HD), hd(0)),
                  pl.BlockSpec((None, t, GDN_HD), hd(GDN_HEADS)),
                  pl.BlockSpec((None, t, GDN_HD), hd(2 * GDN_HEADS)),
                  pl.BlockSpec((None, t, GDN_HD), hd(zcol0)),
                  pl.BlockSpec((None, t, LANES), lambda bi, h: (bi, 0, 0)),
                  pl.BlockSpec((None, t // LANES, LANES, LANES), lambda bi, h: (bi, 0, 0, 0)),
                  pl.BlockSpec((1, LANES), lambda bi, h: (0, 0))],
        out_specs=pl.BlockSpec((None, t, GDN_HD), hd(0)),
        scratch_shapes=[pltpu.VMEM((N_DIRS, t, GDN_HD), f32),
                        pltpu.VMEM((N_DIRS, t, GDN_HD), bf16),
                        pltpu.VMEM((N_DIRS, t, GDN_HD), bf16),
                        pltpu.VMEM((N_DIRS, t, GDN_HD), bf16),
                        pltpu.VMEM((N_DIRS, t, CHUNK), bf16),
                        pltpu.VMEM((N_DIRS, n * 8, LANES), f32),
                        pltpu.VMEM((t, GDN_HD), f32)],
        compiler_params=_cparams(2),
        name="gdn_scan",
    )(qkv, qkv, qkv, pm, gt, grow, gain.reshape(1, GDN_HD))


def _rwkv_prep_kernel(cur_ref, prev_ref, next_ref, mu_ref, wup_ref, aup_ref, gup_ref, w0_ref, a0_ref,
                      rkv_ref, gate_ref, lw_ref, eta_ref, buf, *, tb):
    i = pl.program_id(1)
    nt = pl.num_programs(1)
    buf[8:8 + tb, :] = cur_ref[...]
    buf[7:8, :] = jnp.where(i > 0, prev_ref[7:8, :], 0.0)
    buf[8 + tb:9 + tb, :] = jnp.where(i < nt - 1, next_ref[0:1, :], 0.0)

    def mixed(c0, c1):
        x = buf[8:8 + tb, c0:c1]
        sh = 0.5 * (buf[7:7 + tb, c0:c1] + buf[9:9 + tb, c0:c1])
        return x + (sh - x) * mu_ref[:, c0:c1]

    for gi in range(3):
        c0 = gi * RWKV_WIDTH
        rkv_ref[:, c0:c0 + RWKV_WIDTH] = mixed(c0, c0 + RWKV_WIDTH)
    xwa = mixed(3 * RWKV_WIDTH, 3 * RWKV_WIDTH + LANES)
    xg = mixed(3 * RWKV_WIDTH + LANES, 3 * RWKV_WIDTH + 2 * LANES)
    gate_ref[...] = _mm(_sigmoid(xg).astype(bf16), gup_ref[...].astype(bf16))
    txw = jnp.tanh(xwa)
    for d in range(N_DIRS):
        log_w = -_softplus(-(w0_ref[d] + _mm(txw, wup_ref[d], HI))) - 0.5
        lw_ref[d] = -jnp.exp(log_w)
        eta_ref[d] = _sigmoid(a0_ref[d] + _mm(xwa, aup_ref[d], HI))


def rwkv_prep(pt, mu, w0, w_up, a0, a_up, g_up, tb=256):
    b, t, _ = pt.shape
    tb = min(tb, t)
    nb8 = tb // 8
    mu_p = mu.reshape(1, SHIFT_COLS)
    lora = w_up.shape[1]
    wup_p = jnp.pad(w_up, ((0, 0), (0, LANES - lora), (0, 0)))
    aup_p = jnp.pad(a_up, ((0, 0), (lora, LANES - lora - a_up.shape[1]), (0, 0)))
    out_sds = lambda *lead: jax.ShapeDtypeStruct(lead + (b, t, RWKV_WIDTH), f32)
    full = lambda shape: pl.BlockSpec(shape, lambda bi, i: (0,) * len(shape))
    return pl.pallas_call(
        functools.partial(_rwkv_prep_kernel, tb=tb),
        out_shape=[jax.ShapeDtypeStruct((b, t, 3 * RWKV_WIDTH), f32), out_sds(), out_sds(N_DIRS), out_sds(N_DIRS)],
        grid=(b, t // tb),
        in_specs=[pl.BlockSpec((None, tb, SHIFT_COLS), lambda bi, i: (bi, i, 0)),
                  pl.BlockSpec((None, 8, SHIFT_COLS), lambda bi, i: (bi, jnp.maximum(i * nb8 - 1, 0), 0)),
                  pl.BlockSpec((None, 8, SHIFT_COLS), lambda bi, i: (bi, jnp.minimum((i + 1) * nb8, t // 8 - 1), 0)),
                  full((1, SHIFT_COLS)), full((N_DIRS, LANES, RWKV_WIDTH)), full((N_DIRS, LANES, RWKV_WIDTH)),
                  full((LANES, RWKV_WIDTH)), full((N_DIRS, 1, RWKV_WIDTH)), full((N_DIRS, 1, RWKV_WIDTH))],
        out_specs=[pl.BlockSpec((None, tb, 3 * RWKV_WIDTH), lambda bi, i: (bi, i, 0)),
                   pl.BlockSpec((None, tb, RWKV_WIDTH), lambda bi, i: (bi, i, 0)),
                   pl.BlockSpec((N_DIRS, None, tb, RWKV_WIDTH), lambda bi, i: (0, bi, i, 0)),
                   pl.BlockSpec((N_DIRS, None, tb, RWKV_WIDTH), lambda bi, i: (0, bi, i, 0))],
        scratch_shapes=[pltpu.VMEM((tb + 16, SHIFT_COLS), f32)],
        compiler_params=_cparams(2),
        name="rwkv_prep",
    )(pt, pt, pt, mu_p, wup_p, aup_p, g_up, w0.reshape(N_DIRS, 1, -1), a0.reshape(N_DIRS, 1, -1))


def _stack2(x, m0, m1):
    return jnp.concatenate([x * m0, x * m1], axis=0)


def _rwkv_kernel(r_ref, k_ref, v_ref, gate_ref, lw0_ref, lw1_ref, eta0_ref, eta1_ref,
                 kk_ref, ka_ref, rk_ref, lnw_ref, lnb_ref, o_ref,
                 wt_s, uv_s, yv_s, arb_s, rt_s, bh_s, sk_s, eg_s, yacc, kk_s, gc_s, *, t, seg, group):
    n = t // CHUNK
    nseg = n // seg
    assert seg % group == 0 and n % 2 == 0
    c2 = 2 * CHUNK
    lane = _iota2((1, LANES), 1)
    m0 = (lane < RWKV_HD).astype(f32)
    m1 = 1.0 - m0
    ii, jj, same, ti, tj = _pair_masks()
    bd = same.astype(bf16)
    eye = (ii == jj).astype(f32)
    lvl = _level_masks(ti, tj, CHUNK)
    masks = ((same & (ti >= tj), same & (ti > tj)), (same & (ti <= tj), same & (ti < tj)))
    lw_refs = (lw0_ref, lw1_ref)
    eta_refs = (eta0_ref, eta1_ref)
    kkw = kk_ref[...]
    kaw = ka_ref[...]

    def pre(p, carry):
        rows = pl.ds(pl.multiple_of(p * LANES, LANES), LANES)
        kkr = k_ref[rows, :] * kkw
        kk_s[rows, :] = kkr * lax.rsqrt(_mm_rsel(kkr * kkr, bd) + L2_EPS)
        for d in range(N_DIRS):
            gc_s[d, rows, :] = _mm_sel(masks[d][0].astype(bf16), lw_refs[d][rows, :])
        return carry

    lax.fori_loop(0, t // LANES, pre, 0, unroll=2)

    def prep_chunks(problems):
        pbs = []
        for d, c, slot in problems:
            rows = pl.ds(pl.multiple_of(c * CHUNK, CHUNK), CHUNK)
            r = r_ref[rows, :]
            k = k_ref[rows, :]
            lw = lw_refs[d][rows, :]
            eta = eta_refs[d][rows, :]
            kk = kk_s[rows, :]
            kd = k * (1.0 + (eta - 1.0) * kaw)
            bvec = -kk * eta
            gcum = gc_s[d, rows, :]
            gtot = gcum[CHUNK - 1:CHUNK, :] if d == 0 else gcum[0:1, :]
            eneg = jnp.exp(-gcum)
            erem = jnp.exp(gtot - gcum)
            at2 = _stack2(kk * jnp.exp(gcum - lw), m0, m1)
            rt2 = _stack2(r * jnp.exp(gcum), m0, m1)
            kt2 = _stack2(kd * eneg, m0, m1)
            bt2 = _stack2(bvec * eneg, m0, m1)
            pbs.append(dict(d=d, slot=slot, at2=at2, rt2=rt2, gtot=gtot,
                            v2b=_stack2(v_ref[rows, :], m0, m1).astype(bf16),
                            bh=_stack2(bvec * erem, m0, m1).astype(bf16),
                            kh=_stack2(kd * erem, m0, m1).astype(bf16),
                            lhs=jnp.concatenate([at2, rt2], axis=0).astype(bf16),
                            rhs=jnp.concatenate([kt2, bt2], axis=0).astype(bf16)))
        ggs = [_nt(pb["lhs"], pb["rhs"]) for pb in pbs]
        for pb, gg in zip(pbs, ggs):
            incl, strict = masks[pb["d"]]
            pb["a_ak"] = jnp.where(strict, gg[0:c2, 0:c2], 0.0).astype(bf16)
            pb["a_ab"] = jnp.where(strict, gg[0:c2, c2:2 * c2], 0.0)
            pb["a_rk"] = jnp.where(incl, gg[c2:2 * c2, 0:c2], 0.0).astype(bf16)
            pb["a_rb"] = jnp.where(incl, gg[c2:2 * c2, c2:2 * c2], 0.0).astype(bf16)
        akv = [_mm(pb["a_ak"], pb["v2b"]) for pb in pbs]
        yv = [_mm(pb["a_rk"], pb["v2b"]) for pb in pbs]
        sk = [_tn(pb["v2b"], pb["kh"]) for pb in pbs]
        tinvs = _tri_inverse([-pb["a_ab"] for pb in pbs], eye, lvl)
        wus = []
        for pb, tinv, av in zip(pbs, tinvs, akv):
            rhs = jnp.concatenate([pb["at2"], av], axis=1).astype(bf16)
            t1, t2 = _split2(tinv)
            wus.append(_mm(jnp.concatenate([t1, t2], axis=1), jnp.concatenate([rhs, rhs], axis=0)))
        for i, pb in enumerate(pbs):
            d, slot = pb["d"], pb["slot"]
            srows = pl.ds(pl.multiple_of(slot * c2, c2), c2)
            wt_s[d, srows, :] = wus[i][:, :LANES].astype(bf16)
            uv_s[d, srows, :] = wus[i][:, LANES:]
            yv_s[d, srows, :] = yv[i]
            arb_s[d, srows, :] = pb["a_rb"]
            rt_s[d, srows, :] = pb["rt2"].astype(bf16)
            bh_s[d, srows, :] = pb["bh"]
            sk_s[d, srows, :] = sk[i]
            eg_s[d, pl.ds(pl.multiple_of(slot * 8, 8), 8), :] = jnp.broadcast_to(jnp.exp(pb["gtot"]), (8, LANES))

    def steps(cs, slot, states):
        dirs = range(N_DIRS)
        srows = pl.ds(pl.multiple_of(slot * c2, c2), c2)
        sb = [states[d].astype(bf16) for d in dirs]
        ws = [_nt(wt_s[d, srows, :], sb[d]) for d in dirs]
        rs = [_nt(rt_s[d, srows, :], sb[d]) for d in dirs]
        u2b = [(uv_s[d, srows, :] + ws[d]).astype(bf16) for d in dirs]
        au = [_mm(arb_s[d, srows, :], u2b[d]) for d in dirs]
        ub = [_tn(u2b[d], bh_s[d, srows, :]) for d in dirs]
        new = []
        for d in dirs:
            rows = pl.ds(pl.multiple_of(cs[d] * CHUNK, CHUNK), CHUNK)
            y2 = yv_s[d, srows, :] + rs[d] + au[d]
            eg = eg_s[d, pl.ds(pl.multiple_of(slot * 8, 8), 8), :][0:1, :]
            new.append(states[d] * eg + sk_s[d, srows, :] + ub[d])
            yacc[rows, :] += y2[0:CHUNK, :] + y2[CHUNK:c2, :]
        return tuple(new)

    yacc[...] = jnp.zeros(yacc.shape, f32)

    def segment(sg, states):
        def prep(it, carry):
            problems = []
            for gi in range(group):
                j = it * group + gi
                problems += [(0, sg * seg + j, j), (1, n - 1 - (sg * seg + j), j)]
            prep_chunks(problems)
            return carry

        lax.fori_loop(0, seg // group, prep, 0)

        def scan(j, st):
            return steps((sg * seg + j, n - 1 - (sg * seg + j)), j, st)

        return lax.fori_loop(0, seg, scan, states)

    s0 = jnp.zeros((LANES, LANES), f32)
    lax.fori_loop(0, nseg, segment, (s0, s0))

    tb = min(512, t)
    inv_hd = 1.0 / RWKV_HD
    for ci in range(t // tb):
        rows = slice(ci * tb, (ci + 1) * tb)
        y = yacc[rows, :]
        mean = _mm_rsel(y, bd) * inv_hd
        dlt = y - mean
        var = _mm_rsel(dlt * dlt, bd) * inv_hd
        yn = dlt * lax.rsqrt(var + RWKV_GN_EPS) * lnw_ref[...] + lnb_ref[...]
        k = k_ref[rows, :]
        kd_sum = k * (2.0 + (eta0_ref[rows, :] + eta1_ref[rows, :] - 2.0) * kaw)
        bonus = _mm_rsel(r_ref[rows, :] * kd_sum * rk_ref[...], bd) * v_ref[rows, :]
        o_ref[rows, :] = ((yn + bonus) * gate_ref[rows, :]).astype(o_ref.dtype)


def rwkv_branch(rkv, gate, lw, eta, k_k, k_a, r_k, ln_w, ln_b, seg=8):
    b, t, _ = gate.shape
    n = t // CHUNK
    seg = min(seg, n)
    assert n % seg == 0
    c2 = 2 * CHUNK
    blk = lambda off: pl.BlockSpec((None, t, LANES), lambda bi, p: (bi, 0, off + p))
    dblk = lambda d: pl.BlockSpec((None, None, t, LANES), lambda bi, p: (d, bi, 0, p))
    vec = pl.BlockSpec((1, LANES), lambda bi, p: (0, p))
    row = lambda x: x.reshape(1, RWKV_WIDTH)
    seg_f32 = pltpu.VMEM((N_DIRS, seg * c2, LANES), f32)
    seg_b16 = pltpu.VMEM((N_DIRS, seg * c2, LANES), bf16)
    return pl.pallas_call(
        functools.partial(_rwkv_kernel, t=t, seg=seg, group=2),
        out_shape=jax.ShapeDtypeStruct((b, t, RWKV_WIDTH), bf16),
        grid=(b, RWKV_PAIRS),
        in_specs=[blk(0), blk(RWKV_PAIRS), blk(2 * RWKV_PAIRS), blk(0),
                  dblk(0), dblk(1), dblk(0), dblk(1), vec, vec, vec, vec, vec],
        out_specs=blk(0),
        scratch_shapes=[seg_b16,
                        seg_f32,
                        seg_f32,
                        seg_b16,
                        seg_b16,
                        seg_b16,
                        seg_f32,
                        pltpu.VMEM((N_DIRS, seg * 8, LANES), f32),
                        pltpu.VMEM((t, LANES), f32),
                        pltpu.VMEM((t, LANES), f32),
                        pltpu.VMEM((N_DIRS, t, LANES), f32)],
        compiler_params=_cparams(2),
        name="rwkv_scan",
    )(rkv, rkv, rkv, gate, lw, lw, eta, eta, row(k_k), row(k_a), row(r_k), row(ln_w), row(ln_b))


def _tail_weights(w_in_layer):
    gdn_gate = w_in_layer[:, MAIN_COLS:MAIN_COLS + 2 * N_DIRS * GDN_HEADS]
    rwkv = w_in_layer[:, MAIN_COLS + 2 * N_DIRS * GDN_HEADS:]
    pad = jnp.zeros((w_in_layer.shape[0], TAIL_COLS - SHIFT_COLS - gdn_gate.shape[1]), f32)
    return jnp.concatenate([rwkv, gdn_gate, pad], axis=1)


def kernel(x, norm_mix, w_in, w_out, na_rpb, na_gain, gdn_conv_w, gdn_a_log, gdn_dt_bias, gdn_gain, rwkv_mu, rwkv_w0, rwkv_w_up, rwkv_a0, rwkv_a_up, rwkv_g_up, rwkv_k_k, rwkv_k_a, rwkv_r_k, rwkv_ln_w, rwkv_ln_b, norm_ffn, ffn_w_gate, ffn_w_up, ffn_w_down, moe_router, moe_w_gate, moe_w_up, moe_w_down, norm_final):
    b, t, d = x.shape
    n = b * t
    depth = w_in.shape[0]
    h = x.reshape(n, d)
    for layer in range(depth):
        u = rmsnorm(h, norm_mix[layer], bf16)
        pm = proj(u, w_in, layer, MAIN_COLS, tn=512).reshape(b, t, MAIN_COLS)
        w_tail = _tail_weights(w_in[layer])[None]
        pt = proj(u, w_tail, 0, TAIL_COLS, tn=512).reshape(b, t, TAIL_COLS)

        y_a = na_branch(pm, na_rpb[layer], na_gain[layer])
        qkv = gdn_prep(pm, gdn_conv_w, layer)
        y_b = gdn_branch(pm, pt, qkv, gdn_a_log[layer], gdn_dt_bias[layer], gdn_gain[layer])
        rkv, gate, lw, eta = rwkv_prep(pt, rwkv_mu[layer], rwkv_w0[layer], rwkv_w_up[layer],
                                       rwkv_a0[layer], rwkv_a_up[layer], rwkv_g_up[layer])
        y_c = rwkv_branch(rkv, gate, lw, eta, rwkv_k_k[layer], rwkv_k_a[layer], rwkv_r_k[layer],
                          rwkv_ln_w[layer], rwkv_ln_b[layer])
        h = outproj(y_a.reshape(n, -1), y_b.reshape(n, -1), y_c.reshape(n, -1), w_out, layer, h)

        i = layer // 2
        if layer % 2 == 0:
            u = rmsnorm(h, norm_ffn[layer], bf16)
            act = gateup(u, ffn_w_gate, ffn_w_up, i)
            h = down(act, ffn_w_down[i], h, tk=ffn_w_down.shape[1], tn=256, tm=512)
        else:
            u, combine = rmsnorm_router(h, norm_ffn[layer], moe_router[i])
            act = moe_gateup(u, combine, moe_w_gate, moe_w_up, i)
            ne, ff = moe_w_down.shape[1], moe_w_down.shape[2]
            h = down(act, moe_w_down[i].reshape(ne * ff, d), h, tk=ff, tn=512, tm=1024)
    return rmsnorm(h, norm_final, f32).reshape(b, t, d)
```

```python
import functools

import jax
import jax.numpy as jnp
import numpy as np
from jax import lax
from jax.experimental import pallas as pl
from jax.experimental.pallas import tpu as pltpu

f32 = jnp.float32
bf16 = jnp.bfloat16
HI = lax.Precision.HIGHEST

LANES = 128
VMEM_LIMIT_BYTES = 56 * 1024 * 1024

D_MODEL = 4096
GRID_W = 64
NA_HEADS, NA_HD = 12, 128
NA_WIDTH = NA_HEADS * NA_HD
NA_WIN_ROWS, NA_WIN_COLS = 8, 16
GDN_HEADS, GDN_HD = 12, 128
GDN_WIDTH = GDN_HEADS * GDN_HD
GDN_CONV = 5
CHUNK = 64
RWKV_HEADS, RWKV_HD = 16, 64
RWKV_WIDTH = RWKV_HEADS * RWKV_HD
RWKV_PAIRS = RWKV_HEADS // 2
N_DIRS = 2
MAIN_COLS = 3 * NA_WIDTH + 4 * GDN_WIDTH
RWKV_COLS = 3 * RWKV_WIDTH + 64 + 64 + 128
SHIFT_COLS = RWKV_COLS
assert SHIFT_COLS % LANES == 0
TAIL_COLS = 3584
GATE_COL_BLOCK = SHIFT_COLS // LANES
MOE_EXPERTS = 8
RMS_EPS = 1e-6
L2_EPS = 1e-6
RWKV_GN_EPS = 64e-5
NEG = -1e30


def _cparams(n_axes):
    return pltpu.CompilerParams(dimension_semantics=("arbitrary",) * n_axes,
                                vmem_limit_bytes=VMEM_LIMIT_BYTES)


def _nt(a, b, precision=None):
    return lax.dot_general(a, b, (((1,), (1,)), ((), ())), precision=precision,
                           preferred_element_type=f32)


def _tn(a, b, precision=None):
    return lax.dot_general(a, b, (((0,), (0,)), ((), ())), precision=precision,
                           preferred_element_type=f32)


def _mm(a, b, precision=None):
    return jnp.dot(a, b, precision=precision, preferred_element_type=f32)


def _sigmoid(x):
    return 1.0 / (1.0 + jnp.exp(-x))


def _softplus(x):
    return jnp.maximum(x, 0.0) + jnp.log(1.0 + jnp.exp(-jnp.abs(x)))


def _iota2(shape, axis):
    return lax.broadcasted_iota(jnp.int32, shape, axis)


def _rmsnorm_kernel(x_ref, g_ref, o_ref):
    x = x_ref[...]
    ms = jnp.mean(x * x, axis=-1, keepdims=True)
    o_ref[...] = (x * lax.rsqrt(ms + RMS_EPS) * g_ref[...]).astype(o_ref.dtype)


def rmsnorm(x, gain, out_dtype):
    n, d = x.shape
    tm = min(256, n)
    return pl.pallas_call(
        _rmsnorm_kernel,
        out_shape=jax.ShapeDtypeStruct((n, d), out_dtype),
        grid=(n // tm,),
        in_specs=[pl.BlockSpec((tm, d), lambda i: (i, 0)), pl.BlockSpec((1, d), lambda i: (0, 0))],
        out_specs=pl.BlockSpec((tm, d), lambda i: (i, 0)),
        compiler_params=_cparams(1),
        name="rmsnorm",
    )(x, gain.reshape(1, d))


def _rmsnorm_router_kernel(x_ref, g_ref, r_ref, o_ref, e_ref, w_ref):
    x = x_ref[...]
    ms = jnp.mean(x * x, axis=-1, keepdims=True)
    u = x * lax.rsqrt(ms + RMS_EPS) * g_ref[...]
    o_ref[...] = u
    logits = _mm(u, r_ref[...], HI)
    lane = _iota2(logits.shape, 1)
    lg = jnp.where(lane < MOE_EXPERTS, logits, NEG)
    m1 = jnp.max(lg, axis=-1, keepdims=True)
    i1 = jnp.min(jnp.where(lg == m1, lane, LANES), axis=-1, keepdims=True)
    lg2 = jnp.where(lane == i1, NEG, lg)
    m2 = jnp.max(lg2, axis=-1, keepdims=True)
    i2 = jnp.min(jnp.where(lg2 == m2, lane, LANES), axis=-1, keepdims=True)
    e2 = jnp.exp(m2 - m1)
    w1 = 1.0 / (1.0 + e2)
    w2 = e2 / (1.0 + e2)
    e_ref[...] = jnp.where(lane == 0, i1, jnp.where(lane == 1, i2, 0))
    w_ref[...] = jnp.where(lane == 0, w1, jnp.where(lane == 1, w2, 0.0))


def rmsnorm_router(x, gain, router):
    n, d = x.shape
    tm = min(256, n)
    r_pad = jnp.pad(router, ((0, 0), (0, LANES - router.shape[1])))
    return pl.pallas_call(
        _rmsnorm_router_kernel,
        out_shape=[jax.ShapeDtypeStruct((n, d), f32), jax.ShapeDtypeStruct((n, LANES), jnp.int32),
                   jax.ShapeDtypeStruct((n, LANES), f32)],
        grid=(n // tm,),
        in_specs=[pl.BlockSpec((tm, d), lambda i: (i, 0)), pl.BlockSpec((1, d), lambda i: (0, 0)),
                  pl.BlockSpec((d, LANES), lambda i: (0, 0))],
        out_specs=[pl.BlockSpec((tm, d), lambda i: (i, 0)), pl.BlockSpec((tm, LANES), lambda i: (i, 0)),
                   pl.BlockSpec((tm, LANES), lambda i: (i, 0))],
        compiler_params=_cparams(1),
        name="rmsnorm_router",
    )(x, gain.reshape(1, d), r_pad)


def _moe_rank_kernel(e_ref, rank_ref, cnt_ref, carry, *, tb):
    @pl.when(pl.program_id(0) == 0)
    def _():
        carry[...] = jnp.zeros(carry.shape, f32)

    ids = e_ref[...]
    lane = _iota2(ids.shape, 1)
    oh1 = lane == ids[:, 0:1]
    oh2 = lane == ids[:, 1:2]
    f1 = jnp.where(oh1, 1.0, 0.0)
    f2 = jnp.where(oh2, 1.0, 0.0)
    earlier = jnp.where(_iota2((tb, tb), 0) > _iota2((tb, tb), 1), 1.0, 0.0).astype(bf16)
    p1 = _mm(earlier, f1.astype(bf16))
    p2 = _mm(earlier, f2.astype(bf16))
    c1 = jnp.sum(f1, axis=0, keepdims=True)
    c2 = jnp.sum(f2, axis=0, keepdims=True)
    base = carry[0:1, :]
    r1 = jnp.sum(jnp.where(oh1, p1 + base, 0.0), axis=-1, keepdims=True)
    r2 = jnp.sum(jnp.where(oh2, p2 + (base + c1), 0.0), axis=-1, keepdims=True)
    rank_ref[...] = jnp.where(lane == 0, r1, jnp.where(lane == 1, r2, 0.0))
    carry[...] = jnp.broadcast_to(base + c1 + c2, carry.shape)
    cnt_ref[...] = carry[...]


def moe_rank(ids, tb=256):
    n = ids.shape[0]
    tb = min(tb, n)
    return pl.pallas_call(
        functools.partial(_moe_rank_kernel, tb=tb),
        out_shape=[jax.ShapeDtypeStruct((n, LANES), f32), jax.ShapeDtypeStruct((8, LANES), f32)],
        grid=(n // tb,),
        in_specs=[pl.BlockSpec((tb, LANES), lambda i: (i, 0))],
        out_specs=[pl.BlockSpec((tb, LANES), lambda i: (i, 0)), pl.BlockSpec((8, LANES), lambda i: (0, 0))],
        scratch_shapes=[pltpu.VMEM((8, LANES), f32)],
        compiler_params=_cparams(1),
        name="moe_rank",
    )(ids)


def _gather_rows_kernel(idx_ref, src_ref, o_ref, buf, sem, *, tm):
    base = pl.program_id(0) * tm

    def row_copy(r, row):
        return pltpu.make_async_copy(src_ref.at[pl.ds(row, 1), :], buf.at[pl.ds(r, 1), :], sem)

    def start(r, c):
        row_copy(r, idx_ref[base + r]).start()
        return c

    def wait(r, c):
        row_copy(r, 0).wait()
        return c

    lax.fori_loop(0, tm, start, 0)
    lax.fori_loop(0, tm, wait, 0)
    o_ref[...] = buf[...].astype(o_ref.dtype)


def gather_rows(src, idx, out_dtype, tm=256):
    m = idx.shape[0]
    d = src.shape[1]
    tm = min(tm, m)
    assert m % tm == 0
    return pl.pallas_call(
        functools.partial(_gather_rows_kernel, tm=tm),
        out_shape=jax.ShapeDtypeStruct((m, d), out_dtype),
        grid_spec=pltpu.PrefetchScalarGridSpec(
            num_scalar_prefetch=1, grid=(m // tm,),
            in_specs=[pl.BlockSpec(memory_space=pl.ANY)],
            out_specs=pl.BlockSpec((tm, d), lambda i, idx_ref: (i, 0)),
            scratch_shapes=[pltpu.VMEM((tm, d), f32), pltpu.SemaphoreType.DMA(())]),
        compiler_params=_cparams(1),
        name="gather_rows",
    )(idx, src)


def _moe_combine_kernel(d1_ref, d2_ref, y_ref, h_ref, w_ref, o_ref, buf, sems, *, tm):
    base = pl.program_id(0) * tm
    dests = (d1_ref, d2_ref)

    def row_copy(k, r, row):
        return pltpu.make_async_copy(y_ref.at[pl.ds(row, 1), :], buf.at[k, pl.ds(r, 1), :], sems.at[k])

    def start(r, c):
        for k in range(2):
            row_copy(k, r, dests[k][base + r]).start()
        return c

    def wait(r, c):
        for k in range(2):
            row_copy(k, r, 0).wait()
        return c

    lax.fori_loop(0, tm, start, 0)
    lax.fori_loop(0, tm, wait, 0)
    w = w_ref[...]
    o_ref[...] = h_ref[...] + w[:, 0:1] * buf[0] + w[:, 1:2] * buf[1]


def moe_combine(y, d1, d2, h, wts, tm=256):
    n, d = h.shape
    tm = min(tm, n)
    return pl.pallas_call(
        functools.partial(_moe_combine_kernel, tm=tm),
        out_shape=jax.ShapeDtypeStruct((n, d), f32),
        grid_spec=pltpu.PrefetchScalarGridSpec(
            num_scalar_prefetch=2, grid=(n // tm,),
            in_specs=[pl.BlockSpec(memory_space=pl.ANY),
                      pl.BlockSpec((tm, d), lambda i, a, b: (i, 0)),
                      pl.BlockSpec((tm, LANES), lambda i, a, b: (i, 0))],
            out_specs=pl.BlockSpec((tm, d), lambda i, a, b: (i, 0)),
            scratch_shapes=[pltpu.VMEM((2, tm, d), f32), pltpu.SemaphoreType.DMA((2,))]),
        compiler_params=_cparams(1),
        name="moe_combine",
    )(d1, d2, y, h, wts)


def _proj_kernel(x_ref, w_ref, o_ref, wb_ref):
    @pl.when(pl.program_id(1) == 0)
    def _():
        wb_ref[...] = w_ref[...].astype(bf16)

    o_ref[...] = _mm(x_ref[...], wb_ref[...]).astype(o_ref.dtype)


def proj(x, w, w_index, n_cols, tn, out_dtype=f32, tm=1024):
    n, k = x.shape
    tm = min(tm, n)
    assert n_cols % tn == 0 and n % tm == 0
    return pl.pallas_call(
        _proj_kernel,
        out_shape=jax.ShapeDtypeStruct((n, n_cols), out_dtype),
        grid=(n_cols // tn, n // tm),
        in_specs=[pl.BlockSpec((tm, k), lambda j, i: (i, 0)),
                  pl.BlockSpec((None, k, tn), lambda j, i: (w_index, 0, j))],
        out_specs=pl.BlockSpec((tm, tn), lambda j, i: (i, j)),
        scratch_shapes=[pltpu.VMEM((k, tn), bf16)],
        compiler_params=_cparams(2),
        name="proj",
    )(x, w)


def _outproj_kernel(xa_ref, xb_ref, xc_ref, w_ref, res_ref, o_ref, wb_ref):
    @pl.when(pl.program_id(1) == 0)
    def _():
        wb_ref[...] = w_ref[...].astype(bf16)

    acc = _mm(xa_ref[...], wb_ref[0:NA_WIDTH, :])
    acc += _mm(xb_ref[...], wb_ref[NA_WIDTH:NA_WIDTH + GDN_WIDTH, :])
    acc += _mm(xc_ref[...], wb_ref[NA_WIDTH + GDN_WIDTH:, :])
    o_ref[...] = res_ref[...] + acc


def outproj(ya, yb, yc, w, layer, res, tn=512, tm=1024):
    n = ya.shape[0]
    d = w.shape[-1]
    tm = min(tm, n)
    return pl.pallas_call(
        _outproj_kernel,
        out_shape=jax.ShapeDtypeStruct((n, d), f32),
        grid=(d // tn, n // tm),
        in_specs=[pl.BlockSpec((tm, NA_WIDTH), lambda j, i: (i, 0)),
                  pl.BlockSpec((tm, GDN_WIDTH), lambda j, i: (i, 0)),
                  pl.BlockSpec((tm, RWKV_WIDTH), lambda j, i: (i, 0)),
                  pl.BlockSpec((None, D_MODEL, tn), lambda j, i: (layer, 0, j)),
                  pl.BlockSpec((tm, tn), lambda j, i: (i, j))],
        out_specs=pl.BlockSpec((tm, tn), lambda j, i: (i, j)),
        scratch_shapes=[pltpu.VMEM((D_MODEL, tn), bf16)],
        compiler_params=_cparams(2),
        name="outproj",
    )(ya, yb, yc, w, res)


def _gateup_kernel(x_ref, wg_ref, wu_ref, o_ref, wgb_ref, wub_ref):
    @pl.when(pl.program_id(1) == 0)
    def _():
        wgb_ref[...] = wg_ref[...].astype(bf16)
        wub_ref[...] = wu_ref[...].astype(bf16)

    x = x_ref[...]
    g = _mm(x, wgb_ref[...])
    u = _mm(x, wub_ref[...])
    o_ref[...] = (g * _sigmoid(g) * u).astype(o_ref.dtype)


def gateup(x, w_gate, w_up, w_index, tn=256, tm=1024):
    n, k = x.shape
    ff = w_gate.shape[-1]
    tm = min(tm, n)
    return pl.pallas_call(
        _gateup_kernel,
        out_shape=jax.ShapeDtypeStruct((n, ff), bf16),
        grid=(ff // tn, n // tm),
        in_specs=[pl.BlockSpec((tm, k), lambda j, i: (i, 0)),
                  pl.BlockSpec((None, k, tn), lambda j, i: (w_index, 0, j)),
                  pl.BlockSpec((None, k, tn), lambda j, i: (w_index, 0, j))],
        out_specs=pl.BlockSpec((tm, tn), lambda j, i: (i, j)),
        scratch_shapes=[pltpu.VMEM((k, tn), bf16), pltpu.VMEM((k, tn), bf16)],
        compiler_params=_cparams(2),
        name="gateup",
    )(x, w_gate, w_up)


def _tile_changed(te_ref, i):
    return (i == 0) | (te_ref[i] != te_ref[jnp.maximum(i - 1, 0)])


def _moe_gateup_kernel(te_ref, nu_ref, x_ref, wg_ref, wu_ref, o_ref, wgb_ref, wub_ref):
    i = pl.program_id(1)
    used = i < nu_ref[0]

    @pl.when(used & _tile_changed(te_ref, i))
    def _():
        wgb_ref[...] = wg_ref[...].astype(bf16)
        wub_ref[...] = wu_ref[...].astype(bf16)

    @pl.when(used)
    def _():
        x = x_ref[...]
        g = _mm(x, wgb_ref[...])
        u = _mm(x, wub_ref[...])
        o_ref[...] = (g * _sigmoid(g) * u).astype(o_ref.dtype)

    @pl.when(jnp.logical_not(used))
    def _():
        o_ref[...] = jnp.zeros(o_ref.shape, o_ref.dtype)


def _moe_down_kernel(te_ref, nu_ref, x_ref, w_ref, o_ref, wb_ref):
    i = pl.program_id(1)
    used = i < nu_ref[0]

    @pl.when(used & _tile_changed(te_ref, i))
    def _():
        wb_ref[...] = w_ref[...].astype(bf16)

    @pl.when(used)
    def _():
        o_ref[...] = _mm(x_ref[...], wb_ref[...])

    @pl.when(jnp.logical_not(used))
    def _():
        o_ref[...] = jnp.zeros(o_ref.shape, o_ref.dtype)


def _grouped_specs(tm, k, tn, w_index, n_w):
    last = lambda i, nu: jnp.minimum(i, nu[0] - 1)
    x_spec = pl.BlockSpec((tm, k), lambda j, i, te, nu: (last(i, nu), 0))
    w_spec = pl.BlockSpec((None, None, k, tn), lambda j, i, te, nu: (w_index, te[last(i, nu)], 0, j))
    o_spec = pl.BlockSpec((tm, tn), lambda j, i, te, nu: (i, j))
    return [x_spec] + [w_spec] * n_w, o_spec


def moe_gateup(x, tile_expert, n_used, w_gate, w_up, w_index, tm, tn=512):
    m, k = x.shape
    ff = w_gate.shape[-1]
    in_specs, out_spec = _grouped_specs(tm, k, tn, w_index, 2)
    return pl.pallas_call(
        _moe_gateup_kernel,
        out_shape=jax.ShapeDtypeStruct((m, ff), bf16),
        grid_spec=pltpu.PrefetchScalarGridSpec(
            num_scalar_prefetch=2, grid=(ff // tn, m // tm), in_specs=in_specs, out_specs=out_spec,
            scratch_shapes=[pltpu.VMEM((k, tn), bf16), pltpu.VMEM((k, tn), bf16)]),
        compiler_params=_cparams(2),
        name="moe_gateup",
    )(tile_expert, n_used, x, w_gate, w_up)


def moe_down(x, tile_expert, n_used, w_down, w_index, tm, tn=512):
    m, k = x.shape
    d = w_down.shape[-1]
    in_specs, out_spec = _grouped_specs(tm, k, tn, w_index, 1)
    return pl.pallas_call(
        _moe_down_kernel,
        out_shape=jax.ShapeDtypeStruct((m, d), f32),
        grid_spec=pltpu.PrefetchScalarGridSpec(
            num_scalar_prefetch=2, grid=(d // tn, m // tm), in_specs=in_specs, out_specs=out_spec,
            scratch_shapes=[pltpu.VMEM((k, tn), bf16)]),
        compiler_params=_cparams(2),
        name="moe_down",
    )(tile_expert, n_used, x, w_down)


def moe_ffn(h, gain, router, w_gate, w_up, w_down, w_index, tm=512):
    n, d = h.shape
    ne = w_gate.shape[1]
    tm = min(tm, n)
    u, ids, wts = rmsnorm_router(h, gain, router)
    rank, counts = moe_rank(ids)
    counts = counts[0, :ne].astype(jnp.int32)
    padded = (counts + tm - 1) // tm * tm
    ends = jnp.cumsum(padded)
    e2 = ids[:, :2]
    dest = (ends - padded)[e2] + rank[:, :2].astype(jnp.int32)
    n_tiles = 2 * n // tm + ne
    tile_expert = jnp.minimum(jnp.searchsorted(ends, jnp.arange(n_tiles) * tm, side="right"), ne - 1)
    n_used = (ends[-1] // tm).reshape(1)
    token = jnp.broadcast_to(jnp.arange(n, dtype=jnp.int32)[:, None], (n, 2))
    src = jnp.zeros((n_tiles * tm,), jnp.int32).at[dest.reshape(-1)].set(token.reshape(-1))
    xs = gather_rows(u, src, bf16)
    act = moe_gateup(xs, tile_expert.astype(jnp.int32), n_used.astype(jnp.int32), w_gate, w_up, w_index, tm)
    ys = moe_down(act, tile_expert.astype(jnp.int32), n_used.astype(jnp.int32), w_down, w_index, tm)
    return moe_combine(ys, dest[:, 0], dest[:, 1], h, wts)


def _down_kernel(x_ref, w_ref, res_ref, o_ref, wb_ref):
    @pl.when(pl.program_id(1) == 0)
    def _():
        wb_ref[...] = w_ref[...].astype(bf16)

    o_ref[...] = res_ref[...] + _mm(x_ref[...], wb_ref[...])


def down(x, w, w_index, res, tn=256, tm=256):
    n, kdim = x.shape
    d = w.shape[-1]
    tm = min(tm, n)
    return pl.pallas_call(
        _down_kernel,
        out_shape=jax.ShapeDtypeStruct((n, d), f32),
        grid=(d // tn, n // tm),
        in_specs=[pl.BlockSpec((tm, kdim), lambda j, i: (i, 0)),
                  pl.BlockSpec((None, kdim, tn), lambda j, i: (w_index, 0, j)),
                  pl.BlockSpec((tm, tn), lambda j, i: (i, j))],
        out_specs=pl.BlockSpec((tm, tn), lambda j, i: (i, j)),
        scratch_shapes=[pltpu.VMEM((kdim, tn), bf16)],
        compiler_params=_cparams(2),
        name="down",
    )(x, w, res)


def _na_bias_table(rpb):
    c = np.arange(GRID_W)[:, None]
    kc = np.arange(GRID_W)[None, :]
    cs = np.clip(c - NA_WIN_COLS // 2, 0, GRID_W - NA_WIN_COLS)
    valid = (kc >= cs) & (kc < cs + NA_WIN_COLS)
    dc = kc - c + (NA_WIN_COLS - 1)
    onehot = (valid[None] & (dc[None] == np.arange(2 * NA_WIN_COLS - 1)[:, None, None])).astype(np.float32)
    cols = jnp.einsum("hrm,mck->hrck", rpb.astype(f32), jnp.asarray(onehot), precision=HI)
    lo = NA_WIN_ROWS - 1
    tab = jnp.stack([cols[:, lo - off:lo - off + NA_WIN_ROWS] for off in range(NA_WIN_ROWS)], axis=1)
    tab = tab.transpose(0, 1, 3, 2, 4) + jnp.asarray(np.where(valid, 0.0, NEG).astype(np.float32))[:, None, :]
    return tab.reshape(rpb.shape[0], NA_WIN_ROWS, GRID_W, NA_WIN_ROWS * GRID_W)


def _na_kernel(q_ref, k_ref, v_ref, bias_ref, gain_ref, o_ref, kb_ref, vb_ref, *, rows, rows_per_step, lockstep):
    g = pl.program_id(2)

    @pl.when(g == 0)
    def _():
        kb_ref[...] = k_ref[...].astype(bf16)
        vb_ref[...] = v_ref[...].astype(bf16)

    scale = NA_HD ** -0.5
    win = NA_WIN_ROWS * GRID_W

    def body(it, carry):
        qrows, krows, ss = [], [], []
        for li in range(lockstep):
            i = it * lockstep + li
            r = g * rows_per_step + i
            sr = jnp.clip(r - NA_WIN_ROWS // 2, 0, rows - NA_WIN_ROWS)
            qrows.append(pl.ds(pl.multiple_of(i * GRID_W, GRID_W), GRID_W))
            krows.append(pl.ds(pl.multiple_of(sr * GRID_W, GRID_W), win))
            q = q_ref[qrows[li], :].astype(bf16)
            ss.append(_nt(q, kb_ref[krows[li], :]) * scale + bias_ref[r - sr])
        ps, ls = [], []
        for s in ss:
            p = jnp.exp(s - jnp.max(s, axis=-1, keepdims=True))
            ls.append(jnp.sum(p, axis=-1, keepdims=True))
            ps.append(p)
        ovs = [_mm((p / l).astype(bf16), vb_ref[kr, :]) for p, l, kr in zip(ps, ls, krows)]
        for ov, qr in zip(ovs, qrows):
            ms = jnp.mean(ov * ov, axis=-1, keepdims=True)
            o_ref[qr, :] = (ov * lax.rsqrt(ms + RMS_EPS) * gain_ref[...]).astype(o_ref.dtype)
        return carry

    lax.fori_loop(0, rows_per_step // lockstep, body, 0)


def na_branch(pm, rpb, gain, rows_per_step=8):
    b, t, _ = pm.shape
    rows = t // GRID_W
    assert rows >= NA_WIN_ROWS and rows % rows_per_step == 0
    tq = rows_per_step * GRID_W
    bias = _na_bias_table(rpb)
    return pl.pallas_call(
        functools.partial(_na_kernel, rows=rows, rows_per_step=rows_per_step, lockstep=4),
        out_shape=jax.ShapeDtypeStruct((b, t, NA_WIDTH), bf16),
        grid=(b, NA_HEADS, rows // rows_per_step),
        in_specs=[pl.BlockSpec((None, tq, NA_HD), lambda bi, h, g: (bi, g, h)),
                  pl.BlockSpec((None, t, NA_HD), lambda bi, h, g: (bi, 0, NA_HEADS + h)),
                  pl.BlockSpec((None, t, NA_HD), lambda bi, h, g: (bi, 0, 2 * NA_HEADS + h)),
                  pl.BlockSpec((None, NA_WIN_ROWS, GRID_W, NA_WIN_ROWS * GRID_W), lambda bi, h, g: (h, 0, 0, 0)),
                  pl.BlockSpec((None, 1, NA_HD), lambda bi, h, g: (h, 0, 0))],
        out_specs=pl.BlockSpec((None, tq, NA_HD), lambda bi, h, g: (bi, g, h)),
        scratch_shapes=[pltpu.VMEM((t, NA_HD), bf16), pltpu.VMEM((t, NA_HD), bf16)],
        compiler_params=_cparams(3),
        name="na_attention",
    )(pm, pm, pm, bias, gain.reshape(NA_HEADS, 1, NA_HD))


def _gdn_prep_kernel(x_ref, w_ref, o_ref, xp_ref, *, t, tc):
    c = pl.program_id(1)
    pad = 8
    xp_ref[0:pad, :] = jnp.zeros((pad, LANES), f32)
    xp_ref[t + pad:t + 2 * pad, :] = jnp.zeros((pad, LANES), f32)
    xp_ref[pad:t + pad, :] = x_ref[...]
    half = GDN_CONV // 2
    for ci in range(t // tc):
        acc = jnp.zeros((tc, LANES), f32)
        for j in range(GDN_CONV):
            s0 = ci * tc + pad - half + j
            acc = acc + xp_ref[s0:s0 + tc, :] * w_ref[j:j + 1, :]
        y = acc * _sigmoid(acc)
        inv = lax.rsqrt(jnp.sum(y * y, axis=-1, keepdims=True) + L2_EPS)
        fac = jnp.where(c < GDN_HEADS, inv * GDN_HD ** -0.5, jnp.where(c < 2 * GDN_HEADS, inv, 1.0))
        o_ref[ci * tc:(ci + 1) * tc, :] = y * fac


def gdn_prep(pm, conv_w, layer):
    b, t, _ = pm.shape
    tc = min(512, t)
    col0 = 3 * NA_WIDTH // LANES
    nblk = 3 * GDN_WIDTH // LANES
    return pl.pallas_call(
        functools.partial(_gdn_prep_kernel, t=t, tc=tc),
        out_shape=jax.ShapeDtypeStruct((b, t, 3 * GDN_WIDTH), f32),
        grid=(b, nblk),
        in_specs=[pl.BlockSpec((None, t, LANES), lambda bi, c: (bi, 0, col0 + c)),
                  pl.BlockSpec((None, GDN_CONV, LANES), lambda bi, c: (layer, 0, c))],
        out_specs=pl.BlockSpec((None, t, LANES), lambda bi, c: (bi, 0, c)),
        scratch_shapes=[pltpu.VMEM((t + 16, LANES), f32)],
        compiler_params=_cparams(2),
        name="gdn_prep",
    )(pm, conv_w)


def _level_masks(ii, jj, size):
    masks = []
    s = 1
    while s < size:
        masks.append(((ii // (2 * s)) == (jj // (2 * s))) & ((ii // s) != (jj // s)))
        s *= 2
    return masks


def _split2(x):
    hi = x.astype(bf16)
    return hi, (x - hi.astype(f32)).astype(bf16)


def _split3(x):
    x1 = x.astype(bf16)
    r = x - x1.astype(f32)
    x2 = r.astype(bf16)
    return x1, x2, (r - x2.astype(f32)).astype(bf16)


def _mm_f32(a, b):
    a1, a2 = _split2(a)
    b1, b2 = _split2(b)
    return _mm(jnp.concatenate([a1, a2, a1], axis=1), jnp.concatenate([b1, b1, b2], axis=0))


def _mm_sel(sel, x):
    x1, x2, x3 = _split3(x)
    return _mm(jnp.concatenate([sel, sel, sel], axis=1), jnp.concatenate([x1, x2, x3], axis=0))


def _mm_rsel(x, sel):
    x1, x2, x3 = _split3(x)
    return _mm(jnp.concatenate([x1, x2, x3], axis=1), jnp.concatenate([sel, sel, sel], axis=0))


def _nt_sel(sel, x):
    x1, x2, x3 = _split3(x)
    return _nt(jnp.concatenate([sel, sel, sel], axis=1), jnp.concatenate([x1, x2, x3], axis=1))


def _tri_inverse(ms, eye, masks):
    xs = [eye - jnp.where(masks[0], m, 0.0) for m in ms]
    for mask in masks[1:]:
        xcs = [_mm_f32(x, jnp.where(mask, m, 0.0)) for x, m in zip(xs, ms)]
        xs = [x - _mm_f32(xc, x) for x, xc in zip(xs, xcs)]
    return xs


def _pair_masks():
    ii = _iota2((LANES, LANES), 0)
    jj = _iota2((LANES, LANES), 1)
    same = (ii // CHUNK) == (jj // CHUNK)
    ti, tj = ii % CHUNK, jj % CHUNK
    return ii, jj, same, ti, tj


def _gdn_gates_kernel(ab_ref, alog_ref, dtb_ref, gt_ref, grow_ref, *, tb):
    ii, jj, same, ti, tj = _pair_masks()
    tri = ((same & (ti >= tj)).astype(bf16), (same & (ti <= tj)).astype(bf16))
    eye = (ii == jj).astype(bf16)
    for p in range(tb // LANES):
        rows = slice(p * LANES, (p + 1) * LANES)
        ab = ab_ref[rows, :]
        g = -jnp.exp(alog_ref[...]) * _softplus(ab + dtb_ref[...])
        fwd = _mm_sel(tri[0], g)
        bwd = _mm_sel(tri[1], g)
        gt = jnp.where(jj < GDN_HEADS, fwd, jnp.where(jj < N_DIRS * GDN_HEADS, bwd, _sigmoid(ab)))
        gt_ref[rows, :] = gt
        grow_ref[p] = _nt_sel(eye, gt)


def gdn_gates(pt, a_log, dt_bias, tb=512):
    b, t, _ = pt.shape
    tb = min(tb, t)
    npad = LANES - N_DIRS * GDN_HEADS
    alog = jnp.pad(a_log.reshape(1, -1), ((0, 0), (0, npad)))
    dtb = jnp.pad(dt_bias.reshape(1, -1), ((0, 0), (0, npad)))
    vec = pl.BlockSpec((1, LANES), lambda bi, i: (0, 0))
    return pl.pallas_call(
        functools.partial(_gdn_gates_kernel, tb=tb),
        out_shape=[jax.ShapeDtypeStruct((b, t, LANES), f32),
                   jax.ShapeDtypeStruct((b, t // LANES, LANES, LANES), f32)],
        grid=(b, t // tb),
        in_specs=[pl.BlockSpec((None, tb, LANES), lambda bi, i: (bi, i, GATE_COL_BLOCK)), vec, vec],
        out_specs=[pl.BlockSpec((None, tb, LANES), lambda bi, i: (bi, i, 0)),
                   pl.BlockSpec((None, tb // LANES, LANES, LANES), lambda bi, i: (bi, i, 0, 0))],
        compiler_params=_cparams(2),
        name="gdn_gates",
    )(pt, alog, dtb)


def _gdn_kernel(q_ref, k_ref, v_ref, z_ref, gt_ref, grow_ref, gain_ref, o_ref,
                u_s, w_s, qg_s, kg_s, a_s, el_s, oacc, *, t, group):
    h = pl.program_id(1)
    npairs = t // LANES
    assert npairs % group == 0 and npairs % 2 == 0
    ii, jj, same, ti, tj = _pair_masks()
    eye = (ii == jj).astype(f32)
    lvl = _level_masks(ti, tj, CHUNK)
    masks = ((same & (ti >= tj), same & (ti > tj)), (same & (ti <= tj), same & (ti < tj)))

    def prep(it, carry):
        tiles = []
        for gi in range(group):
            p = it * group + gi
            rows = pl.ds(pl.multiple_of(p * LANES, LANES), LANES)
            q = q_ref[rows, :]
            k = k_ref[rows, :]
            kb16 = k.astype(bf16)
            tiles.append(dict(p=p, rows=rows, q=q, k=k, v=v_ref[rows, :], gt=gt_ref[rows, :],
                              qk=_nt(q.astype(bf16), kb16), kk=_nt(kb16, kb16)))
        probs = []
        for tl in tiles:
            for d in range(N_DIRS):
                la = d * GDN_HEADS + h
                lb = N_DIRS * GDN_HEADS + la
                incl, strict = masks[d]
                gcol = jnp.sum(jnp.where(jj == la, tl["gt"], 0.0), axis=-1, keepdims=True)
                bcol = jnp.sum(jnp.where(jj == lb, tl["gt"], 0.0), axis=-1, keepdims=True)
                grow = grow_ref[tl["p"], pl.ds(la, 1), :]
                dec = jnp.exp(jnp.where(incl, gcol - grow, NEG))
                probs.append(dict(tl=tl, d=d, gcol=gcol, bcol=bcol, dec=dec,
                                  m=jnp.where(strict, tl["kk"] * bcol * dec, 0.0)))
        tinvs = _tri_inverse([pb["m"] for pb in probs], eye, lvl)
        uws = []
        for pb, tinv in zip(probs, tinvs):
            tl, gcol, bcol = pb["tl"], pb["gcol"], pb["bcol"]
            pb["eg"] = jnp.exp(gcol)
            rhs = jnp.concatenate([tl["v"] * bcol, tl["k"] * (bcol * pb["eg"])], axis=1).astype(bf16)
            t1, t2 = _split2(tinv)
            uws.append(_mm(jnp.concatenate([t1, t2], axis=1), jnp.concatenate([rhs, rhs], axis=0)))
        for pb, uw in zip(probs, uws):
            tl, d, gcol, rows = pb["tl"], pb["d"], pb["gcol"], pb["tl"]["rows"]
            ends = (CHUNK - 1, 2 * CHUNK - 1) if d == 0 else (0, CHUNK)
            gl = [gcol[e:e + 1, :] for e in ends]
            gtot = jnp.concatenate([jnp.broadcast_to(g, (CHUNK, 1)) for g in gl], axis=0)
            a2 = jnp.where(masks[d][0], tl["qk"] * pb["dec"], 0.0)
            u_s[d, rows, :] = uw[:, :GDN_HD]
            w_s[d, rows, :] = uw[:, GDN_HD:].astype(bf16)
            a_s[d, rows, :] = jnp.concatenate([a2[:CHUNK, :CHUNK], a2[CHUNK:, CHUNK:]], axis=0).astype(bf16)
            qg_s[d, rows, :] = (tl["q"] * pb["eg"]).astype(bf16)
            kg_s[d, rows, :] = (tl["k"] * jnp.exp(gtot - gcol)).astype(bf16)
            el_s[d, pl.ds(pl.multiple_of(tl["p"] * 16, 16), 16), :] = jnp.concatenate(
                [jnp.broadcast_to(jnp.exp(g), (8, LANES)) for g in gl], axis=0)
        return carry

    lax.fori_loop(0, npairs // group, prep, 0)
    oacc[...] = jnp.zeros(oacc.shape, f32)

    def scan(i, states):
        prs = (i, npairs - 1 - i)
        halves = ((0, 1), (1, 0))
        s = list(states)
        for step in range(2):
            rows = [pl.ds(pl.multiple_of(prs[d] * LANES + halves[d][step] * CHUNK, CHUNK), CHUNK)
                    for d in range(N_DIRS)]
            sb = [s[d].astype(bf16) for d in range(N_DIRS)]
            ws = [_mm(w_s[d, rows[d], :], sb[d]) for d in range(N_DIRS)]
            qs = [_mm(qg_s[d, rows[d], :], sb[d]) for d in range(N_DIRS)]
            vnb = [(u_s[d, rows[d], :] - ws[d]).astype(bf16) for d in range(N_DIRS)]
            av = [_mm(a_s[d, rows[d], :], vnb[d]) for d in range(N_DIRS)]
            kv = [_tn(kg_s[d, rows[d], :], vnb[d]) for d in range(N_DIRS)]
            for d in range(N_DIRS):
                el = el_s[d, pl.ds(pl.multiple_of(prs[d] * 16 + halves[d][step] * 8, 8), 8), :][0:1, :]
                s[d] = s[d] * el + kv[d]
                oacc[rows[d], :] += qs[d] + av[d]
        return tuple(s)

    s0 = jnp.zeros((GDN_HD, GDN_HD), f32)
    lax.fori_loop(0, npairs, scan, (s0, s0))

    tb = min(512, t)
    for ci in range(t // tb):
        rows = slice(ci * tb, (ci + 1) * tb)
        o = oacc[rows, :]
        z = z_ref[rows, :]
        ms = jnp.mean(o * o, axis=-1, keepdims=True)
        o_ref[rows, :] = (o * lax.rsqrt(ms + RMS_EPS) * gain_ref[...] * (z * _sigmoid(z))).astype(o_ref.dtype)


def gdn_branch(pm, pt, qkv, a_log, dt_bias, gain):
    b, t, _ = pm.shape
    n = t // CHUNK
    assert t % LANES == 0
    zcol0 = (3 * NA_WIDTH + 3 * GDN_WIDTH) // LANES
    gt, grow = gdn_gates(pt, a_log, dt_bias)
    hd = lambda off: (lambda bi, h: (bi, 0, off + h))
    return pl.pallas_call(
        functools.partial(_gdn_kernel, t=t, group=min(4, t // LANES)),
        out_shape=jax.ShapeDtypeStruct((b, t, GDN_WIDTH), bf16),
        grid=(b, GDN_HEADS),
        in_specs=[pl.BlockSpec((None, t, GDN_HD), hd(0)),
                  pl.BlockSpec((None, t, GDN_HD), hd(GDN_HEADS)),
                  pl.BlockSpec((None, t, GDN_HD), hd(2 * GDN_HEADS)),
                  pl.BlockSpec((None, t, GDN_HD), hd(zcol0)),
                  pl.BlockSpec((None, t, LANES), lambda bi, h: (bi, 0, 0)),
                  pl.BlockSpec((None, t // LANES, LANES, LANES), lambda bi, h: (bi, 0, 0, 0)),
                  pl.BlockSpec((1, LANES), lambda bi, h: (0, 0))],
        out_specs=pl.BlockSpec((None, t, GDN_HD), hd(0)),
        scratch_shapes=[pltpu.VMEM((N_DIRS, t, GDN_HD), f32),
                        pltpu.VMEM((N_DIRS, t, GDN_HD), bf16),
                        pltpu.VMEM((N_DIRS, t, GDN_HD), bf16),
                        pltpu.VMEM((N_DIRS, t, GDN_HD), bf16),
                        pltpu.VMEM((N_DIRS, t, CHUNK), bf16),
                        pltpu.VMEM((N_DIRS, n * 8, LANES), f32),
                        pltpu.VMEM((t, GDN_HD), f32)],
        compiler_params=_cparams(2),
        name="gdn_scan",
    )(qkv, qkv, qkv, pm, gt, grow, gain.reshape(1, GDN_HD))


def _rwkv_prep_kernel(cur_ref, prev_ref, next_ref, mu_ref, wup_ref, aup_ref, gup_ref, w0_ref, a0_ref,
                      rkv_ref, gate_ref, lw_ref, eta_ref, buf, *, tb):
    i = pl.program_id(1)
    nt = pl.num_programs(1)
    buf[8:8 + tb, :] = cur_ref[...]
    buf[7:8, :] = jnp.where(i > 0, prev_ref[7:8, :], 0.0)
    buf[8 + tb:9 + tb, :] = jnp.where(i < nt - 1, next_ref[0:1, :], 0.0)

    def mixed(c0, c1):
        x = buf[8:8 + tb, c0:c1]
        sh = 0.5 * (buf[7:7 + tb, c0:c1] + buf[9:9 + tb, c0:c1])
        return x + (sh - x) * mu_ref[:, c0:c1]

    for gi in range(3):
        c0 = gi * RWKV_WIDTH
        rkv_ref[:, c0:c0 + RWKV_WIDTH] = mixed(c0, c0 + RWKV_WIDTH)
    xwa = mixed(3 * RWKV_WIDTH, 3 * RWKV_WIDTH + LANES)
    xg = mixed(3 * RWKV_WIDTH + LANES, 3 * RWKV_WIDTH + 2 * LANES)
    gate_ref[...] = _mm(_sigmoid(xg).astype(bf16), gup_ref[...].astype(bf16))
    txw = jnp.tanh(xwa)
    for d in range(N_DIRS):
        log_w = -_softplus(-(w0_ref[d] + _mm(txw, wup_ref[d], HI))) - 0.5
        lw_ref[d] = -jnp.exp(log_w)
        eta_ref[d] = _sigmoid(a0_ref[d] + _mm(xwa, aup_ref[d], HI))


def rwkv_prep(pt, mu, w0, w_up, a0, a_up, g_up, tb=256):
    b, t, _ = pt.shape
    tb = min(tb, t)
    nb8 = tb // 8
    mu_p = mu.reshape(1, SHIFT_COLS)
    lora = w_up.shape[1]
    wup_p = jnp.pad(w_up, ((0, 0), (0, LANES - lora), (0, 0)))
    aup_p = jnp.pad(a_up, ((0, 0), (lora, LANES - lora - a_up.shape[1]), (0, 0)))
    out_sds = lambda *lead: jax.ShapeDtypeStruct(lead + (b, t, RWKV_WIDTH), f32)
    full = lambda shape: pl.BlockSpec(shape, lambda bi, i: (0,) * len(shape))
    return pl.pallas_call(
        functools.partial(_rwkv_prep_kernel, tb=tb),
        out_shape=[jax.ShapeDtypeStruct((b, t, 3 * RWKV_WIDTH), f32), out_sds(), out_sds(N_DIRS), out_sds(N_DIRS)],
        grid=(b, t // tb),
        in_specs=[pl.BlockSpec((None, tb, SHIFT_COLS), lambda bi, i: (bi, i, 0)),
                  pl.BlockSpec((None, 8, SHIFT_COLS), lambda bi, i: (bi, jnp.maximum(i * nb8 - 1, 0), 0)),
                  pl.BlockSpec((None, 8, SHIFT_COLS), lambda bi, i: (bi, jnp.minimum((i + 1) * nb8, t // 8 - 1), 0)),
                  full((1, SHIFT_COLS)), full((N_DIRS, LANES, RWKV_WIDTH)), full((N_DIRS, LANES, RWKV_WIDTH)),
                  full((LANES, RWKV_WIDTH)), full((N_DIRS, 1, RWKV_WIDTH)), full((N_DIRS, 1, RWKV_WIDTH))],
        out_specs=[pl.BlockSpec((None, tb, 3 * RWKV_WIDTH), lambda bi, i: (bi, i, 0)),
                   pl.BlockSpec((None, tb, RWKV_WIDTH), lambda bi, i: (bi, i, 0)),
                   pl.BlockSpec((N_DIRS, None, tb, RWKV_WIDTH), lambda bi, i: (0, bi, i, 0)),
                   pl.BlockSpec((N_DIRS, None, tb, RWKV_WIDTH), lambda bi, i: (0, bi, i, 0))],
        scratch_shapes=[pltpu.VMEM((tb + 16, SHIFT_COLS), f32)],
        compiler_params=_cparams(2),
        name="rwkv_prep",
    )(pt, pt, pt, mu_p, wup_p, aup_p, g_up, w0.reshape(N_DIRS, 1, -1), a0.reshape(N_DIRS, 1, -1))


def _stack2(x, m0, m1):
    return jnp.concatenate([x * m0, x * m1], axis=0)


def _rwkv_kernel(r_ref, k_ref, v_ref, gate_ref, lw0_ref, lw1_ref, eta0_ref, eta1_ref,
                 kk_ref, ka_ref, rk_ref, lnw_ref, lnb_ref, o_ref,
                 wt_s, uv_s, yv_s, arb_s, rt_s, bh_s, sk_s, eg_s, yacc, kk_s, gc_s, *, t, seg, group):
    n = t // CHUNK
    nseg = n // seg
    assert seg % group == 0 and n % 2 == 0
    c2 = 2 * CHUNK
    lane = _iota2((1, LANES), 1)
    m0 = (lane < RWKV_HD).astype(f32)
    m1 = 1.0 - m0
    ii, jj, same, ti, tj = _pair_masks()
    bd = same.astype(bf16)
    eye = (ii == jj).astype(f32)
    lvl = _level_masks(ti, tj, CHUNK)
    masks = ((same & (ti >= tj), same & (ti > tj)), (same & (ti <= tj), same & (ti < tj)))
    lw_refs = (lw0_ref, lw1_ref)
    eta_refs = (eta0_ref, eta1_ref)
    kkw = kk_ref[...]
    kaw = ka_ref[...]

    def pre(p, carry):
        rows = pl.ds(pl.multiple_of(p * LANES, LANES), LANES)
        kkr = k_ref[rows, :] * kkw
        kk_s[rows, :] = kkr * lax.rsqrt(_mm_rsel(kkr * kkr, bd) + L2_EPS)
        for d in range(N_DIRS):
            gc_s[d, rows, :] = _mm_sel(masks[d][0].astype(bf16), lw_refs[d][rows, :])
        return carry

    lax.fori_loop(0, t // LANES, pre, 0, unroll=2)

    def prep_chunks(problems):
        pbs = []
        for d, c, slot in problems:
            rows = pl.ds(pl.multiple_of(c * CHUNK, CHUNK), CHUNK)
            r = r_ref[rows, :]
            k = k_ref[rows, :]
            lw = lw_refs[d][rows, :]
            eta = eta_refs[d][rows, :]
            kk = kk_s[rows, :]
            kd = k * (1.0 + (eta - 1.0) * kaw)
            bvec = -kk * eta
            gcum = gc_s[d, rows, :]
            gtot = gcum[CHUNK - 1:CHUNK, :] if d == 0 else gcum[0:1, :]
            eneg = jnp.exp(-gcum)
            erem = jnp.exp(gtot - gcum)
            at2 = _stack2(kk * jnp.exp(gcum - lw), m0, m1)
            rt2 = _stack2(r * jnp.exp(gcum), m0, m1)
            kt2 = _stack2(kd * eneg, m0, m1)
            bt2 = _stack2(bvec * eneg, m0, m1)
            pbs.append(dict(d=d, slot=slot, at2=at2, rt2=rt2, gtot=gtot,
                            v2b=_stack2(v_ref[rows, :], m0, m1).astype(bf16),
                            bh=_stack2(bvec * erem, m0, m1).astype(bf16),
                            kh=_stack2(kd * erem, m0, m1).astype(bf16),
                            lhs=jnp.concatenate([at2, rt2], axis=0).astype(bf16),
                            rhs=jnp.concatenate([kt2, bt2], axis=0).astype(bf16)))
        ggs = [_nt(pb["lhs"], pb["rhs"]) for pb in pbs]
        for pb, gg in zip(pbs, ggs):
            incl, strict = masks[pb["d"]]
            pb["a_ak"] = jnp.where(strict, gg[0:c2, 0:c2], 0.0).astype(bf16)
            pb["a_ab"] = jnp.where(strict, gg[0:c2, c2:2 * c2], 0.0)
            pb["a_rk"] = jnp.where(incl, gg[c2:2 * c2, 0:c2], 0.0).astype(bf16)
            pb["a_rb"] = jnp.where(incl, gg[c2:2 * c2, c2:2 * c2], 0.0).astype(bf16)
        akv = [_mm(pb["a_ak"], pb["v2b"]) for pb in pbs]
        yv = [_mm(pb["a_rk"], pb["v2b"]) for pb in pbs]
        sk = [_tn(pb["v2b"], pb["kh"]) for pb in pbs]
        tinvs = _tri_inverse([-pb["a_ab"] for pb in pbs], eye, lvl)
        wus = []
        for pb, tinv, av in zip(pbs, tinvs, akv):
            rhs = jnp.concatenate([pb["at2"], av], axis=1).astype(bf16)
            t1, t2 = _split2(tinv)
            wus.append(_mm(jnp.concatenate([t1, t2], axis=1), jnp.concatenate([rhs, rhs], axis=0)))
        for i, pb in enumerate(pbs):
            d, slot = pb["d"], pb["slot"]
            srows = pl.ds(pl.multiple_of(slot * c2, c2), c2)
            wt_s[d, srows, :] = wus[i][:, :LANES].astype(bf16)
            uv_s[d, srows, :] = wus[i][:, LANES:]
            yv_s[d, srows, :] = yv[i]
            arb_s[d, srows, :] = pb["a_rb"]
            rt_s[d, srows, :] = pb["rt2"].astype(bf16)
            bh_s[d, srows, :] = pb["bh"]
            sk_s[d, srows, :] = sk[i]
            eg_s[d, pl.ds(pl.multiple_of(slot * 8, 8), 8), :] = jnp.broadcast_to(jnp.exp(pb["gtot"]), (8, LANES))

    def steps(cs, slot, states):
        dirs = range(N_DIRS)
        srows = pl.ds(pl.multiple_of(slot * c2, c2), c2)
        sb = [states[d].astype(bf16) for d in dirs]
        ws = [_nt(wt_s[d, srows, :], sb[d]) for d in dirs]
        rs = [_nt(rt_s[d, srows, :], sb[d]) for d in dirs]
        u2b = [(uv_s[d, srows, :] + ws[d]).astype(bf16) for d in dirs]
        au = [_mm(arb_s[d, srows, :], u2b[d]) for d in dirs]
        ub = [_tn(u2b[d], bh_s[d, srows, :]) for d in dirs]
        new = []
        for d in dirs:
            rows = pl.ds(pl.multiple_of(cs[d] * CHUNK, CHUNK), CHUNK)
            y2 = yv_s[d, srows, :] + rs[d] + au[d]
            eg = eg_s[d, pl.ds(pl.multiple_of(slot * 8, 8), 8), :][0:1, :]
            new.append(states[d] * eg + sk_s[d, srows, :] + ub[d])
            yacc[rows, :] += y2[0:CHUNK, :] + y2[CHUNK:c2, :]
        return tuple(new)

    yacc[...] = jnp.zeros(yacc.shape, f32)

    def segment(sg, states):
        def prep(it, carry):
            problems = []
            for gi in range(group):
                j = it * group + gi
                problems += [(0, sg * seg + j, j), (1, n - 1 - (sg * seg + j), j)]
            prep_chunks(problems)
            return carry

        lax.fori_loop(0, seg // group, prep, 0)

        def scan(j, st):
            return steps((sg * seg + j, n - 1 - (sg * seg + j)), j, st)

        return lax.fori_loop(0, seg, scan, states)

    s0 = jnp.zeros((LANES, LANES), f32)
    lax.fori_loop(0, nseg, segment, (s0, s0))

    tb = min(512, t)
    inv_hd = 1.0 / RWKV_HD
    for ci in range(t // tb):
        rows = slice(ci * tb, (ci + 1) * tb)
        y = yacc[rows, :]
        mean = _mm_rsel(y, bd) * inv_hd
        dlt = y - mean
        var = _mm_rsel(dlt * dlt, bd) * inv_hd
        yn = dlt * lax.rsqrt(var + RWKV_GN_EPS) * lnw_ref[...] + lnb_ref[...]
        k = k_ref[rows, :]
        kd_sum = k * (2.0 + (eta0_ref[rows, :] + eta1_ref[rows, :] - 2.0) * kaw)
        bonus = _mm_rsel(r_ref[rows, :] * kd_sum * rk_ref[...], bd) * v_ref[rows, :]
        o_ref[rows, :] = ((yn + bonus) * gate_ref[rows, :]).astype(o_ref.dtype)


def rwkv_branch(rkv, gate, lw, eta, k_k, k_a, r_k, ln_w, ln_b, seg=8):
    b, t, _ = gate.shape
    n = t // CHUNK
    seg = min(seg, n)
    assert n % seg == 0
    c2 = 2 * CHUNK
    blk = lambda off: pl.BlockSpec((None, t, LANES), lambda bi, p: (bi, 0, off + p))
    dblk = lambda d: pl.BlockSpec((None, None, t, LANES), lambda bi, p: (d, bi, 0, p))
    vec = pl.BlockSpec((1, LANES), lambda bi, p: (0, p))
    row = lambda x: x.reshape(1, RWKV_WIDTH)
    seg_f32 = pltpu.VMEM((N_DIRS, seg * c2, LANES), f32)
    seg_b16 = pltpu.VMEM((N_DIRS, seg * c2, LANES), bf16)
    return pl.pallas_call(
        functools.partial(_rwkv_kernel, t=t, seg=seg, group=min(4, seg)),
        out_shape=jax.ShapeDtypeStruct((b, t, RWKV_WIDTH), bf16),
        grid=(b, RWKV_PAIRS),
        in_specs=[blk(0), blk(RWKV_PAIRS), blk(2 * RWKV_PAIRS), blk(0),
                  dblk(0), dblk(1), dblk(0), dblk(1), vec, vec, vec, vec, vec],
        out_specs=blk(0),
        scratch_shapes=[seg_b16,
                        seg_f32,
                        seg_f32,
                        seg_b16,
                        seg_b16,
                        seg_b16,
                        seg_f32,
                        pltpu.VMEM((N_DIRS, seg * 8, LANES), f32),
                        pltpu.VMEM((t, LANES), f32),
                        pltpu.VMEM((t, LANES), f32),
                        pltpu.VMEM((N_DIRS, t, LANES), f32)],
        compiler_params=_cparams(2),
        name="rwkv_scan",
    )(rkv, rkv, rkv, gate, lw, lw, eta, eta, row(k_k), row(k_a), row(r_k), row(ln_w), row(ln_b))


def _tail_weights(w_in_layer):
    gdn_gate = w_in_layer[:, MAIN_COLS:MAIN_COLS + 2 * N_DIRS * GDN_HEADS]
    rwkv = w_in_layer[:, MAIN_COLS + 2 * N_DIRS * GDN_HEADS:]
    pad = jnp.zeros((w_in_layer.shape[0], TAIL_COLS - SHIFT_COLS - gdn_gate.shape[1]), f32)
    return jnp.concatenate([rwkv, gdn_gate, pad], axis=1)


def kernel(x, norm_mix, w_in, w_out, na_rpb, na_gain, gdn_conv_w, gdn_a_log, gdn_dt_bias, gdn_gain, rwkv_mu, rwkv_w0, rwkv_w_up, rwkv_a0, rwkv_a_up, rwkv_g_up, rwkv_k_k, rwkv_k_a, rwkv_r_k, rwkv_ln_w, rwkv_ln_b, norm_ffn, ffn_w_gate, ffn_w_up, ffn_w_down, moe_router, moe_w_gate, moe_w_up, moe_w_down, norm_final):
    b, t, d = x.shape
    n = b * t
    depth = w_in.shape[0]
    h = x.reshape(n, d)
    for layer in range(depth):
        u = rmsnorm(h, norm_mix[layer], bf16)
        pm = proj(u, w_in, layer, MAIN_COLS, tn=512).reshape(b, t, MAIN_COLS)
        w_tail = _tail_weights(w_in[layer])[None]
        pt = proj(u, w_tail, 0, TAIL_COLS, tn=512).reshape(b, t, TAIL_COLS)

        y_a = na_branch(pm, na_rpb[layer], na_gain[layer])
        qkv = gdn_prep(pm, gdn_conv_w, layer)
        y_b = gdn_branch(pm, pt, qkv, gdn_a_log[layer], gdn_dt_bias[layer], gdn_gain[layer])
        rkv, gate, lw, eta = rwkv_prep(pt, rwkv_mu[layer], rwkv_w0[layer], rwkv_w_up[layer],
                                       rwkv_a0[layer], rwkv_a_up[layer], rwkv_g_up[layer])
        y_c = rwkv_branch(rkv, gate, lw, eta, rwkv_k_k[layer], rwkv_k_a[layer], rwkv_r_k[layer],
                          rwkv_ln_w[layer], rwkv_ln_b[layer])
        h = outproj(y_a.reshape(n, -1), y_b.reshape(n, -1), y_c.reshape(n, -1), w_out, layer, h)

        i = layer // 2
        if layer % 2 == 0:
            u = rmsnorm(h, norm_ffn[layer], bf16)
            act = gateup(u, ffn_w_gate, ffn_w_up, i)
            h = down(act, ffn_w_down, i, h)
        else:
            h = moe_ffn(h, norm_ffn[layer], moe_router[i], moe_w_gate, moe_w_up, moe_w_down, i)
    return rmsnorm(h, norm_final, f32).reshape(b, t, d)
```

```python
import functools

import jax
import jax.numpy as jnp
import numpy as np
from jax import lax
from jax.experimental import pallas as pl
from jax.experimental.pallas import tpu as pltpu

f32 = jnp.float32
bf16 = jnp.bfloat16
HI = lax.Precision.HIGHEST

LANES = 128
VMEM_LIMIT_BYTES = 56 * 1024 * 1024

D_MODEL = 4096
GRID_W = 64
NA_HEADS, NA_HD = 12, 128
NA_WIDTH = NA_HEADS * NA_HD
NA_WIN_ROWS, NA_WIN_COLS = 8, 16
GDN_HEADS, GDN_HD = 12, 128
GDN_WIDTH = GDN_HEADS * GDN_HD
GDN_CONV = 5
CHUNK = 64
RWKV_HEADS, RWKV_HD = 16, 64
RWKV_WIDTH = RWKV_HEADS * RWKV_HD
RWKV_PAIRS = RWKV_HEADS // 2
N_DIRS = 2
MAIN_COLS = 3 * NA_WIDTH + 4 * GDN_WIDTH
RWKV_COLS = 3 * RWKV_WIDTH + 64 + 64 + 128
SHIFT_COLS = RWKV_COLS
assert SHIFT_COLS % LANES == 0
GATE_COLS = 2 * N_DIRS * GDN_HEADS
TAIL_TN = 512
TAIL_A_COLS = 3072
TAIL_B_USED = GATE_COLS + RWKV_COLS - TAIL_A_COLS
assert MAIN_COLS % TAIL_TN == 0 and TAIL_A_COLS % TAIL_TN == 0 and 0 < TAIL_B_USED <= TAIL_TN
MOE_EXPERTS = 8
RMS_EPS = 1e-6
L2_EPS = 1e-6
RWKV_GN_EPS = 64e-5
NEG = -1e30


def _cparams(n_axes):
    return pltpu.CompilerParams(dimension_semantics=("arbitrary",) * n_axes,
                                vmem_limit_bytes=VMEM_LIMIT_BYTES)


def _nt(a, b, precision=None):
    return lax.dot_general(a, b, (((1,), (1,)), ((), ())), precision=precision,
                           preferred_element_type=f32)


def _tn(a, b, precision=None):
    return lax.dot_general(a, b, (((0,), (0,)), ((), ())), precision=precision,
                           preferred_element_type=f32)


def _mm(a, b, precision=None):
    return jnp.dot(a, b, precision=precision, preferred_element_type=f32)


def _sigmoid(x):
    return 1.0 / (1.0 + jnp.exp(-x))


def _softplus(x):
    return jnp.maximum(x, 0.0) + jnp.log(1.0 + jnp.exp(-jnp.abs(x)))


def _iota2(shape, axis):
    return lax.broadcasted_iota(jnp.int32, shape, axis)


def _rmsnorm_kernel(x_ref, g_ref, o_ref):
    x = x_ref[...]
    ms = jnp.mean(x * x, axis=-1, keepdims=True)
    o_ref[...] = (x * lax.rsqrt(ms + RMS_EPS) * g_ref[...]).astype(o_ref.dtype)


def rmsnorm(x, gain, out_dtype):
    n, d = x.shape
    tm = min(256, n)
    return pl.pallas_call(
        _rmsnorm_kernel,
        out_shape=jax.ShapeDtypeStruct((n, d), out_dtype),
        grid=(n // tm,),
        in_specs=[pl.BlockSpec((tm, d), lambda i: (i, 0)), pl.BlockSpec((1, d), lambda i: (0, 0))],
        out_specs=pl.BlockSpec((tm, d), lambda i: (i, 0)),
        compiler_params=_cparams(1),
        name="rmsnorm",
    )(x, gain.reshape(1, d))


def _rmsnorm_router_kernel(x_ref, g_ref, r_ref, o_ref, e_ref, w_ref):
    x = x_ref[...]
    ms = jnp.mean(x * x, axis=-1, keepdims=True)
    u = x * lax.rsqrt(ms + RMS_EPS) * g_ref[...]
    o_ref[...] = u
    logits = _mm(u, r_ref[...], HI)
    lane = _iota2(logits.shape, 1)
    lg = jnp.where(lane < MOE_EXPERTS, logits, NEG)
    m1 = jnp.max(lg, axis=-1, keepdims=True)
    i1 = jnp.min(jnp.where(lg == m1, lane, LANES), axis=-1, keepdims=True)
    lg2 = jnp.where(lane == i1, NEG, lg)
    m2 = jnp.max(lg2, axis=-1, keepdims=True)
    i2 = jnp.min(jnp.where(lg2 == m2, lane, LANES), axis=-1, keepdims=True)
    e2 = jnp.exp(m2 - m1)
    w1 = 1.0 / (1.0 + e2)
    w2 = e2 / (1.0 + e2)
    e_ref[...] = jnp.where(lane == 0, i1, jnp.where(lane == 1, i2, 0))
    w_ref[...] = jnp.where(lane == 0, w1, jnp.where(lane == 1, w2, 0.0))


def rmsnorm_router(x, gain, router):
    n, d = x.shape
    tm = min(256, n)
    r_pad = jnp.pad(router, ((0, 0), (0, LANES - router.shape[1])))
    return pl.pallas_call(
        _rmsnorm_router_kernel,
        out_shape=[jax.ShapeDtypeStruct((n, d), f32), jax.ShapeDtypeStruct((n, LANES), jnp.int32),
                   jax.ShapeDtypeStruct((n, LANES), f32)],
        grid=(n // tm,),
        in_specs=[pl.BlockSpec((tm, d), lambda i: (i, 0)), pl.BlockSpec((1, d), lambda i: (0, 0)),
                  pl.BlockSpec((d, LANES), lambda i: (0, 0))],
        out_specs=[pl.BlockSpec((tm, d), lambda i: (i, 0)), pl.BlockSpec((tm, LANES), lambda i: (i, 0)),
                   pl.BlockSpec((tm, LANES), lambda i: (i, 0))],
        compiler_params=_cparams(1),
        name="rmsnorm_router",
    )(x, gain.reshape(1, d), r_pad)


def _moe_rank_kernel(e_ref, rank_ref, cnt_ref, carry, *, tb):
    @pl.when(pl.program_id(0) == 0)
    def _():
        carry[...] = jnp.zeros(carry.shape, f32)

    ids = e_ref[...]
    lane = _iota2(ids.shape, 1)
    oh1 = lane == ids[:, 0:1]
    oh2 = lane == ids[:, 1:2]
    f1 = jnp.where(oh1, 1.0, 0.0)
    f2 = jnp.where(oh2, 1.0, 0.0)
    earlier = jnp.where(_iota2((tb, tb), 0) > _iota2((tb, tb), 1), 1.0, 0.0).astype(bf16)
    p1 = _mm(earlier, f1.astype(bf16))
    p2 = _mm(earlier, f2.astype(bf16))
    c1 = jnp.sum(f1, axis=0, keepdims=True)
    c2 = jnp.sum(f2, axis=0, keepdims=True)
    base = carry[0:1, :]
    r1 = jnp.sum(jnp.where(oh1, p1 + base, 0.0), axis=-1, keepdims=True)
    r2 = jnp.sum(jnp.where(oh2, p2 + (base + c1), 0.0), axis=-1, keepdims=True)
    rank_ref[...] = jnp.where(lane == 0, r1, jnp.where(lane == 1, r2, 0.0))
    carry[...] = jnp.broadcast_to(base + c1 + c2, carry.shape)
    cnt_ref[...] = carry[...]


def moe_rank(ids, tb=256):
    n = ids.shape[0]
    tb = min(tb, n)
    return pl.pallas_call(
        functools.partial(_moe_rank_kernel, tb=tb),
        out_shape=[jax.ShapeDtypeStruct((n, LANES), f32), jax.ShapeDtypeStruct((8, LANES), f32)],
        grid=(n // tb,),
        in_specs=[pl.BlockSpec((tb, LANES), lambda i: (i, 0))],
        out_specs=[pl.BlockSpec((tb, LANES), lambda i: (i, 0)), pl.BlockSpec((8, LANES), lambda i: (0, 0))],
        scratch_shapes=[pltpu.VMEM((8, LANES), f32)],
        compiler_params=_cparams(1),
        name="moe_rank",
    )(ids)


def _gather_rows_kernel(idx_ref, src_ref, o_ref, buf, sem, *, tm):
    base = pl.program_id(0) * tm

    def row_copy(r, row):
        return pltpu.make_async_copy(src_ref.at[pl.ds(row, 1), :], buf.at[pl.ds(r, 1), :], sem)

    def start(r, c):
        row_copy(r, idx_ref[base + r]).start()
        return c

    def wait(r, c):
        row_copy(r, 0).wait()
        return c

    lax.fori_loop(0, tm, start, 0, unroll=8)
    lax.fori_loop(0, tm, wait, 0, unroll=8)
    o_ref[...] = buf[...].astype(o_ref.dtype)


def gather_rows(src, idx, out_dtype, tm=256):
    m = idx.shape[0]
    d = src.shape[1]
    tm = min(tm, m)
    assert m % tm == 0
    return pl.pallas_call(
        functools.partial(_gather_rows_kernel, tm=tm),
        out_shape=jax.ShapeDtypeStruct((m, d), out_dtype),
        grid_spec=pltpu.PrefetchScalarGridSpec(
            num_scalar_prefetch=1, grid=(m // tm,),
            in_specs=[pl.BlockSpec(memory_space=pl.ANY)],
            out_specs=pl.BlockSpec((tm, d), lambda i, idx_ref: (i, 0)),
            scratch_shapes=[pltpu.VMEM((tm, d), f32), pltpu.SemaphoreType.DMA(())]),
        compiler_params=_cparams(1),
        name="gather_rows",
    )(idx, src)


def _moe_combine_kernel(d1_ref, d2_ref, y_ref, h_ref, w_ref, o_ref, buf, sems, *, tm):
    base = pl.program_id(0) * tm
    dests = (d1_ref, d2_ref)

    def row_copy(k, r, row):
        return pltpu.make_async_copy(y_ref.at[pl.ds(row, 1), :], buf.at[k, pl.ds(r, 1), :], sems.at[k])

    def start(r, c):
        for k in range(2):
            row_copy(k, r, dests[k][base + r]).start()
        return c

    def wait(r, c):
        for k in range(2):
            row_copy(k, r, 0).wait()
        return c

    lax.fori_loop(0, tm, start, 0, unroll=8)
    lax.fori_loop(0, tm, wait, 0, unroll=8)
    w = w_ref[...]
    o_ref[...] = h_ref[...] + w[:, 0:1] * buf[0] + w[:, 1:2] * buf[1]


def moe_combine(y, d1, d2, h, wts, tm=256):
    n, d = h.shape
    tm = min(tm, n)
    return pl.pallas_call(
        functools.partial(_moe_combine_kernel, tm=tm),
        out_shape=jax.ShapeDtypeStruct((n, d), f32),
        grid_spec=pltpu.PrefetchScalarGridSpec(
            num_scalar_prefetch=2, grid=(n // tm,),
            in_specs=[pl.BlockSpec(memory_space=pl.ANY),
                      pl.BlockSpec((tm, d), lambda i, a, b: (i, 0)),
                      pl.BlockSpec((tm, LANES), lambda i, a, b: (i, 0))],
            out_specs=pl.BlockSpec((tm, d), lambda i, a, b: (i, 0)),
            scratch_shapes=[pltpu.VMEM((2, tm, d), f32), pltpu.SemaphoreType.DMA((2,))]),
        compiler_params=_cparams(1),
        name="moe_combine",
    )(d1, d2, y, h, wts)


def _proj_kernel(x_ref, w_ref, o_ref, wb_ref):
    @pl.when(pl.program_id(1) == 0)
    def _():
        wb_ref[...] = w_ref[...].astype(bf16)

    o_ref[...] = _mm(x_ref[...], wb_ref[...]).astype(o_ref.dtype)


def proj(x, w, w_index, n_cols, tn, out_dtype=f32, tm=1024, col_block0=0):
    n, k = x.shape
    tm = min(tm, n)
    assert n_cols % tn == 0 and n % tm == 0 and (col_block0 * tn + n_cols) <= w.shape[-1]
    return pl.pallas_call(
        _proj_kernel,
        out_shape=jax.ShapeDtypeStruct((n, n_cols), out_dtype),
        grid=(n_cols // tn, n // tm),
        in_specs=[pl.BlockSpec((tm, k), lambda j, i: (i, 0)),
                  pl.BlockSpec((None, k, tn), lambda j, i: (w_index, 0, col_block0 + j))],
        out_specs=pl.BlockSpec((tm, tn), lambda j, i: (i, j)),
        scratch_shapes=[pltpu.VMEM((k, tn), bf16)],
        compiler_params=_cparams(2),
        name="proj",
    )(x, w)


def _outproj_kernel(xa_ref, xb_ref, xc_ref, w_ref, res_ref, o_ref, wb_ref):
    @pl.when(pl.program_id(1) == 0)
    def _():
        wb_ref[...] = w_ref[...].astype(bf16)

    acc = _mm(xa_ref[...], wb_ref[0:NA_WIDTH, :])
    acc += _mm(xb_ref[...], wb_ref[NA_WIDTH:NA_WIDTH + GDN_WIDTH, :])
    acc += _mm(xc_ref[...], wb_ref[NA_WIDTH + GDN_WIDTH:, :])
    o_ref[...] = res_ref[...] + acc


def outproj(ya, yb, yc, w, layer, res, tn=512, tm=1024):
    n = ya.shape[0]
    d = w.shape[-1]
    tm = min(tm, n)
    return pl.pallas_call(
        _outproj_kernel,
        out_shape=jax.ShapeDtypeStruct((n, d), f32),
        grid=(d // tn, n // tm),
        in_specs=[pl.BlockSpec((tm, NA_WIDTH), lambda j, i: (i, 0)),
                  pl.BlockSpec((tm, GDN_WIDTH), lambda j, i: (i, 0)),
                  pl.BlockSpec((tm, RWKV_WIDTH), lambda j, i: (i, 0)),
                  pl.BlockSpec((None, D_MODEL, tn), lambda j, i: (layer, 0, j)),
                  pl.BlockSpec((tm, tn), lambda j, i: (i, j))],
        out_specs=pl.BlockSpec((tm, tn), lambda j, i: (i, j)),
        scratch_shapes=[pltpu.VMEM((D_MODEL, tn), bf16)],
        compiler_params=_cparams(2),
        name="outproj",
    )(ya, yb, yc, w, res)


def _gateup_kernel(x_ref, wg_ref, wu_ref, o_ref, wgb_ref, wub_ref):
    @pl.when(pl.program_id(1) == 0)
    def _():
        wgb_ref[...] = wg_ref[...].astype(bf16)
        wub_ref[...] = wu_ref[...].astype(bf16)

    x = x_ref[...]
    g = _mm(x, wgb_ref[...])
    u = _mm(x, wub_ref[...])
    o_ref[...] = (g * _sigmoid(g) * u).astype(o_ref.dtype)


def gateup(x, w_gate, w_up, w_index, tn=256, tm=1024):
    n, k = x.shape
    ff = w_gate.shape[-1]
    tm = min(tm, n)
    return pl.pallas_call(
        _gateup_kernel,
        out_shape=jax.ShapeDtypeStruct((n, ff), bf16),
        grid=(ff // tn, n // tm),
        in_specs=[pl.BlockSpec((tm, k), lambda j, i: (i, 0)),
                  pl.BlockSpec((None, k, tn), lambda j, i: (w_index, 0, j)),
                  pl.BlockSpec((None, k, tn), lambda j, i: (w_index, 0, j))],
        out_specs=pl.BlockSpec((tm, tn), lambda j, i: (i, j)),
        scratch_shapes=[pltpu.VMEM((k, tn), bf16), pltpu.VMEM((k, tn), bf16)],
        compiler_params=_cparams(2),
        name="gateup",
    )(x, w_gate, w_up)


def _tile_changed(te_ref, i):
    return (i == 0) | (te_ref[i] != te_ref[jnp.maximum(i - 1, 0)])


def _moe_gateup_kernel(te_ref, nu_ref, x_ref, wg_ref, wu_ref, o_ref, wgb_ref, wub_ref):
    i = pl.program_id(1)
    used = i < nu_ref[0]

    @pl.when(used & _tile_changed(te_ref, i))
    def _():
        wgb_ref[...] = wg_ref[...].astype(bf16)
        wub_ref[...] = wu_ref[...].astype(bf16)

    @pl.when(used)
    def _():
        x = x_ref[...]
        g = _mm(x, wgb_ref[...])
        u = _mm(x, wub_ref[...])
        o_ref[...] = (g * _sigmoid(g) * u).astype(o_ref.dtype)

    @pl.when(jnp.logical_not(used))
    def _():
        o_ref[...] = jnp.zeros(o_ref.shape, o_ref.dtype)


def _moe_down_kernel(te_ref, nu_ref, x_ref, w_ref, o_ref, wb_ref):
    i = pl.program_id(1)
    used = i < nu_ref[0]

    @pl.when(used & _tile_changed(te_ref, i))
    def _():
        wb_ref[...] = w_ref[...].astype(bf16)

    @pl.when(used)
    def _():
        o_ref[...] = _mm(x_ref[...], wb_ref[...])

    @pl.when(jnp.logical_not(used))
    def _():
        o_ref[...] = jnp.zeros(o_ref.shape, o_ref.dtype)


def _grouped_specs(tm, k, tn, w_index, n_w):
    last = lambda i, nu: jnp.minimum(i, nu[0] - 1)
    x_spec = pl.BlockSpec((tm, k), lambda j, i, te, nu: (last(i, nu), 0))
    w_spec = pl.BlockSpec((None, None, k, tn), lambda j, i, te, nu: (w_index, te[last(i, nu)], 0, j))
    o_spec = pl.BlockSpec((tm, tn), lambda j, i, te, nu: (i, j))
    return [x_spec] + [w_spec] * n_w, o_spec


def moe_gateup(x, tile_expert, n_used, w_gate, w_up, w_index, tm, tn=512):
    m, k = x.shape
    ff = w_gate.shape[-1]
    in_specs, out_spec = _grouped_specs(tm, k, tn, w_index, 2)
    return pl.pallas_call(
        _moe_gateup_kernel,
        out_shape=jax.ShapeDtypeStruct((m, ff), bf16),
        grid_spec=pltpu.PrefetchScalarGridSpec(
            num_scalar_prefetch=2, grid=(ff // tn, m // tm), in_specs=in_specs, out_specs=out_spec,
            scratch_shapes=[pltpu.VMEM((k, tn), bf16), pltpu.VMEM((k, tn), bf16)]),
        compiler_params=_cparams(2),
        name="moe_gateup",
    )(tile_expert, n_used, x, w_gate, w_up)


def moe_down(x, tile_expert, n_used, w_down, w_index, tm, tn=512):
    m, k = x.shape
    d = w_down.shape[-1]
    in_specs, out_spec = _grouped_specs(tm, k, tn, w_index, 1)
    return pl.pallas_call(
        _moe_down_kernel,
        out_shape=jax.ShapeDtypeStruct((m, d), f32),
        grid_spec=pltpu.PrefetchScalarGridSpec(
            num_scalar_prefetch=2, grid=(d // tn, m // tm), in_specs=in_specs, out_specs=out_spec,
            scratch_shapes=[pltpu.VMEM((k, tn), bf16)]),
        compiler_params=_cparams(2),
        name="moe_down",
    )(tile_expert, n_used, x, w_down)


def moe_ffn(h, gain, router, w_gate, w_up, w_down, w_index, tm=512):
    n, d = h.shape
    ne = w_gate.shape[1]
    tm = min(tm, n)
    u, ids, wts = rmsnorm_router(h, gain, router)
    rank, counts = moe_rank(ids)
    counts = counts[0, :ne].astype(jnp.int32)
    padded = (counts + tm - 1) // tm * tm
    ends = jnp.cumsum(padded)
    e2 = ids[:, :2]
    dest = (ends - padded)[e2] + rank[:, :2].astype(jnp.int32)
    n_tiles = 2 * n // tm + ne
    tile_expert = jnp.minimum(jnp.searchsorted(ends, jnp.arange(n_tiles) * tm, side="right"), ne - 1)
    n_used = (ends[-1] // tm).reshape(1)
    token = jnp.broadcast_to(jnp.arange(n, dtype=jnp.int32)[:, None], (n, 2))
    src = jnp.zeros((n_tiles * tm,), jnp.int32).at[dest.reshape(-1)].set(token.reshape(-1))
    xs = gather_rows(u, src, bf16)
    act = moe_gateup(xs, tile_expert.astype(jnp.int32), n_used.astype(jnp.int32), w_gate, w_up, w_index, tm)
    ys = moe_down(act, tile_expert.astype(jnp.int32), n_used.astype(jnp.int32), w_down, w_index, tm)
    return moe_combine(ys, dest[:, 0], dest[:, 1], h, wts)


def _down_kernel(x_ref, w_ref, res_ref, o_ref, wb_ref):
    @pl.when(pl.program_id(1) == 0)
    def _():
        wb_ref[...] = w_ref[...].astype(bf16)

    o_ref[...] = res_ref[...] + _mm(x_ref[...], wb_ref[...])


def down(x, w, w_index, res, k_block, tk, tn=512, tm=512):
    n = x.shape[0]
    d = w.shape[-1]
    tm = min(tm, n)
    return pl.pallas_call(
        _down_kernel,
        out_shape=jax.ShapeDtypeStruct((n, d), f32),
        grid=(d // tn, n // tm),
        in_specs=[pl.BlockSpec((tm, tk), lambda j, i: (i, k_block)),
                  pl.BlockSpec((None, tk, tn), lambda j, i: (w_index, k_block, j)),
                  pl.BlockSpec((tm, tn), lambda j, i: (i, j))],
        out_specs=pl.BlockSpec((tm, tn), lambda j, i: (i, j)),
        scratch_shapes=[pltpu.VMEM((tk, tn), bf16)],
        compiler_params=_cparams(2),
        name="down",
    )(x, w, res)


def _na_bias_table(rpb):
    c = np.arange(GRID_W)[:, None]
    kc = np.arange(GRID_W)[None, :]
    cs = np.clip(c - NA_WIN_COLS // 2, 0, GRID_W - NA_WIN_COLS)
    valid = (kc >= cs) & (kc < cs + NA_WIN_COLS)
    dc = kc - c + (NA_WIN_COLS - 1)
    onehot = (valid[None] & (dc[None] == np.arange(2 * NA_WIN_COLS - 1)[:, None, None])).astype(np.float32)
    cols = jnp.einsum("hrm,mck->hrck", rpb.astype(f32), jnp.asarray(onehot), precision=HI)
    lo = NA_WIN_ROWS - 1
    tab = jnp.stack([cols[:, lo - off:lo - off + NA_WIN_ROWS] for off in range(NA_WIN_ROWS)], axis=1)
    tab = tab.transpose(0, 1, 3, 2, 4) + jnp.asarray(np.where(valid, 0.0, NEG).astype(np.float32))[:, None, :]
    return tab.reshape(rpb.shape[0], NA_WIN_ROWS, GRID_W, NA_WIN_ROWS * GRID_W)


def _na_kernel(q_ref, k_ref, v_ref, bias_ref, gain_ref, o_ref, kb_ref, vb_ref, *, rows, rows_per_step, lockstep):
    g = pl.program_id(2)

    @pl.when(g == 0)
    def _():
        kb_ref[...] = k_ref[...].astype(bf16)
        vb_ref[...] = v_ref[...].astype(bf16)

    scale = NA_HD ** -0.5
    win = NA_WIN_ROWS * GRID_W

    def body(it, carry):
        qrows, krows, ss = [], [], []
        for li in range(lockstep):
            i = it * lockstep + li
            r = g * rows_per_step + i
            sr = jnp.clip(r - NA_WIN_ROWS // 2, 0, rows - NA_WIN_ROWS)
            qrows.append(pl.ds(pl.multiple_of(i * GRID_W, GRID_W), GRID_W))
            krows.append(pl.ds(pl.multiple_of(sr * GRID_W, GRID_W), win))
            q = q_ref[qrows[li], :].astype(bf16)
            ss.append(_nt(q, kb_ref[krows[li], :]) * scale + bias_ref[r - sr])
        ps, ls = [], []
        for s in ss:
            p = jnp.exp(s - jnp.max(s, axis=-1, keepdims=True))
            ls.append(jnp.sum(p, axis=-1, keepdims=True))
            ps.append(p)
        ovs = [_mm((p / l).astype(bf16), vb_ref[kr, :]) for p, l, kr in zip(ps, ls, krows)]
        for ov, qr in zip(ovs, qrows):
            ms = jnp.mean(ov * ov, axis=-1, keepdims=True)
            o_ref[qr, :] = (ov * lax.rsqrt(ms + RMS_EPS) * gain_ref[...]).astype(o_ref.dtype)
        return carry

    lax.fori_loop(0, rows_per_step // lockstep, body, 0)


def na_branch(pm, rpb, gain, rows_per_step=8):
    b, t, _ = pm.shape
    rows = t // GRID_W
    assert rows >= NA_WIN_ROWS and rows % rows_per_step == 0
    tq = rows_per_step * GRID_W
    bias = _na_bias_table(rpb)
    return pl.pallas_call(
        functools.partial(_na_kernel, rows=rows, rows_per_step=rows_per_step, lockstep=4),
        out_shape=jax.ShapeDtypeStruct((b, t, NA_WIDTH), bf16),
        grid=(b, NA_HEADS, rows // rows_per_step),
        in_specs=[pl.BlockSpec((None, tq, NA_HD), lambda bi, h, g: (bi, g, h)),
                  pl.BlockSpec((None, t, NA_HD), lambda bi, h, g: (bi, 0, NA_HEADS + h)),
                  pl.BlockSpec((None, t, NA_HD), lambda bi, h, g: (bi, 0, 2 * NA_HEADS + h)),
                  pl.BlockSpec((None, NA_WIN_ROWS, GRID_W, NA_WIN_ROWS * GRID_W), lambda bi, h, g: (h, 0, 0, 0)),
                  pl.BlockSpec((None, 1, NA_HD), lambda bi, h, g: (h, 0, 0))],
        out_specs=pl.BlockSpec((None, tq, NA_HD), lambda bi, h, g: (bi, g, h)),
        scratch_shapes=[pltpu.VMEM((t, NA_HD), bf16), pltpu.VMEM((t, NA_HD), bf16)],
        compiler_params=_cparams(3),
        name="na_attention",
    )(pm, pm, pm, bias, gain.reshape(NA_HEADS, 1, NA_HD))


def _gdn_prep_kernel(x_ref, w_ref, o_ref, xp_ref, *, t, tc):
    c = pl.program_id(1)
    pad = 8
    xp_ref[0:pad, :] = jnp.zeros((pad, LANES), f32)
    xp_ref[t + pad:t + 2 * pad, :] = jnp.zeros((pad, LANES), f32)
    xp_ref[pad:t + pad, :] = x_ref[...]
    half = GDN_CONV // 2
    for ci in range(t // tc):
        acc = jnp.zeros((tc, LANES), f32)
        for j in range(GDN_CONV):
            s0 = ci * tc + pad - half + j
            acc = acc + xp_ref[s0:s0 + tc, :] * w_ref[j:j + 1, :]
        y = acc * _sigmoid(acc)
        inv = lax.rsqrt(jnp.sum(y * y, axis=-1, keepdims=True) + L2_EPS)
        fac = jnp.where(c < GDN_HEADS, inv * GDN_HD ** -0.5, jnp.where(c < 2 * GDN_HEADS, inv, 1.0))
        o_ref[ci * tc:(ci + 1) * tc, :] = y * fac


def gdn_prep(pm, conv_w, layer):
    b, t, _ = pm.shape
    tc = min(512, t)
    col0 = 3 * NA_WIDTH // LANES
    nblk = 3 * GDN_WIDTH // LANES
    return pl.pallas_call(
        functools.partial(_gdn_prep_kernel, t=t, tc=tc),
        out_shape=jax.ShapeDtypeStruct((b, t, 3 * GDN_WIDTH), f32),
        grid=(b, nblk),
        in_specs=[pl.BlockSpec((None, t, LANES), lambda bi, c: (bi, 0, col0 + c)),
                  pl.BlockSpec((None, GDN_CONV, LANES), lambda bi, c: (layer, 0, c))],
        out_specs=pl.BlockSpec((None, t, LANES), lambda bi, c: (bi, 0, c)),
        scratch_shapes=[pltpu.VMEM((t + 16, LANES), f32)],
        compiler_params=_cparams(2),
        name="gdn_prep",
    )(pm, conv_w)


def _level_masks(ii, jj, size):
    masks = []
    s = 1
    while s < size:
        masks.append(((ii // (2 * s)) == (jj // (2 * s))) & ((ii // s) != (jj // s)))
        s *= 2
    return masks


def _split2(x):
    hi = x.astype(bf16)
    return hi, (x - hi.astype(f32)).astype(bf16)


def _split3(x):
    x1 = x.astype(bf16)
    r = x - x1.astype(f32)
    x2 = r.astype(bf16)
    return x1, x2, (r - x2.astype(f32)).astype(bf16)


def _mm_f32(a, b):
    a1, a2 = _split2(a)
    b1, b2 = _split2(b)
    return _mm(jnp.concatenate([a1, a2, a1], axis=1), jnp.concatenate([b1, b1, b2], axis=0))


def _mm_sel(sel, x):
    x1, x2, x3 = _split3(x)
    return _mm(jnp.concatenate([sel, sel, sel], axis=1), jnp.concatenate([x1, x2, x3], axis=0))


def _mm_rsel(x, sel):
    x1, x2, x3 = _split3(x)
    return _mm(jnp.concatenate([x1, x2, x3], axis=1), jnp.concatenate([sel, sel, sel], axis=0))


def _nt_sel(sel, x):
    x1, x2, x3 = _split3(x)
    return _nt(jnp.concatenate([sel, sel, sel], axis=1), jnp.concatenate([x1, x2, x3], axis=1))


def _tri_inverse(ms, eye, masks):
    xs = [eye - jnp.where(masks[0], m, 0.0) for m in ms]
    for mask in masks[1:]:
        xcs = [_mm_f32(x, jnp.where(mask, m, 0.0)) for x, m in zip(xs, ms)]
        xs = [x - _mm_f32(xc, x) for x, xc in zip(xs, xcs)]
    return xs


def _pair_masks():
    ii = _iota2((LANES, LANES), 0)
    jj = _iota2((LANES, LANES), 1)
    same = (ii // CHUNK) == (jj // CHUNK)
    ti, tj = ii % CHUNK, jj % CHUNK
    return ii, jj, same, ti, tj


def _gdn_gates_kernel(ab_ref, alog_ref, dtb_ref, gt_ref, grow_ref, *, tb):
    ii, jj, same, ti, tj = _pair_masks()
    tri = ((same & (ti >= tj)).astype(bf16), (same & (ti <= tj)).astype(bf16))
    eye = (ii == jj).astype(bf16)
    for p in range(tb // LANES):
        rows = slice(p * LANES, (p + 1) * LANES)
        ab = ab_ref[rows, :]
        g = -jnp.exp(alog_ref[...]) * _softplus(ab + dtb_ref[...])
        fwd = _mm_sel(tri[0], g)
        bwd = _mm_sel(tri[1], g)
        gt = jnp.where(jj < GDN_HEADS, fwd, jnp.where(jj < N_DIRS * GDN_HEADS, bwd, _sigmoid(ab)))
        gt_ref[rows, :] = gt
        grow_ref[p] = _nt_sel(eye, gt)


def gdn_gates(pt, a_log, dt_bias, tb=512):
    b, t, _ = pt.shape
    tb = min(tb, t)
    npad = LANES - N_DIRS * GDN_HEADS
    alog = jnp.pad(a_log.reshape(1, -1), ((0, 0), (0, npad)))
    dtb = jnp.pad(dt_bias.reshape(1, -1), ((0, 0), (0, npad)))
    vec = pl.BlockSpec((1, LANES), lambda bi, i: (0, 0))
    return pl.pallas_call(
        functools.partial(_gdn_gates_kernel, tb=tb),
        out_shape=[jax.ShapeDtypeStruct((b, t, LANES), f32),
                   jax.ShapeDtypeStruct((b, t // LANES, LANES, LANES), f32)],
        grid=(b, t // tb),
        in_specs=[pl.BlockSpec((None, tb, LANES), lambda bi, i: (bi, i, 0)), vec, vec],
        out_specs=[pl.BlockSpec((None, tb, LANES), lambda bi, i: (bi, i, 0)),
                   pl.BlockSpec((None, tb // LANES, LANES, LANES), lambda bi, i: (bi, i, 0, 0))],
        compiler_params=_cparams(2),
        name="gdn_gates",
    )(pt, alog, dtb)


def _gdn_kernel(q_ref, k_ref, v_ref, z_ref, gt_ref, grow_ref, gain_ref, o_ref,
                u_s, w_s, qg_s, kg_s, a_s, el_s, oacc, *, t, group):
    h = pl.program_id(1)
    npairs = t // LANES
    assert npairs % group == 0 and npairs % 2 == 0
    ii, jj, same, ti, tj = _pair_masks()
    eye = (ii == jj).astype(f32)
    lvl = _level_masks(ti, tj, CHUNK)
    masks = ((same & (ti >= tj), same & (ti > tj)), (same & (ti <= tj), same & (ti < tj)))

    def prep(it, carry):
        tiles = []
        for gi in range(group):
            p = it * group + gi
            rows = pl.ds(pl.multiple_of(p * LANES, LANES), LANES)
            q = q_ref[rows, :]
            k = k_ref[rows, :]
            kb16 = k.astype(bf16)
            tiles.append(dict(p=p, rows=rows, q=q, k=k, v=v_ref[rows, :], gt=gt_ref[rows, :],
                              qk=_nt(q.astype(bf16), kb16), kk=_nt(kb16, kb16)))
        probs = []
        for tl in tiles:
            for d in range(N_DIRS):
                la = d * GDN_HEADS + h
                lb = N_DIRS * GDN_HEADS + la
                incl, strict = masks[d]
                gcol = jnp.sum(jnp.where(jj == la, tl["gt"], 0.0), axis=-1, keepdims=True)
                bcol = jnp.sum(jnp.where(jj == lb, tl["gt"], 0.0), axis=-1, keepdims=True)
                grow = grow_ref[tl["p"], pl.ds(la, 1), :]
                dec = jnp.exp(jnp.where(incl, gcol - grow, NEG))
                probs.append(dict(tl=tl, d=d, gcol=gcol, bcol=bcol, dec=dec,
                                  m=jnp.where(strict, tl["kk"] * bcol * dec, 0.0)))
        tinvs = _tri_inverse([pb["m"] for pb in probs], eye, lvl)
        uws = []
        for pb, tinv in zip(probs, tinvs):
            tl, gcol, bcol = pb["tl"], pb["gcol"], pb["bcol"]
            pb["eg"] = jnp.exp(gcol)
            rhs = jnp.concatenate([tl["v"] * bcol, tl["k"] * (bcol * pb["eg"])], axis=1).astype(bf16)
            t1, t2 = _split2(tinv)
            uws.append(_mm(jnp.concatenate([t1, t2], axis=1), jnp.concatenate([rhs, rhs], axis=0)))
        for pb, uw in zip(probs, uws):
            tl, d, gcol, rows = pb["tl"], pb["d"], pb["gcol"], pb["tl"]["rows"]
            ends = (CHUNK - 1, 2 * CHUNK - 1) if d == 0 else (0, CHUNK)
            gl = [gcol[e:e + 1, :] for e in ends]
            gtot = jnp.concatenate([jnp.broadcast_to(g, (CHUNK, 1)) for g in gl], axis=0)
            a2 = jnp.where(masks[d][0], tl["qk"] * pb["dec"], 0.0)
            u_s[d, rows, :] = uw[:, :GDN_HD]
            w_s[d, rows, :] = uw[:, GDN_HD:].astype(bf16)
            a_s[d, rows, :] = jnp.concatenate([a2[:CHUNK, :CHUNK], a2[CHUNK:, CHUNK:]], axis=0).astype(bf16)
            qg_s[d, rows, :] = (tl["q"] * pb["eg"]).astype(bf16)
            kg_s[d, rows, :] = (tl["k"] * jnp.exp(gtot - gcol)).astype(bf16)
            el_s[d, pl.ds(pl.multiple_of(tl["p"] * 16, 16), 16), :] = jnp.concatenate(
                [jnp.broadcast_to(jnp.exp(g), (8, LANES)) for g in gl], axis=0)
        return carry

    lax.fori_loop(0, npairs // group, prep, 0)
    oacc[...] = jnp.zeros(oacc.shape, f32)

    def scan(i, states):
        prs = (i, npairs - 1 - i)
        halves = ((0, 1), (1, 0))
        s = list(states)
        for step in range(2):
            rows = [pl.ds(pl.multiple_of(prs[d] * LANES + halves[d][step] * CHUNK, CHUNK), CHUNK)
                    for d in range(N_DIRS)]
            sb = [s[d].astype(bf16) for d in range(N_DIRS)]
            ws = [_mm(w_s[d, rows[d], :], sb[d]) for d in range(N_DIRS)]
            qs = [_mm(qg_s[d, rows[d], :], sb[d]) for d in range(N_DIRS)]
            vnb = [(u_s[d, rows[d], :] - ws[d]).astype(bf16) for d in range(N_DIRS)]
            av = [_mm(a_s[d, rows[d], :], vnb[d]) for d in range(N_DIRS)]
            kv = [_tn(kg_s[d, rows[d], :], vnb[d]) for d in range(N_DIRS)]
            for d in range(N_DIRS):
                el = el_s[d, pl.ds(pl.multiple_of(prs[d] * 16 + halves[d][step] * 8, 8), 8), :][0:1, :]
                s[d] = s[d] * el + kv[d]
                oacc[rows[d], :] += qs[d] + av[d]
        return tuple(s)

    s0 = jnp.zeros((GDN_HD, GDN_HD), f32)
    lax.fori_loop(0, npairs, scan, (s0, s0))

    tb = min(512, t)
    for ci in range(t // tb):
        rows = slice(ci * tb, (ci + 1) * tb)
        o = oacc[rows, :]
        z = z_ref[rows, :]
        ms = jnp.mean(o * o, axis=-1, keepdims=True)
        o_ref[rows, :] = (o * lax.rsqrt(ms + RMS_EPS) * gain_ref[...] * (z * _sigmoid(z))).astype(o_ref.dtype)


def gdn_branch(pm, pt, qkv, a_log, dt_bias, gain):
    b, t, _ = pm.shape
    n = t // CHUNK
    assert t % LANES == 0
    zcol0 = (3 * NA_WIDTH + 3 * GDN_WIDTH) // LANES
    gt, grow = gdn_gates(pt, a_log, dt_bias)
    hd = lambda off: (lambda bi, h: (bi, 0, off + h))
    return pl.pallas_call(
        functools.partial(_gdn_kernel, t=t, group=min(4, t // LANES)),
        out_shape=jax.ShapeDtypeStruct((b, t, GDN_WIDTH), bf16),
        grid=(b, GDN_HEADS),
        in_specs=[pl.BlockSpec((None, t, GDN_HD), hd(0)),
                  pl.BlockSpec((None, t, GDN_HD), hd(GDN_HEADS)),
                  pl.BlockSpec((None, t, GDN_HD), hd(2 * GDN_HEADS)),
                  pl.BlockSpec((None, t, GDN_HD), hd(zcol0)),
                  pl.BlockSpec((None, t, LANES), lambda bi, h: (bi, 0, 0)),
                  pl.BlockSpec((None, t // LANES, LANES, LANES), lambda bi, h: (bi, 0, 0, 0)),
                  pl.BlockSpec((1, LANES), lambda bi, h: (0, 0))],
        out_specs=pl.BlockSpec((None, t, GDN_HD), hd(0)),
        scratch_shapes=[pltpu.VMEM((N_DIRS, t, GDN_HD), f32),
                        pltpu.VMEM((N_DIRS, t, GDN_HD), bf16),
                        pltpu.VMEM((N_DIRS, t, GDN_HD), bf16),
                        pltpu.VMEM((N_DIRS, t, GDN_HD), bf16),
                        pltpu.VMEM((N_DIRS, t, CHUNK), bf16),
                        pltpu.VMEM((N_DIRS, n * 8, LANES), f32),
                        pltpu.VMEM((t, GDN_HD), f32)],
        compiler_params=_cparams(2),
        name="gdn_scan",
    )(qkv, qkv, qkv, pm, gt, grow, gain.reshape(1, GDN_HD))


def _rwkv_prep_kernel(cur_a, prev_a, next_a, cur_b, prev_b, next_b, mu_ref, wup_ref, aup_ref, gup_ref, w0_ref,
                      a0_ref, rkv_ref, gate_ref, lw_ref, eta_ref, buf, *, tb):
    i = pl.program_id(1)
    nt = pl.num_programs(1)
    na = TAIL_A_COLS - GATE_COLS
    buf[8:8 + tb, 0:na] = cur_a[:, GATE_COLS:]
    buf[8:8 + tb, na:] = cur_b[:, 0:TAIL_B_USED]
    has_prev = i > 0
    has_next = i < nt - 1
    buf[7:8, 0:na] = jnp.where(has_prev, prev_a[7:8, GATE_COLS:], 0.0)
    buf[7:8, na:] = jnp.where(has_prev, prev_b[7:8, 0:TAIL_B_USED], 0.0)
    buf[8 + tb:9 + tb, 0:na] = jnp.where(has_next, next_a[0:1, GATE_COLS:], 0.0)
    buf[8 + tb:9 + tb, na:] = jnp.where(has_next, next_b[0:1, 0:TAIL_B_USED], 0.0)

    def mixed(c0, c1):
        x = buf[8:8 + tb, c0:c1]
        sh = 0.5 * (buf[7:7 + tb, c0:c1] + buf[9:9 + tb, c0:c1])
        return x + (sh - x) * mu_ref[:, c0:c1]

    for gi in range(3):
        c0 = gi * RWKV_WIDTH
        rkv_ref[:, c0:c0 + RWKV_WIDTH] = mixed(c0, c0 + RWKV_WIDTH)
    xwa = mixed(3 * RWKV_WIDTH, 3 * RWKV_WIDTH + LANES)
    xg = mixed(3 * RWKV_WIDTH + LANES, 3 * RWKV_WIDTH + 2 * LANES)
    gate_ref[...] = _mm(_sigmoid(xg).astype(bf16), gup_ref[...].astype(bf16))
    txw = jnp.tanh(xwa)
    for d in range(N_DIRS):
        log_w = -_softplus(-(w0_ref[d] + _mm_f32(txw, wup_ref[d]))) - 0.5
        lw_ref[d] = -jnp.exp(log_w)
        eta_ref[d] = _sigmoid(a0_ref[d] + _mm_f32(xwa, aup_ref[d]))


def rwkv_prep(pta, ptb, mu, w0, w_up, a0, a_up, g_up, tb=256):
    b, t, _ = pta.shape
    tb = min(tb, t)
    nb8 = tb // 8
    mu_p = mu.reshape(1, SHIFT_COLS)
    lora = w_up.shape[1]
    wup_p = jnp.pad(w_up, ((0, 0), (0, LANES - lora), (0, 0)))
    aup_p = jnp.pad(a_up, ((0, 0), (lora, LANES - lora - a_up.shape[1]), (0, 0)))
    out_sds = lambda *lead: jax.ShapeDtypeStruct(lead + (b, t, RWKV_WIDTH), f32)
    full = lambda shape: pl.BlockSpec(shape, lambda bi, i: (0,) * len(shape))

    def halo(cols, side):
        if side == 0:
            return pl.BlockSpec((None, tb, cols), lambda bi, i: (bi, i, 0))
        if side < 0:
            return pl.BlockSpec((None, 8, cols), lambda bi, i: (bi, jnp.maximum(i * nb8 - 1, 0), 0))
        return pl.BlockSpec((None, 8, cols), lambda bi, i: (bi, jnp.minimum((i + 1) * nb8, t // 8 - 1), 0))

    return pl.pallas_call(
        functools.partial(_rwkv_prep_kernel, tb=tb),
        out_shape=[jax.ShapeDtypeStruct((b, t, 3 * RWKV_WIDTH), f32), out_sds(), out_sds(N_DIRS), out_sds(N_DIRS)],
        grid=(b, t // tb),
        in_specs=[halo(TAIL_A_COLS, 0), halo(TAIL_A_COLS, -1), halo(TAIL_A_COLS, 1),
                  halo(TAIL_TN, 0), halo(TAIL_TN, -1), halo(TAIL_TN, 1),
                  full((1, SHIFT_COLS)), full((N_DIRS, LANES, RWKV_WIDTH)), full((N_DIRS, LANES, RWKV_WIDTH)),
                  full((LANES, RWKV_WIDTH)), full((N_DIRS, 1, RWKV_WIDTH)), full((N_DIRS, 1, RWKV_WIDTH))],
        out_specs=[pl.BlockSpec((None, tb, 3 * RWKV_WIDTH), lambda bi, i: (bi, i, 0)),
                   pl.BlockSpec((None, tb, RWKV_WIDTH), lambda bi, i: (bi, i, 0)),
                   pl.BlockSpec((N_DIRS, None, tb, RWKV_WIDTH), lambda bi, i: (0, bi, i, 0)),
                   pl.BlockSpec((N_DIRS, None, tb, RWKV_WIDTH), lambda bi, i: (0, bi, i, 0))],
        scratch_shapes=[pltpu.VMEM((tb + 16, SHIFT_COLS), f32)],
        compiler_params=_cparams(2),
        name="rwkv_prep",
    )(pta, pta, pta, ptb, ptb, ptb, mu_p, wup_p, aup_p, g_up, w0.reshape(N_DIRS, 1, -1), a0.reshape(N_DIRS, 1, -1))


def _stack2(x, m0, m1):
    return jnp.concatenate([x * m0, x * m1], axis=0)


def _rwkv_kernel(r_ref, k_ref, v_ref, gate_ref, lw0_ref, lw1_ref, eta0_ref, eta1_ref,
                 kk_ref, ka_ref, rk_ref, lnw_ref, lnb_ref, o_ref,
                 wt_s, uv_s, yv_s, arb_s, rt_s, bh_s, sk_s, eg_s, yacc, kk_s, gc_s, *, t, seg, group):
    n = t // CHUNK
    nseg = n // seg
    assert seg % group == 0 and n % 2 == 0
    c2 = 2 * CHUNK
    lane = _iota2((1, LANES), 1)
    m0 = (lane < RWKV_HD).astype(f32)
    m1 = 1.0 - m0
    ii, jj, same, ti, tj = _pair_masks()
    bd = same.astype(bf16)
    eye = (ii == jj).astype(f32)
    lvl = _level_masks(ti, tj, CHUNK)
    masks = ((same & (ti >= tj), same & (ti > tj)), (same & (ti <= tj), same & (ti < tj)))
    lw_refs = (lw0_ref, lw1_ref)
    eta_refs = (eta0_ref, eta1_ref)
    kkw = kk_ref[...]
    kaw = ka_ref[...]

    def pre(p, carry):
        rows = pl.ds(pl.multiple_of(p * LANES, LANES), LANES)
        kkr = k_ref[rows, :] * kkw
        kk_s[rows, :] = kkr * lax.rsqrt(_mm_rsel(kkr * kkr, bd) + L2_EPS)
        for d in range(N_DIRS):
            gc_s[d, rows, :] = _mm_sel(masks[d][0].astype(bf16), lw_refs[d][rows, :])
        return carry

    lax.fori_loop(0, t // LANES, pre, 0, unroll=2)

    def prep_chunks(problems):
        pbs = []
        for d, c, slot in problems:
            rows = pl.ds(pl.multiple_of(c * CHUNK, CHUNK), CHUNK)
            r = r_ref[rows, :]
            k = k_ref[rows, :]
            lw = lw_refs[d][rows, :]
            eta = eta_refs[d][rows, :]
            kk = kk_s[rows, :]
            kd = k * (1.0 + (eta - 1.0) * kaw)
            bvec = -kk * eta
            gcum = gc_s[d, rows, :]
            gtot = gcum[CHUNK - 1:CHUNK, :] if d == 0 else gcum[0:1, :]
            eneg = jnp.exp(-gcum)
            erem = jnp.exp(gtot - gcum)
            at2 = _stack2(kk * jnp.exp(gcum - lw), m0, m1)
            rt2 = _stack2(r * jnp.exp(gcum), m0, m1)
            kt2 = _stack2(kd * eneg, m0, m1)
            bt2 = _stack2(bvec * eneg, m0, m1)
            pbs.append(dict(d=d, slot=slot, at2=at2, rt2=rt2, gtot=gtot,
                            v2b=_stack2(v_ref[rows, :], m0, m1).astype(bf16),
                            bh=_stack2(bvec * erem, m0, m1).astype(bf16),
                            kh=_stack2(kd * erem, m0, m1).astype(bf16),
                            lhs=jnp.concatenate([at2, rt2], axis=0).astype(bf16),
                            rhs=jnp.concatenate([kt2, bt2], axis=0).astype(bf16)))
        ggs = [_nt(pb["lhs"], pb["rhs"]) for pb in pbs]
        for pb, gg in zip(pbs, ggs):
            incl, strict = masks[pb["d"]]
            pb["a_ak"] = jnp.where(strict, gg[0:c2, 0:c2], 0.0).astype(bf16)
            pb["a_ab"] = jnp.where(strict, gg[0:c2, c2:2 * c2], 0.0)
            pb["a_rk"] = jnp.where(incl, gg[c2:2 * c2, 0:c2], 0.0).astype(bf16)
            pb["a_rb"] = jnp.where(incl, gg[c2:2 * c2, c2:2 * c2], 0.0).astype(bf16)
        akv = [_mm(pb["a_ak"], pb["v2b"]) for pb in pbs]
        yv = [_mm(pb["a_rk"], pb["v2b"]) for pb in pbs]
        sk = [_tn(pb["v2b"], pb["kh"]) for pb in pbs]
        tinvs = _tri_inverse([-pb["a_ab"] for pb in pbs], eye, lvl)
        wus = []
        for pb, tinv, av in zip(pbs, tinvs, akv):
            rhs = jnp.concatenate([pb["at2"], av], axis=1).astype(bf16)
            t1, t2 = _split2(tinv)
            wus.append(_mm(jnp.concatenate([t1, t2], axis=1), jnp.concatenate([rhs, rhs], axis=0)))
        for i, pb in enumerate(pbs):
            d, slot = pb["d"], pb["slot"]
            srows = pl.ds(pl.multiple_of(slot * c2, c2), c2)
            wt_s[d, srows, :] = wus[i][:, :LANES].astype(bf16)
            uv_s[d, srows, :] = wus[i][:, LANES:]
            yv_s[d, srows, :] = yv[i]
            arb_s[d, srows, :] = pb["a_rb"]
            rt_s[d, srows, :] = pb["rt2"].astype(bf16)
            bh_s[d, srows, :] = pb["bh"]
            sk_s[d, srows, :] = sk[i]
            eg_s[d, pl.ds(pl.multiple_of(slot * 8, 8), 8), :] = jnp.broadcast_to(jnp.exp(pb["gtot"]), (8, LANES))

    def steps(cs, slot, states):
        dirs = range(N_DIRS)
        srows = pl.ds(pl.multiple_of(slot * c2, c2), c2)
        sb = [states[d].astype(bf16) for d in dirs]
        ws = [_nt(wt_s[d, srows, :], sb[d]) for d in dirs]
        rs = [_nt(rt_s[d, srows, :], sb[d]) for d in dirs]
        u2b = [(uv_s[d, srows, :] + ws[d]).astype(bf16) for d in dirs]
        au = [_mm(arb_s[d, srows, :], u2b[d]) for d in dirs]
        ub = [_tn(u2b[d], bh_s[d, srows, :]) for d in dirs]
        new = []
        for d in dirs:
            rows = pl.ds(pl.multiple_of(cs[d] * CHUNK, CHUNK), CHUNK)
            y2 = yv_s[d, srows, :] + rs[d] + au[d]
            eg = eg_s[d, pl.ds(pl.multiple_of(slot * 8, 8), 8), :][0:1, :]
            new.append(states[d] * eg + sk_s[d, srows, :] + ub[d])
            yacc[rows, :] += y2[0:CHUNK, :] + y2[CHUNK:c2, :]
        return tuple(new)

    yacc[...] = jnp.zeros(yacc.shape, f32)

    def segment(sg, states):
        def prep(it, carry):
            problems = []
            for gi in range(group):
                j = it * group + gi
                problems += [(0, sg * seg + j, j), (1, n - 1 - (sg * seg + j), j)]
            prep_chunks(problems)
            return carry

        lax.fori_loop(0, seg // group, prep, 0)

        def scan(j, st):
            return steps((sg * seg + j, n - 1 - (sg * seg + j)), j, st)

        return lax.fori_loop(0, seg, scan, states)

    s0 = jnp.zeros((LANES, LANES), f32)
    lax.fori_loop(0, nseg, segment, (s0, s0))

    tb = min(512, t)
    inv_hd = 1.0 / RWKV_HD
    for ci in range(t // tb):
        rows = slice(ci * tb, (ci + 1) * tb)
        y = yacc[rows, :]
        mean = _mm_rsel(y, bd) * inv_hd
        dlt = y - mean
        var = _mm_rsel(dlt * dlt, bd) * inv_hd
        yn = dlt * lax.rsqrt(var + RWKV_GN_EPS) * lnw_ref[...] + lnb_ref[...]
        k = k_ref[rows, :]
        kd_sum = k * (2.0 + (eta0_ref[rows, :] + eta1_ref[rows, :] - 2.0) * kaw)
        bonus = _mm_rsel(r_ref[rows, :] * kd_sum * rk_ref[...], bd) * v_ref[rows, :]
        o_ref[rows, :] = ((yn + bonus) * gate_ref[rows, :]).astype(o_ref.dtype)


def rwkv_branch(rkv, gate, lw, eta, k_k, k_a, r_k, ln_w, ln_b, seg=8):
    b, t, _ = gate.shape
    n = t // CHUNK
    seg = min(seg, n)
    assert n % seg == 0
    c2 = 2 * CHUNK
    blk = lambda off: pl.BlockSpec((None, t, LANES), lambda bi, p: (bi, 0, off + p))
    dblk = lambda d: pl.BlockSpec((None, None, t, LANES), lambda bi, p: (d, bi, 0, p))
    vec = pl.BlockSpec((1, LANES), lambda bi, p: (0, p))
    row = lambda x: x.reshape(1, RWKV_WIDTH)
    seg_f32 = pltpu.VMEM((N_DIRS, seg * c2, LANES), f32)
    seg_b16 = pltpu.VMEM((N_DIRS, seg * c2, LANES), bf16)
    return pl.pallas_call(
        functools.partial(_rwkv_kernel, t=t, seg=seg, group=min(4, seg)),
        out_shape=jax.ShapeDtypeStruct((b, t, RWKV_WIDTH), bf16),
        grid=(b, RWKV_PAIRS),
        in_specs=[blk(0), blk(RWKV_PAIRS), blk(2 * RWKV_PAIRS), blk(0),
                  dblk(0), dblk(1), dblk(0), dblk(1), vec, vec, vec, vec, vec],
        out_specs=blk(0),
        scratch_shapes=[seg_b16,
                        seg_f32,
                        seg_f32,
                        seg_b16,
                        seg_b16,
                        seg_b16,
                        seg_f32,
                        pltpu.VMEM((N_DIRS, seg * 8, LANES), f32),
                        pltpu.VMEM((t, LANES), f32),
                        pltpu.VMEM((t, LANES), f32),
                        pltpu.VMEM((N_DIRS, t, LANES), f32)],
        compiler_params=_cparams(2),
        name="rwkv_scan",
    )(rkv, rkv, rkv, gate, lw, lw, eta, eta, row(k_k), row(k_a), row(r_k), row(ln_w), row(ln_b))


def _tail_b_weights(w_in, layer):
    last = w_in[layer, :, MAIN_COLS + TAIL_A_COLS:]
    assert last.shape[1] == TAIL_B_USED
    return jnp.pad(last, ((0, 0), (0, TAIL_TN - TAIL_B_USED)))[None]


def kernel(x, norm_mix, w_in, w_out, na_rpb, na_gain, gdn_conv_w, gdn_a_log, gdn_dt_bias, gdn_gain, rwkv_mu, rwkv_w0, rwkv_w_up, rwkv_a0, rwkv_a_up, rwkv_g_up, rwkv_k_k, rwkv_k_a, rwkv_r_k, rwkv_ln_w, rwkv_ln_b, norm_ffn, ffn_w_gate, ffn_w_up, ffn_w_down, moe_router, moe_w_gate, moe_w_up, moe_w_down, norm_final):
    b, t, d = x.shape
    n = b * t
    depth = w_in.shape[0]
    h = x.reshape(n, d)
    for layer in range(depth):
        u = rmsnorm(h, norm_mix[layer], bf16)
        pm = proj(u, w_in, layer, MAIN_COLS, tn=512).reshape(b, t, MAIN_COLS)
        pta = proj(u, w_in, layer, TAIL_A_COLS, tn=TAIL_TN, col_block0=MAIN_COLS // TAIL_TN).reshape(b, t, TAIL_A_COLS)
        ptb = proj(u, _tail_b_weights(w_in, layer), 0, TAIL_TN, tn=TAIL_TN).reshape(b, t, TAIL_TN)

        y_a = na_branch(pm, na_rpb[layer], na_gain[layer])
        qkv = gdn_prep(pm, gdn_conv_w, layer)
        y_b = gdn_branch(pm, pta, qkv, gdn_a_log[layer], gdn_dt_bias[layer], gdn_gain[layer])
        rkv, gate, lw, eta = rwkv_prep(pta, ptb, rwkv_mu[layer], rwkv_w0[layer], rwkv_w_up[layer],
                                       rwkv_a0[layer], rwkv_a_up[layer], rwkv_g_up[layer])
        y_c = rwkv_branch(rkv, gate, lw, eta, rwkv_k_k[layer], rwkv_k_a[layer], rwkv_r_k[layer],
                          rwkv_ln_w[layer], rwkv_ln_b[layer])
        h = outproj(y_a.reshape(n, -1), y_b.reshape(n, -1), y_c.reshape(n, -1), w_out, layer, h)

        i = layer // 2
        if layer % 2 == 0:
            u = rmsnorm(h, norm_ffn[layer], bf16)
            act = gateup(u, ffn_w_gate, ffn_w_up, i)
            half = ffn_w_down.shape[1] // 2
            assert half % LANES == 0
            for kb in range(2):
                h = down(act, ffn_w_down, i, h, kb, half)
        else:
            h = moe_ffn(h, norm_ffn[layer], moe_router[i], moe_w_gate, moe_w_up, moe_w_down, i)
    return rmsnorm(h, norm_final, f32).reshape(b, t, d)
```

```python
import functools

import jax
import jax.numpy as jnp
import numpy as np
from jax import lax
from jax.experimental import pallas as pl
from jax.experimental.pallas import tpu as pltpu

f32 = jnp.float32
bf16 = jnp.bfloat16
HI = lax.Precision.HIGHEST

LANES = 128
VMEM_LIMIT_BYTES = 56 * 1024 * 1024

D_MODEL = 4096
GRID_W = 64
NA_HEADS, NA_HD = 12, 128
NA_WIDTH = NA_HEADS * NA_HD
NA_WIN_ROWS, NA_WIN_COLS = 8, 16
GDN_HEADS, GDN_HD = 12, 128
GDN_WIDTH = GDN_HEADS * GDN_HD
GDN_CONV = 5
CHUNK = 64
RWKV_HEADS, RWKV_HD = 16, 64
RWKV_WIDTH = RWKV_HEADS * RWKV_HD
RWKV_PAIRS = RWKV_HEADS // 2
N_DIRS = 2
MAIN_COLS = 3 * NA_WIDTH + 4 * GDN_WIDTH
RWKV_COLS = 3 * RWKV_WIDTH + 64 + 64 + 128
SHIFT_COLS = RWKV_COLS
assert SHIFT_COLS % LANES == 0
GATE_COLS = 2 * N_DIRS * GDN_HEADS
TAIL_TN = 512
TAIL_A_COLS = 3072
TAIL_B_USED = GATE_COLS + RWKV_COLS - TAIL_A_COLS
assert MAIN_COLS % TAIL_TN == 0 and TAIL_A_COLS % TAIL_TN == 0 and 0 < TAIL_B_USED <= TAIL_TN
MOE_EXPERTS = 8
RMS_EPS = 1e-6
L2_EPS = 1e-6
RWKV_GN_EPS = 64e-5
NEG = -1e30


def _cparams(n_axes):
    return pltpu.CompilerParams(dimension_semantics=("arbitrary",) * n_axes,
                                vmem_limit_bytes=VMEM_LIMIT_BYTES)


def _nt(a, b, precision=None):
    return lax.dot_general(a, b, (((1,), (1,)), ((), ())), precision=precision,
                           preferred_element_type=f32)


def _tn(a, b, precision=None):
    return lax.dot_general(a, b, (((0,), (0,)), ((), ())), precision=precision,
                           preferred_element_type=f32)


def _mm(a, b, precision=None):
    return jnp.dot(a, b, precision=precision, preferred_element_type=f32)


def _sigmoid(x):
    return 1.0 / (1.0 + jnp.exp(-x))


def _softplus(x):
    return jnp.maximum(x, 0.0) + jnp.log(1.0 + jnp.exp(-jnp.abs(x)))


def _iota2(shape, axis):
    return lax.broadcasted_iota(jnp.int32, shape, axis)


def _rmsnorm_kernel(x_ref, g_ref, o_ref):
    x = x_ref[...]
    ms = jnp.mean(x * x, axis=-1, keepdims=True)
    o_ref[...] = (x * lax.rsqrt(ms + RMS_EPS) * g_ref[...]).astype(o_ref.dtype)


def rmsnorm(x, gain, out_dtype):
    n, d = x.shape
    tm = min(256, n)
    return pl.pallas_call(
        _rmsnorm_kernel,
        out_shape=jax.ShapeDtypeStruct((n, d), out_dtype),
        grid=(n // tm,),
        in_specs=[pl.BlockSpec((tm, d), lambda i: (i, 0)), pl.BlockSpec((1, d), lambda i: (0, 0))],
        out_specs=pl.BlockSpec((tm, d), lambda i: (i, 0)),
        compiler_params=_cparams(1),
        name="rmsnorm",
    )(x, gain.reshape(1, d))


def _rmsnorm_router_kernel(x_ref, g_ref, r_ref, o_ref, e_ref, w_ref):
    x = x_ref[...]
    ms = jnp.mean(x * x, axis=-1, keepdims=True)
    u = x * lax.rsqrt(ms + RMS_EPS) * g_ref[...]
    o_ref[...] = u
    logits = _mm(u, r_ref[...], HI)
    lane = _iota2(logits.shape, 1)
    lg = jnp.where(lane < MOE_EXPERTS, logits, NEG)
    m1 = jnp.max(lg, axis=-1, keepdims=True)
    i1 = jnp.min(jnp.where(lg == m1, lane, LANES), axis=-1, keepdims=True)
    lg2 = jnp.where(lane == i1, NEG, lg)
    m2 = jnp.max(lg2, axis=-1, keepdims=True)
    i2 = jnp.min(jnp.where(lg2 == m2, lane, LANES), axis=-1, keepdims=True)
    e2 = jnp.exp(m2 - m1)
    w1 = 1.0 / (1.0 + e2)
    w2 = e2 / (1.0 + e2)
    e_ref[...] = jnp.where(lane == 0, i1, jnp.where(lane == 1, i2, 0))
    w_ref[...] = jnp.where(lane == 0, w1, jnp.where(lane == 1, w2, 0.0))


def rmsnorm_router(x, gain, router):
    n, d = x.shape
    tm = min(256, n)
    r_pad = jnp.pad(router, ((0, 0), (0, LANES - router.shape[1])))
    return pl.pallas_call(
        _rmsnorm_router_kernel,
        out_shape=[jax.ShapeDtypeStruct((n, d), f32), jax.ShapeDtypeStruct((n, LANES), jnp.int32),
                   jax.ShapeDtypeStruct((n, LANES), f32)],
        grid=(n // tm,),
        in_specs=[pl.BlockSpec((tm, d), lambda i: (i, 0)), pl.BlockSpec((1, d), lambda i: (0, 0)),
                  pl.BlockSpec((d, LANES), lambda i: (0, 0))],
        out_specs=[pl.BlockSpec((tm, d), lambda i: (i, 0)), pl.BlockSpec((tm, LANES), lambda i: (i, 0)),
                   pl.BlockSpec((tm, LANES), lambda i: (i, 0))],
        compiler_params=_cparams(1),
        name="rmsnorm_router",
    )(x, gain.reshape(1, d), r_pad)


def _moe_rank_kernel(e_ref, rank_ref, cnt_ref, carry, *, tb):
    @pl.when(pl.program_id(0) == 0)
    def _():
        carry[...] = jnp.zeros(carry.shape, f32)

    ids = e_ref[...]
    lane = _iota2(ids.shape, 1)
    oh1 = lane == ids[:, 0:1]
    oh2 = lane == ids[:, 1:2]
    f1 = jnp.where(oh1, 1.0, 0.0)
    f2 = jnp.where(oh2, 1.0, 0.0)
    earlier = jnp.where(_iota2((tb, tb), 0) > _iota2((tb, tb), 1), 1.0, 0.0).astype(bf16)
    p1 = _mm(earlier, f1.astype(bf16))
    p2 = _mm(earlier, f2.astype(bf16))
    c1 = jnp.sum(f1, axis=0, keepdims=True)
    c2 = jnp.sum(f2, axis=0, keepdims=True)
    base = carry[0:1, :]
    r1 = jnp.sum(jnp.where(oh1, p1 + base, 0.0), axis=-1, keepdims=True)
    r2 = jnp.sum(jnp.where(oh2, p2 + (base + c1), 0.0), axis=-1, keepdims=True)
    rank_ref[...] = jnp.where(lane == 0, r1, jnp.where(lane == 1, r2, 0.0))
    carry[...] = jnp.broadcast_to(base + c1 + c2, carry.shape)
    cnt_ref[...] = carry[...]


def moe_rank(ids, tb=256):
    n = ids.shape[0]
    tb = min(tb, n)
    return pl.pallas_call(
        functools.partial(_moe_rank_kernel, tb=tb),
        out_shape=[jax.ShapeDtypeStruct((n, LANES), f32), jax.ShapeDtypeStruct((8, LANES), f32)],
        grid=(n // tb,),
        in_specs=[pl.BlockSpec((tb, LANES), lambda i: (i, 0))],
        out_specs=[pl.BlockSpec((tb, LANES), lambda i: (i, 0)), pl.BlockSpec((8, LANES), lambda i: (0, 0))],
        scratch_shapes=[pltpu.VMEM((8, LANES), f32)],
        compiler_params=_cparams(1),
        name="moe_rank",
    )(ids)


def _gather_rows_kernel(idx_ref, src_ref, o_ref, buf, sem, *, tm):
    base = pl.program_id(0) * tm

    def row_copy(r, row):
        return pltpu.make_async_copy(src_ref.at[pl.ds(row, 1), :], buf.at[pl.ds(r, 1), :], sem)

    def start(r, c):
        row_copy(r, idx_ref[base + r]).start()
        return c

    def wait(r, c):
        row_copy(r, 0).wait()
        return c

    lax.fori_loop(0, tm, start, 0, unroll=8)
    lax.fori_loop(0, tm, wait, 0, unroll=8)
    o_ref[...] = buf[...].astype(o_ref.dtype)


def gather_rows(src, idx, out_dtype, tm=256):
    m = idx.shape[0]
    d = src.shape[1]
    tm = min(tm, m)
    assert m % tm == 0
    return pl.pallas_call(
        functools.partial(_gather_rows_kernel, tm=tm),
        out_shape=jax.ShapeDtypeStruct((m, d), out_dtype),
        grid_spec=pltpu.PrefetchScalarGridSpec(
            num_scalar_prefetch=1, grid=(m // tm,),
            in_specs=[pl.BlockSpec(memory_space=pl.ANY)],
            out_specs=pl.BlockSpec((tm, d), lambda i, idx_ref: (i, 0)),
            scratch_shapes=[pltpu.VMEM((tm, d), f32), pltpu.SemaphoreType.DMA(())]),
        compiler_params=_cparams(1),
        name="gather_rows",
    )(idx, src)


def _moe_combine_kernel(d1_ref, d2_ref, y_ref, h_ref, w_ref, o_ref, buf, sems, *, tm):
    base = pl.program_id(0) * tm
    dests = (d1_ref, d2_ref)

    def row_copy(k, r, row):
        return pltpu.make_async_copy(y_ref.at[pl.ds(row, 1), :], buf.at[k, pl.ds(r, 1), :], sems.at[k])

    def start(r, c):
        for k in range(2):
            row_copy(k, r, dests[k][base + r]).start()
        return c

    def wait(r, c):
        for k in range(2):
            row_copy(k, r, 0).wait()
        return c

    lax.fori_loop(0, tm, start, 0, unroll=8)
    lax.fori_loop(0, tm, wait, 0, unroll=8)
    w = w_ref[...]
    o_ref[...] = h_ref[...] + w[:, 0:1] * buf[0] + w[:, 1:2] * buf[1]


def moe_combine(y, d1, d2, h, wts, tm=256):
    n, d = h.shape
    tm = min(tm, n)
    return pl.pallas_call(
        functools.partial(_moe_combine_kernel, tm=tm),
        out_shape=jax.ShapeDtypeStruct((n, d), f32),
        grid_spec=pltpu.PrefetchScalarGridSpec(
            num_scalar_prefetch=2, grid=(n // tm,),
            in_specs=[pl.BlockSpec(memory_space=pl.ANY),
                      pl.BlockSpec((tm, d), lambda i, a, b: (i, 0)),
                      pl.BlockSpec((tm, LANES), lambda i, a, b: (i, 0))],
            out_specs=pl.BlockSpec((tm, d), lambda i, a, b: (i, 0)),
            scratch_shapes=[pltpu.VMEM((2, tm, d), f32), pltpu.SemaphoreType.DMA((2,))]),
        compiler_params=_cparams(1),
        name="moe_combine",
    )(d1, d2, y, h, wts)


def _proj_t_kernel(x_ref, w_ref, o_ref, wb_ref):
    @pl.when(pl.program_id(1) == 0)
    def _():
        wb_ref[...] = w_ref[...].T.astype(bf16)

    o_ref[...] = _mm(x_ref[...], wb_ref[...]).astype(o_ref.dtype)


def proj(x, wt, w_index, n_cols, tn, out_dtype=f32, tm=1024, col_block0=0):
    n, k = x.shape
    tm = min(tm, n)
    assert n_cols % tn == 0 and n % tm == 0 and (col_block0 * tn + n_cols) <= wt.shape[1]
    return pl.pallas_call(
        _proj_t_kernel,
        out_shape=jax.ShapeDtypeStruct((n, n_cols), out_dtype),
        grid=(n_cols // tn, n // tm),
        in_specs=[pl.BlockSpec((tm, k), lambda j, i: (i, 0)),
                  pl.BlockSpec((None, tn, k), lambda j, i: (w_index, col_block0 + j, 0))],
        out_specs=pl.BlockSpec((tm, tn), lambda j, i: (i, j)),
        scratch_shapes=[pltpu.VMEM((k, tn), bf16)],
        compiler_params=_cparams(2),
        name="proj",
    )(x, wt)


def _outproj_kernel(xa_ref, xb_ref, xc_ref, w_ref, res_ref, o_ref, wb_ref):
    @pl.when(pl.program_id(1) == 0)
    def _():
        wb_ref[...] = w_ref[...].astype(bf16)

    acc = _mm(xa_ref[...], wb_ref[0:NA_WIDTH, :])
    acc += _mm(xb_ref[...], wb_ref[NA_WIDTH:NA_WIDTH + GDN_WIDTH, :])
    acc += _mm(xc_ref[...], wb_ref[NA_WIDTH + GDN_WIDTH:, :])
    o_ref[...] = res_ref[...] + acc


def outproj(ya, yb, yc, w, layer, res, tn=512, tm=1024):
    n = ya.shape[0]
    d = w.shape[-1]
    tm = min(tm, n)
    return pl.pallas_call(
        _outproj_kernel,
        out_shape=jax.ShapeDtypeStruct((n, d), f32),
        grid=(d // tn, n // tm),
        in_specs=[pl.BlockSpec((tm, NA_WIDTH), lambda j, i: (i, 0)),
                  pl.BlockSpec((tm, GDN_WIDTH), lambda j, i: (i, 0)),
                  pl.BlockSpec((tm, RWKV_WIDTH), lambda j, i: (i, 0)),
                  pl.BlockSpec((None, D_MODEL, tn), lambda j, i: (layer, 0, j)),
                  pl.BlockSpec((tm, tn), lambda j, i: (i, j))],
        out_specs=pl.BlockSpec((tm, tn), lambda j, i: (i, j)),
        scratch_shapes=[pltpu.VMEM((D_MODEL, tn), bf16)],
        compiler_params=_cparams(2),
        name="outproj",
    )(ya, yb, yc, w, res)


def _gateup_kernel(x_ref, wg_ref, wu_ref, o_ref, wgb_ref, wub_ref):
    @pl.when(pl.program_id(1) == 0)
    def _():
        wgb_ref[...] = wg_ref[...].astype(bf16)
        wub_ref[...] = wu_ref[...].astype(bf16)

    x = x_ref[...]
    g = _mm(x, wgb_ref[...])
    u = _mm(x, wub_ref[...])
    o_ref[...] = (g * _sigmoid(g) * u).astype(o_ref.dtype)


def gateup(x, w_gate, w_up, w_index, tn=256, tm=1024):
    n, k = x.shape
    ff = w_gate.shape[-1]
    tm = min(tm, n)
    return pl.pallas_call(
        _gateup_kernel,
        out_shape=jax.ShapeDtypeStruct((n, ff), bf16),
        grid=(ff // tn, n // tm),
        in_specs=[pl.BlockSpec((tm, k), lambda j, i: (i, 0)),
                  pl.BlockSpec((None, k, tn), lambda j, i: (w_index, 0, j)),
                  pl.BlockSpec((None, k, tn), lambda j, i: (w_index, 0, j))],
        out_specs=pl.BlockSpec((tm, tn), lambda j, i: (i, j)),
        scratch_shapes=[pltpu.VMEM((k, tn), bf16), pltpu.VMEM((k, tn), bf16)],
        compiler_params=_cparams(2),
        name="gateup",
    )(x, w_gate, w_up)


def _tile_changed(te_ref, i):
    return (i == 0) | (te_ref[i] != te_ref[jnp.maximum(i - 1, 0)])


def _moe_gateup_kernel(te_ref, nu_ref, x_ref, wg_ref, wu_ref, o_ref, wgb_ref, wub_ref):
    i = pl.program_id(1)
    used = i < nu_ref[0]

    @pl.when(used & _tile_changed(te_ref, i))
    def _():
        wgb_ref[...] = wg_ref[...].astype(bf16)
        wub_ref[...] = wu_ref[...].astype(bf16)

    @pl.when(used)
    def _():
        x = x_ref[...]
        g = _mm(x, wgb_ref[...])
        u = _mm(x, wub_ref[...])
        o_ref[...] = (g * _sigmoid(g) * u).astype(o_ref.dtype)

    @pl.when(jnp.logical_not(used))
    def _():
        o_ref[...] = jnp.zeros(o_ref.shape, o_ref.dtype)


def _moe_down_kernel(te_ref, nu_ref, x_ref, w_ref, o_ref, wb_ref):
    i = pl.program_id(1)
    used = i < nu_ref[0]

    @pl.when(used & _tile_changed(te_ref, i))
    def _():
        wb_ref[...] = w_ref[...].astype(bf16)

    @pl.when(used)
    def _():
        o_ref[...] = _mm(x_ref[...], wb_ref[...])

    @pl.when(jnp.logical_not(used))
    def _():
        o_ref[...] = jnp.zeros(o_ref.shape, o_ref.dtype)


def _grouped_specs(tm, k, tn, w_index, n_w):
    last = lambda i, nu: jnp.minimum(i, nu[0] - 1)
    x_spec = pl.BlockSpec((tm, k), lambda j, i, te, nu: (last(i, nu), 0))
    w_spec = pl.BlockSpec((None, None, k, tn), lambda j, i, te, nu: (w_index, te[last(i, nu)], 0, j))
    o_spec = pl.BlockSpec((tm, tn), lambda j, i, te, nu: (i, j))
    return [x_spec] + [w_spec] * n_w, o_spec


def moe_gateup(x, tile_expert, n_used, w_gate, w_up, w_index, tm, tn=512):
    m, k = x.shape
    ff = w_gate.shape[-1]
    in_specs, out_spec = _grouped_specs(tm, k, tn, w_index, 2)
    return pl.pallas_call(
        _moe_gateup_kernel,
        out_shape=jax.ShapeDtypeStruct((m, ff), bf16),
        grid_spec=pltpu.PrefetchScalarGridSpec(
            num_scalar_prefetch=2, grid=(ff // tn, m // tm), in_specs=in_specs, out_specs=out_spec,
            scratch_shapes=[pltpu.VMEM((k, tn), bf16), pltpu.VMEM((k, tn), bf16)]),
        compiler_params=_cparams(2),
        name="moe_gateup",
    )(tile_expert, n_used, x, w_gate, w_up)


def moe_down(x, tile_expert, n_used, w_down, w_index, tm, tn=512):
    m, k = x.shape
    d = w_down.shape[-1]
    in_specs, out_spec = _grouped_specs(tm, k, tn, w_index, 1)
    return pl.pallas_call(
        _moe_down_kernel,
        out_shape=jax.ShapeDtypeStruct((m, d), f32),
        grid_spec=pltpu.PrefetchScalarGridSpec(
            num_scalar_prefetch=2, grid=(d // tn, m // tm), in_specs=in_specs, out_specs=out_spec,
            scratch_shapes=[pltpu.VMEM((k, tn), bf16)]),
        compiler_params=_cparams(2),
        name="moe_down",
    )(tile_expert, n_used, x, w_down)


def moe_ffn(h, gain, router, w_gate, w_up, w_down, w_index, tm=512):
    n, d = h.shape
    ne = w_gate.shape[1]
    tm = min(tm, n)
    u, ids, wts = rmsnorm_router(h, gain, router)
    rank, counts = moe_rank(ids)
    counts = counts[0, :ne].astype(jnp.int32)
    padded = (counts + tm - 1) // tm * tm
    ends = jnp.cumsum(padded)
    e2 = ids[:, :2]
    dest = (ends - padded)[e2] + rank[:, :2].astype(jnp.int32)
    n_tiles = 2 * n // tm + ne
    tile_expert = jnp.minimum(jnp.searchsorted(ends, jnp.arange(n_tiles) * tm, side="right"), ne - 1)
    n_used = (ends[-1] // tm).reshape(1)
    token = jnp.broadcast_to(jnp.arange(n, dtype=jnp.int32)[:, None], (n, 2))
    src = jnp.zeros((n_tiles * tm,), jnp.int32).at[dest.reshape(-1)].set(token.reshape(-1))
    xs = gather_rows(u, src, bf16)
    act = moe_gateup(xs, tile_expert.astype(jnp.int32), n_used.astype(jnp.int32), w_gate, w_up, w_index, tm)
    ys = moe_down(act, tile_expert.astype(jnp.int32), n_used.astype(jnp.int32), w_down, w_index, tm)
    return moe_combine(ys, dest[:, 0], dest[:, 1], h, wts)


def _down_kernel(x_ref, w_ref, res_ref, o_ref, wb_ref):
    @pl.when(pl.program_id(1) == 0)
    def _():
        wb_ref[...] = w_ref[...].astype(bf16)

    o_ref[...] = res_ref[...] + _mm(x_ref[...], wb_ref[...])


def down(x, w, w_index, res, k_block, tk, tn=512, tm=512):
    n = x.shape[0]
    d = w.shape[-1]
    tm = min(tm, n)
    return pl.pallas_call(
        _down_kernel,
        out_shape=jax.ShapeDtypeStruct((n, d), f32),
        grid=(d // tn, n // tm),
        in_specs=[pl.BlockSpec((tm, tk), lambda j, i: (i, k_block)),
                  pl.BlockSpec((None, tk, tn), lambda j, i: (w_index, k_block, j)),
                  pl.BlockSpec((tm, tn), lambda j, i: (i, j))],
        out_specs=pl.BlockSpec((tm, tn), lambda j, i: (i, j)),
        scratch_shapes=[pltpu.VMEM((tk, tn), bf16)],
        compiler_params=_cparams(2),
        name="down",
    )(x, w, res)


def _na_bias_table(rpb):
    c = np.arange(GRID_W)[:, None]
    kc = np.arange(GRID_W)[None, :]
    cs = np.clip(c - NA_WIN_COLS // 2, 0, GRID_W - NA_WIN_COLS)
    valid = (kc >= cs) & (kc < cs + NA_WIN_COLS)
    dc = kc - c + (NA_WIN_COLS - 1)
    onehot = (valid[None] & (dc[None] == np.arange(2 * NA_WIN_COLS - 1)[:, None, None])).astype(np.float32)
    cols = jnp.einsum("hrm,mck->hrck", rpb.astype(f32), jnp.asarray(onehot), precision=HI)
    lo = NA_WIN_ROWS - 1
    tab = jnp.stack([cols[:, lo - off:lo - off + NA_WIN_ROWS] for off in range(NA_WIN_ROWS)], axis=1)
    tab = tab.transpose(0, 1, 3, 2, 4) + jnp.asarray(np.where(valid, 0.0, NEG).astype(np.float32))[:, None, :]
    return tab.reshape(rpb.shape[0], NA_WIN_ROWS, GRID_W, NA_WIN_ROWS * GRID_W)


def _na_kernel(q_ref, k_ref, v_ref, bias_ref, gain_ref, o_ref, kb_ref, vb_ref, *, rows, rows_per_step, lockstep):
    g = pl.program_id(2)

    @pl.when(g == 0)
    def _():
        kb_ref[...] = k_ref[...].astype(bf16)
        vb_ref[...] = v_ref[...].astype(bf16)

    scale = NA_HD ** -0.5
    win = NA_WIN_ROWS * GRID_W

    def body(it, carry):
        qrows, krows, ss = [], [], []
        for li in range(lockstep):
            i = it * lockstep + li
            r = g * rows_per_step + i
            sr = jnp.clip(r - NA_WIN_ROWS // 2, 0, rows - NA_WIN_ROWS)
            qrows.append(pl.ds(pl.multiple_of(i * GRID_W, GRID_W), GRID_W))
            krows.append(pl.ds(pl.multiple_of(sr * GRID_W, GRID_W), win))
            q = q_ref[qrows[li], :].astype(bf16)
            ss.append(_nt(q, kb_ref[krows[li], :]) * scale + bias_ref[r - sr])
        ps, ls = [], []
        for s in ss:
            p = jnp.exp(s - jnp.max(s, axis=-1, keepdims=True))
            ls.append(jnp.sum(p, axis=-1, keepdims=True))
            ps.append(p)
        ovs = [_mm((p / l).astype(bf16), vb_ref[kr, :]) for p, l, kr in zip(ps, ls, krows)]
        for ov, qr in zip(ovs, qrows):
            ms = jnp.mean(ov * ov, axis=-1, keepdims=True)
            o_ref[qr, :] = (ov * lax.rsqrt(ms + RMS_EPS) * gain_ref[...]).astype(o_ref.dtype)
        return carry

    lax.fori_loop(0, rows_per_step // lockstep, body, 0)


def na_branch(pm, rpb, gain, rows_per_step=8):
    b, t, _ = pm.shape
    rows = t // GRID_W
    assert rows >= NA_WIN_ROWS and rows % rows_per_step == 0
    tq = rows_per_step * GRID_W
    bias = _na_bias_table(rpb)
    return pl.pallas_call(
        functools.partial(_na_kernel, rows=rows, rows_per_step=rows_per_step, lockstep=4),
        out_shape=jax.ShapeDtypeStruct((b, t, NA_WIDTH), bf16),
        grid=(b, NA_HEADS, rows // rows_per_step),
        in_specs=[pl.BlockSpec((None, tq, NA_HD), lambda bi, h, g: (bi, g, h)),
                  pl.BlockSpec((None, t, NA_HD), lambda bi, h, g: (bi, 0, NA_HEADS + h)),
                  pl.BlockSpec((None, t, NA_HD), lambda bi, h, g: (bi, 0, 2 * NA_HEADS + h)),
                  pl.BlockSpec((None, NA_WIN_ROWS, GRID_W, NA_WIN_ROWS * GRID_W), lambda bi, h, g: (h, 0, 0, 0)),
                  pl.BlockSpec((None, 1, NA_HD), lambda bi, h, g: (h, 0, 0))],
        out_specs=pl.BlockSpec((None, tq, NA_HD), lambda bi, h, g: (bi, g, h)),
        scratch_shapes=[pltpu.VMEM((t, NA_HD), bf16), pltpu.VMEM((t, NA_HD), bf16)],
        compiler_params=_cparams(3),
        name="na_attention",
    )(pm, pm, pm, bias, gain.reshape(NA_HEADS, 1, NA_HD))


def _gdn_prep_kernel(x_ref, w_ref, o_ref, xp_ref, *, t, tc):
    c = pl.program_id(1)
    pad = 8
    xp_ref[0:pad, :] = jnp.zeros((pad, LANES), f32)
    xp_ref[t + pad:t + 2 * pad, :] = jnp.zeros((pad, LANES), f32)
    xp_ref[pad:t + pad, :] = x_ref[...]
    half = GDN_CONV // 2
    for ci in range(t // tc):
        acc = jnp.zeros((tc, LANES), f32)
        for j in range(GDN_CONV):
            s0 = ci * tc + pad - half + j
            acc = acc + xp_ref[s0:s0 + tc, :] * w_ref[j:j + 1, :]
        y = acc * _sigmoid(acc)
        inv = lax.rsqrt(jnp.sum(y * y, axis=-1, keepdims=True) + L2_EPS)
        fac = jnp.where(c < GDN_HEADS, inv * GDN_HD ** -0.5, jnp.where(c < 2 * GDN_HEADS, inv, 1.0))
        o_ref[ci * tc:(ci + 1) * tc, :] = y * fac


def gdn_prep(pm, conv_w, layer):
    b, t, _ = pm.shape
    tc = min(512, t)
    col0 = 3 * NA_WIDTH // LANES
    nblk = 3 * GDN_WIDTH // LANES
    return pl.pallas_call(
        functools.partial(_gdn_prep_kernel, t=t, tc=tc),
        out_shape=jax.ShapeDtypeStruct((b, t, 3 * GDN_WIDTH), f32),
        grid=(b, nblk),
        in_specs=[pl.BlockSpec((None, t, LANES), lambda bi, c: (bi, 0, col0 + c)),
                  pl.BlockSpec((None, GDN_CONV, LANES), lambda bi, c: (layer, 0, c))],
        out_specs=pl.BlockSpec((None, t, LANES), lambda bi, c: (bi, 0, c)),
        scratch_shapes=[pltpu.VMEM((t + 16, LANES), f32)],
        compiler_params=_cparams(2),
        name="gdn_prep",
    )(pm, conv_w)


def _level_masks(ii, jj, size):
    masks = []
    s = 1
    while s < size:
        masks.append(((ii // (2 * s)) == (jj // (2 * s))) & ((ii // s) != (jj // s)))
        s *= 2
    return masks


def _split2(x):
    hi = x.astype(bf16)
    return hi, (x - hi.astype(f32)).astype(bf16)


def _split3(x):
    x1 = x.astype(bf16)
    r = x - x1.astype(f32)
    x2 = r.astype(bf16)
    return x1, x2, (r - x2.astype(f32)).astype(bf16)


def _mm_f32(a, b):
    a1, a2 = _split2(a)
    b1, b2 = _split2(b)
    return _mm(jnp.concatenate([a1, a2, a1], axis=1), jnp.concatenate([b1, b1, b2], axis=0))


def _mm_sel(sel, x):
    x1, x2, x3 = _split3(x)
    return _mm(jnp.concatenate([sel, sel, sel], axis=1), jnp.concatenate([x1, x2, x3], axis=0))


def _mm_rsel(x, sel):
    x1, x2, x3 = _split3(x)
    return _mm(jnp.concatenate([x1, x2, x3], axis=1), jnp.concatenate([sel, sel, sel], axis=0))


def _nt_sel(sel, x):
    x1, x2, x3 = _split3(x)
    return _nt(jnp.concatenate([sel, sel, sel], axis=1), jnp.concatenate([x1, x2, x3], axis=1))


def _tri_inverse(ms, eye, masks, hook):
    xs = [eye - jnp.where(masks[0], m, 0.0) for m in ms]
    for mask in masks[1:]:
        xcs = [_mm_f32(x, jnp.where(mask, m, 0.0)) for x, m in zip(xs, ms)]
        hook()
        xs = [x - _mm_f32(xc, x) for x, xc in zip(xs, xcs)]
        hook()
    return xs


def _stage_hook(gen, n_stages, n_hooks):
    state = {"calls": 0, "done": 0}

    def hook():
        state["calls"] += 1
        target = -(-state["calls"] * n_stages // n_hooks)
        while state["done"] < min(target, n_stages):
            next(gen, None)
            state["done"] += 1

    return hook


def _no_hook():
    return None


def _pair_masks():
    ii = _iota2((LANES, LANES), 0)
    jj = _iota2((LANES, LANES), 1)
    same = (ii // CHUNK) == (jj // CHUNK)
    ti, tj = ii % CHUNK, jj % CHUNK
    return ii, jj, same, ti, tj


def _gdn_gates_kernel(ab_ref, alog_ref, dtb_ref, gt_ref, grow_ref, *, tb):
    ii, jj, same, ti, tj = _pair_masks()
    tri = ((same & (ti >= tj)).astype(bf16), (same & (ti <= tj)).astype(bf16))
    eye = (ii == jj).astype(bf16)
    for p in range(tb // LANES):
        rows = slice(p * LANES, (p + 1) * LANES)
        ab = ab_ref[rows, :]
        g = -jnp.exp(alog_ref[...]) * _softplus(ab + dtb_ref[...])
        fwd = _mm_sel(tri[0], g)
        bwd = _mm_sel(tri[1], g)
        gt = jnp.where(jj < GDN_HEADS, fwd, jnp.where(jj < N_DIRS * GDN_HEADS, bwd, _sigmoid(ab)))
        gt_ref[rows, :] = gt
        grow_ref[p] = _nt_sel(eye, gt)


def gdn_gates(pt, a_log, dt_bias, tb=512):
    b, t, _ = pt.shape
    tb = min(tb, t)
    npad = LANES - N_DIRS * GDN_HEADS
    alog = jnp.pad(a_log.reshape(1, -1), ((0, 0), (0, npad)))
    dtb = jnp.pad(dt_bias.reshape(1, -1), ((0, 0), (0, npad)))
    vec = pl.BlockSpec((1, LANES), lambda bi, i: (0, 0))
    return pl.pallas_call(
        functools.partial(_gdn_gates_kernel, tb=tb),
        out_shape=[jax.ShapeDtypeStruct((b, t, LANES), f32),
                   jax.ShapeDtypeStruct((b, t // LANES, LANES, LANES), f32)],
        grid=(b, t // tb),
        in_specs=[pl.BlockSpec((None, tb, LANES), lambda bi, i: (bi, i, 0)), vec, vec],
        out_specs=[pl.BlockSpec((None, tb, LANES), lambda bi, i: (bi, i, 0)),
                   pl.BlockSpec((None, tb // LANES, LANES, LANES), lambda bi, i: (bi, i, 0, 0))],
        compiler_params=_cparams(2),
        name="gdn_gates",
    )(pt, alog, dtb)


def _gdn_kernel(q_ref, k_ref, v_ref, z_ref, gt_ref, grow_ref, gain_ref, o_ref,
                u_s, w_s, qg_s, kg_s, a_s, el_s, oacc, *, t, seg):
    h = pl.program_id(1)
    npairs = t // LANES
    nseg = npairs // seg
    assert npairs % seg == 0 and npairs % 2 == 0
    ii, jj, same, ti, tj = _pair_masks()
    eye = (ii == jj).astype(f32)
    lvl = _level_masks(ti, tj, CHUNK)
    masks = ((same & (ti >= tj), same & (ti > tj)), (same & (ti <= tj), same & (ti < tj)))
    dirs = range(N_DIRS)
    n_hooks = 2 * (len(lvl) - 1) + 1
    n_stages = seg * 2 * 2

    def tile_of(d, sg, j):
        p = sg * seg + j
        return p if d == 0 else npairs - 1 - p

    def prep_seg(sg, slot, hook):
        probs = []
        for j in range(seg):
            for d in dirs:
                p = tile_of(d, sg, j)
                rows = pl.ds(pl.multiple_of(p * LANES, LANES), LANES)
                k = k_ref[rows, :]
                probs.append(dict(d=d, j=j, p=p, q=q_ref[rows, :], k=k, v=v_ref[rows, :], gt=gt_ref[rows, :],
                                  kb16=k.astype(bf16)))
        for pb in probs:
            pb["qk"] = _nt(pb["q"].astype(bf16), pb["kb16"])
            pb["kk"] = _nt(pb["kb16"], pb["kb16"])
        hook()
        for pb in probs:
            d = pb["d"]
            la = d * GDN_HEADS + h
            lb = N_DIRS * GDN_HEADS + la
            incl, strict = masks[d]
            gcol = jnp.sum(jnp.where(jj == la, pb["gt"], 0.0), axis=-1, keepdims=True)
            bcol = jnp.sum(jnp.where(jj == lb, pb["gt"], 0.0), axis=-1, keepdims=True)
            grow = grow_ref[pb["p"], pl.ds(la, 1), :]
            pb["dec"] = jnp.exp(jnp.where(incl, gcol - grow, NEG))
            pb["gcol"], pb["bcol"] = gcol, bcol
            pb["m"] = jnp.where(strict, pb["kk"] * bcol * pb["dec"], 0.0)
        tinvs = _tri_inverse([pb["m"] for pb in probs], eye, lvl, hook)
        uws = []
        for pb, tinv in zip(probs, tinvs):
            gcol, bcol = pb["gcol"], pb["bcol"]
            pb["eg"] = jnp.exp(gcol)
            rhs = jnp.concatenate([pb["v"] * bcol, pb["k"] * (bcol * pb["eg"])], axis=1).astype(bf16)
            t1, t2 = _split2(tinv)
            uws.append(_mm(jnp.concatenate([t1, t2], axis=1), jnp.concatenate([rhs, rhs], axis=0)))
        for pb, uw in zip(probs, uws):
            d, gcol = pb["d"], pb["gcol"]
            srows = slice(pb["j"] * LANES, (pb["j"] + 1) * LANES)
            ends = (CHUNK - 1, 2 * CHUNK - 1) if d == 0 else (0, CHUNK)
            gl = [gcol[e:e + 1, :] for e in ends]
            gtot = jnp.concatenate([jnp.broadcast_to(g, (CHUNK, 1)) for g in gl], axis=0)
            a2 = jnp.where(masks[d][0], pb["qk"] * pb["dec"], 0.0)
            u_s[slot, d, srows, :] = uw[:, :GDN_HD]
            w_s[slot, d, srows, :] = uw[:, GDN_HD:].astype(bf16)
            a_s[slot, d, srows, :] = jnp.concatenate([a2[:CHUNK, :CHUNK], a2[CHUNK:, CHUNK:]], axis=0).astype(bf16)
            qg_s[slot, d, srows, :] = (pb["q"] * pb["eg"]).astype(bf16)
            kg_s[slot, d, srows, :] = (pb["k"] * jnp.exp(gtot - gcol)).astype(bf16)
            el_s[slot, d, pb["j"] * 16:(pb["j"] + 1) * 16, :] = jnp.concatenate(
                [jnp.broadcast_to(jnp.exp(g), (8, LANES)) for g in gl], axis=0)

    def scan_seg(sg, slot, st):
        halves = ((0, 1), (1, 0))
        for j in range(seg):
            for step in range(2):
                sl = [slice(j * LANES + halves[d][step] * CHUNK, j * LANES + (halves[d][step] + 1) * CHUNK)
                      for d in dirs]
                sb = [st[d].astype(bf16) for d in dirs]
                ws = [_mm(w_s[slot, d, sl[d], :], sb[d]) for d in dirs]
                qs = [_mm(qg_s[slot, d, sl[d], :], sb[d]) for d in dirs]
                yield
                vnb = [(u_s[slot, d, sl[d], :] - ws[d]).astype(bf16) for d in dirs]
                av = [_mm(a_s[slot, d, sl[d], :], vnb[d]) for d in dirs]
                kv = [_tn(kg_s[slot, d, sl[d], :], vnb[d]) for d in dirs]
                yield
                for d in dirs:
                    e0 = j * 16 + halves[d][step] * 8
                    st[d] = st[d] * el_s[slot, d, e0:e0 + 8, :][0:1, :] + kv[d]
                    start = tile_of(d, sg, j) * LANES + halves[d][step] * CHUNK
                    oacc[pl.ds(pl.multiple_of(start, CHUNK), CHUNK), :] += qs[d] + av[d]

    oacc[...] = jnp.zeros(oacc.shape, f32)
    prep_seg(0, 0, _no_hook)

    def body(sg, states):
        st = list(states)
        gen = scan_seg(sg, sg % 2, st)
        prep_seg(sg + 1, (sg + 1) % 2, _stage_hook(gen, n_stages, n_hooks))
        for _ in gen:
            pass
        return tuple(st)

    s0 = jnp.zeros((GDN_HD, GDN_HD), f32)
    st = list(lax.fori_loop(0, nseg - 1, body, (s0, s0)))
    for _ in scan_seg(nseg - 1, (nseg - 1) % 2, st):
        pass

    tb = min(512, t)
    for ci in range(t // tb):
        rows = slice(ci * tb, (ci + 1) * tb)
        o = oacc[rows, :]
        z = z_ref[rows, :]
        ms = jnp.mean(o * o, axis=-1, keepdims=True)
        o_ref[rows, :] = (o * lax.rsqrt(ms + RMS_EPS) * gain_ref[...] * (z * _sigmoid(z))).astype(o_ref.dtype)


def gdn_branch(pm, pt, qkv, a_log, dt_bias, gain):
    b, t, _ = pm.shape
    assert t % LANES == 0
    seg = min(4, t // LANES // 2)
    rows = seg * LANES
    zcol0 = (3 * NA_WIDTH + 3 * GDN_WIDTH) // LANES
    gt, grow = gdn_gates(pt, a_log, dt_bias)
    hd = lambda off: (lambda bi, h: (bi, 0, off + h))
    return pl.pallas_call(
        functools.partial(_gdn_kernel, t=t, seg=seg),
        out_shape=jax.ShapeDtypeStruct((b, t, GDN_WIDTH), bf16),
        grid=(b, GDN_HEADS),
        in_specs=[pl.BlockSpec((None, t, GDN_HD), hd(0)),
                  pl.BlockSpec((None, t, GDN_HD), hd(GDN_HEADS)),
                  pl.BlockSpec((None, t, GDN_HD), hd(2 * GDN_HEADS)),
                  pl.BlockSpec((None, t, GDN_HD), hd(zcol0)),
                  pl.BlockSpec((None, t, LANES), lambda bi, h: (bi, 0, 0)),
                  pl.BlockSpec((None, t // LANES, LANES, LANES), lambda bi, h: (bi, 0, 0, 0)),
                  pl.BlockSpec((1, LANES), lambda bi, h: (0, 0))],
        out_specs=pl.BlockSpec((None, t, GDN_HD), hd(0)),
        scratch_shapes=[pltpu.VMEM((2, N_DIRS, rows, GDN_HD), f32),
                        pltpu.VMEM((2, N_DIRS, rows, GDN_HD), bf16),
                        pltpu.VMEM((2, N_DIRS, rows, GDN_HD), bf16),
                        pltpu.VMEM((2, N_DIRS, rows, GDN_HD), bf16),
                        pltpu.VMEM((2, N_DIRS, rows, CHUNK), bf16),
                        pltpu.VMEM((2, N_DIRS, seg * 16, LANES), f32),
                        pltpu.VMEM((t, GDN_HD), f32)],
        compiler_params=_cparams(2),
        name="gdn_scan",
    )(qkv, qkv, qkv, pm, gt, grow, gain.reshape(1, GDN_HD))


def _rwkv_prep_kernel(cur_a, prev_a, next_a, cur_b, prev_b, next_b, mu_ref, wup_ref, aup_ref, gup_ref, w0_ref,
                      a0_ref, rkv_ref, gate_ref, lw_ref, eta_ref, buf, *, tb):
    i = pl.program_id(1)
    nt = pl.num_programs(1)
    na = TAIL_A_COLS - GATE_COLS
    buf[8:8 + tb, 0:na] = cur_a[:, GATE_COLS:]
    buf[8:8 + tb, na:] = cur_b[:, 0:TAIL_B_USED]
    has_prev = i > 0
    has_next = i < nt - 1
    buf[7:8, 0:na] = jnp.where(has_prev, prev_a[7:8, GATE_COLS:], 0.0)
    buf[7:8, na:] = jnp.where(has_prev, prev_b[7:8, 0:TAIL_B_USED], 0.0)
    buf[8 + tb:9 + tb, 0:na] = jnp.where(has_next, next_a[0:1, GATE_COLS:], 0.0)
    buf[8 + tb:9 + tb, na:] = jnp.where(has_next, next_b[0:1, 0:TAIL_B_USED], 0.0)

    def mixed(c0, c1):
        x = buf[8:8 + tb, c0:c1]
        sh = 0.5 * (buf[7:7 + tb, c0:c1] + buf[9:9 + tb, c0:c1])
        return x + (sh - x) * mu_ref[:, c0:c1]

    for gi in range(3):
        c0 = gi * RWKV_WIDTH
        rkv_ref[:, c0:c0 + RWKV_WIDTH] = mixed(c0, c0 + RWKV_WIDTH)
    xwa = mixed(3 * RWKV_WIDTH, 3 * RWKV_WIDTH + LANES)
    xg = mixed(3 * RWKV_WIDTH + LANES, 3 * RWKV_WIDTH + 2 * LANES)
    gate_ref[...] = _mm(_sigmoid(xg).astype(bf16), gup_ref[...].astype(bf16))
    txw = jnp.tanh(xwa)
    for d in range(N_DIRS):
        log_w = -_softplus(-(w0_ref[d] + _mm_f32(txw, wup_ref[d]))) - 0.5
        lw_ref[d] = -jnp.exp(log_w)
        eta_ref[d] = _sigmoid(a0_ref[d] + _mm_f32(xwa, aup_ref[d]))


def rwkv_prep(pta, ptb, mu, w0, w_up, a0, a_up, g_up, tb=256):
    b, t, _ = pta.shape
    tb = min(tb, t)
    nb8 = tb // 8
    mu_p = mu.reshape(1, SHIFT_COLS)
    lora = w_up.shape[1]
    wup_p = jnp.pad(w_up, ((0, 0), (0, LANES - lora), (0, 0)))
    aup_p = jnp.pad(a_up, ((0, 0), (lora, LANES - lora - a_up.shape[1]), (0, 0)))
    out_sds = lambda *lead: jax.ShapeDtypeStruct(lead + (b, t, RWKV_WIDTH), f32)
    full = lambda shape: pl.BlockSpec(shape, lambda bi, i: (0,) * len(shape))

    def halo(cols, side):
        if side == 0:
            return pl.BlockSpec((None, tb, cols), lambda bi, i: (bi, i, 0))
        if side < 0:
            return pl.BlockSpec((None, 8, cols), lambda bi, i: (bi, jnp.maximum(i * nb8 - 1, 0), 0))
        return pl.BlockSpec((None, 8, cols), lambda bi, i: (bi, jnp.minimum((i + 1) * nb8, t // 8 - 1), 0))

    return pl.pallas_call(
        functools.partial(_rwkv_prep_kernel, tb=tb),
        out_shape=[jax.ShapeDtypeStruct((b, t, 3 * RWKV_WIDTH), f32), out_sds(), out_sds(N_DIRS), out_sds(N_DIRS)],
        grid=(b, t // tb),
        in_specs=[halo(TAIL_A_COLS, 0), halo(TAIL_A_COLS, -1), halo(TAIL_A_COLS, 1),
                  halo(TAIL_TN, 0), halo(TAIL_TN, -1), halo(TAIL_TN, 1),
                  full((1, SHIFT_COLS)), full((N_DIRS, LANES, RWKV_WIDTH)), full((N_DIRS, LANES, RWKV_WIDTH)),
                  full((LANES, RWKV_WIDTH)), full((N_DIRS, 1, RWKV_WIDTH)), full((N_DIRS, 1, RWKV_WIDTH))],
        out_specs=[pl.BlockSpec((None, tb, 3 * RWKV_WIDTH), lambda bi, i: (bi, i, 0)),
                   pl.BlockSpec((None, tb, RWKV_WIDTH), lambda bi, i: (bi, i, 0)),
                   pl.BlockSpec((N_DIRS, None, tb, RWKV_WIDTH), lambda bi, i: (0, bi, i, 0)),
                   pl.BlockSpec((N_DIRS, None, tb, RWKV_WIDTH), lambda bi, i: (0, bi, i, 0))],
        scratch_shapes=[pltpu.VMEM((tb + 16, SHIFT_COLS), f32)],
        compiler_params=_cparams(2),
        name="rwkv_prep",
    )(pta, pta, pta, ptb, ptb, ptb, mu_p, wup_p, aup_p, g_up, w0.reshape(N_DIRS, 1, -1), a0.reshape(N_DIRS, 1, -1))


def _stack2(x, m0, m1):
    return jnp.concatenate([x * m0, x * m1], axis=0)


def _rwkv_kernel(r_ref, k_ref, v_ref, gate_ref, lw0_ref, lw1_ref, eta0_ref, eta1_ref,
                 kk_ref, ka_ref, rk_ref, lnw_ref, lnb_ref, o_ref,
                 wt_s, uv_s, yv_s, arb_s, rt_s, bh_s, sk_s, eg_s, yacc, kk_s, gc_s, *, t, seg):
    n = t // CHUNK
    nseg = n // seg
    assert n % seg == 0 and n % 2 == 0 and nseg >= 2
    c2 = 2 * CHUNK
    lane = _iota2((1, LANES), 1)
    m0 = (lane < RWKV_HD).astype(f32)
    m1 = 1.0 - m0
    ii, jj, same, ti, tj = _pair_masks()
    bd = same.astype(bf16)
    eye = (ii == jj).astype(f32)
    lvl = _level_masks(ti, tj, CHUNK)
    masks = ((same & (ti >= tj), same & (ti > tj)), (same & (ti <= tj), same & (ti < tj)))
    lw_refs = (lw0_ref, lw1_ref)
    eta_refs = (eta0_ref, eta1_ref)
    kkw = kk_ref[...]
    kaw = ka_ref[...]

    def pre(p, carry):
        rows = pl.ds(pl.multiple_of(p * LANES, LANES), LANES)
        kkr = k_ref[rows, :] * kkw
        kk_s[rows, :] = kkr * lax.rsqrt(_mm_rsel(kkr * kkr, bd) + L2_EPS)
        for d in range(N_DIRS):
            gc_s[d, rows, :] = _mm_sel(masks[d][0].astype(bf16), lw_refs[d][rows, :])
        return carry

    lax.fori_loop(0, t // LANES, pre, 0, unroll=2)

    def prep_chunks(problems, slot, hook):
        pbs = []
        for d, c, j in problems:
            rows = pl.ds(pl.multiple_of(c * CHUNK, CHUNK), CHUNK)
            r = r_ref[rows, :]
            k = k_ref[rows, :]
            lw = lw_refs[d][rows, :]
            eta = eta_refs[d][rows, :]
            kk = kk_s[rows, :]
            kd = k * (1.0 + (eta - 1.0) * kaw)
            bvec = -kk * eta
            gcum = gc_s[d, rows, :]
            gtot = gcum[CHUNK - 1:CHUNK, :] if d == 0 else gcum[0:1, :]
            eneg = jnp.exp(-gcum)
            erem = jnp.exp(gtot - gcum)
            at2 = _stack2(kk * jnp.exp(gcum - lw), m0, m1)
            rt2 = _stack2(r * jnp.exp(gcum), m0, m1)
            kt2 = _stack2(kd * eneg, m0, m1)
            bt2 = _stack2(bvec * eneg, m0, m1)
            pbs.append(dict(d=d, slot=j, at2=at2, rt2=rt2, gtot=gtot,
                            v2b=_stack2(v_ref[rows, :], m0, m1).astype(bf16),
                            bh=_stack2(bvec * erem, m0, m1).astype(bf16),
                            kh=_stack2(kd * erem, m0, m1).astype(bf16),
                            lhs=jnp.concatenate([at2, rt2], axis=0).astype(bf16),
                            rhs=jnp.concatenate([kt2, bt2], axis=0).astype(bf16)))
        ggs = [_nt(pb["lhs"], pb["rhs"]) for pb in pbs]
        hook()
        for pb, gg in zip(pbs, ggs):
            incl, strict = masks[pb["d"]]
            pb["a_ak"] = jnp.where(strict, gg[0:c2, 0:c2], 0.0).astype(bf16)
            pb["a_ab"] = jnp.where(strict, gg[0:c2, c2:2 * c2], 0.0)
            pb["a_rk"] = jnp.where(incl, gg[c2:2 * c2, 0:c2], 0.0).astype(bf16)
            pb["a_rb"] = jnp.where(incl, gg[c2:2 * c2, c2:2 * c2], 0.0).astype(bf16)
        akv = [_mm(pb["a_ak"], pb["v2b"]) for pb in pbs]
        yv = [_mm(pb["a_rk"], pb["v2b"]) for pb in pbs]
        sk = [_tn(pb["v2b"], pb["kh"]) for pb in pbs]
        hook()
        tinvs = _tri_inverse([-pb["a_ab"] for pb in pbs], eye, lvl, hook)
        wus = []
        for pb, tinv, av in zip(pbs, tinvs, akv):
            rhs = jnp.concatenate([pb["at2"], av], axis=1).astype(bf16)
            t1, t2 = _split2(tinv)
            wus.append(_mm(jnp.concatenate([t1, t2], axis=1), jnp.concatenate([rhs, rhs], axis=0)))
        for i, pb in enumerate(pbs):
            d, j = pb["d"], pb["slot"]
            srows = slice(j * c2, (j + 1) * c2)
            wt_s[slot, d, srows, :] = wus[i][:, :LANES].astype(bf16)
            uv_s[slot, d, srows, :] = wus[i][:, LANES:]
            yv_s[slot, d, srows, :] = yv[i]
            arb_s[slot, d, srows, :] = pb["a_rb"]
            rt_s[slot, d, srows, :] = pb["rt2"].astype(bf16)
            bh_s[slot, d, srows, :] = pb["bh"]
            sk_s[slot, d, srows, :] = sk[i]
            eg_s[slot, d, j * 8:(j + 1) * 8, :] = jnp.broadcast_to(jnp.exp(pb["gtot"]), (8, LANES))

    n_hooks = 2 * (len(lvl) - 1) + 2
    n_stages = seg * 2
    dirs = range(N_DIRS)

    def chunk_of(d, sg, j):
        c = sg * seg + j
        return c if d == 0 else n - 1 - c

    def prep_seg(sg, slot, hook):
        prep_chunks([(d, chunk_of(d, sg, j), j) for j in range(seg) for d in dirs], slot, hook)

    def scan_seg(sg, slot, st):
        for j in range(seg):
            srows = slice(j * c2, (j + 1) * c2)
            sb = [st[d].astype(bf16) for d in dirs]
            ws = [_nt(wt_s[slot, d, srows, :], sb[d]) for d in dirs]
            rs = [_nt(rt_s[slot, d, srows, :], sb[d]) for d in dirs]
            yield
            u2b = [(uv_s[slot, d, srows, :] + ws[d]).astype(bf16) for d in dirs]
            au = [_mm(arb_s[slot, d, srows, :], u2b[d]) for d in dirs]
            ub = [_tn(u2b[d], bh_s[slot, d, srows, :]) for d in dirs]
            yield
            for d in dirs:
                rows = pl.ds(pl.multiple_of(chunk_of(d, sg, j) * CHUNK, CHUNK), CHUNK)
                y2 = yv_s[slot, d, srows, :] + rs[d] + au[d]
                st[d] = st[d] * eg_s[slot, d, j * 8:(j + 1) * 8, :][0:1, :] + sk_s[slot, d, srows, :] + ub[d]
                yacc[rows, :] += y2[0:CHUNK, :] + y2[CHUNK:c2, :]

    yacc[...] = jnp.zeros(yacc.shape, f32)
    prep_seg(0, 0, _no_hook)

    def body(sg, states):
        st = list(states)
        gen = scan_seg(sg, sg % 2, st)
        prep_seg(sg + 1, (sg + 1) % 2, _stage_hook(gen, n_stages, n_hooks))
        for _ in gen:
            pass
        return tuple(st)

    s0 = jnp.zeros((LANES, LANES), f32)
    st = list(lax.fori_loop(0, nseg - 1, body, (s0, s0)))
    for _ in scan_seg(nseg - 1, (nseg - 1) % 2, st):
        pass

    tb = min(512, t)
    inv_hd = 1.0 / RWKV_HD
    for ci in range(t // tb):
        rows = slice(ci * tb, (ci + 1) * tb)
        y = yacc[rows, :]
        mean = _mm_rsel(y, bd) * inv_hd
        dlt = y - mean
        var = _mm_rsel(dlt * dlt, bd) * inv_hd
        yn = dlt * lax.rsqrt(var + RWKV_GN_EPS) * lnw_ref[...] + lnb_ref[...]
        k = k_ref[rows, :]
        kd_sum = k * (2.0 + (eta0_ref[rows, :] + eta1_ref[rows, :] - 2.0) * kaw)
        bonus = _mm_rsel(r_ref[rows, :] * kd_sum * rk_ref[...], bd) * v_ref[rows, :]
        o_ref[rows, :] = ((yn + bonus) * gate_ref[rows, :]).astype(o_ref.dtype)


def rwkv_branch(rkv, gate, lw, eta, k_k, k_a, r_k, ln_w, ln_b, seg=4):
    b, t, _ = gate.shape
    n = t // CHUNK
    seg = min(seg, n // 2)
    c2 = 2 * CHUNK
    blk = lambda off: pl.BlockSpec((None, t, LANES), lambda bi, p: (bi, 0, off + p))
    dblk = lambda d: pl.BlockSpec((None, None, t, LANES), lambda bi, p: (d, bi, 0, p))
    vec = pl.BlockSpec((1, LANES), lambda bi, p: (0, p))
    row = lambda x: x.reshape(1, RWKV_WIDTH)
    seg_f32 = pltpu.VMEM((2, N_DIRS, seg * c2, LANES), f32)
    seg_b16 = pltpu.VMEM((2, N_DIRS, seg * c2, LANES), bf16)
    return pl.pallas_call(
        functools.partial(_rwkv_kernel, t=t, seg=seg),
        out_shape=jax.ShapeDtypeStruct((b, t, RWKV_WIDTH), bf16),
        grid=(b, RWKV_PAIRS),
        in_specs=[blk(0), blk(RWKV_PAIRS), blk(2 * RWKV_PAIRS), blk(0),
                  dblk(0), dblk(1), dblk(0), dblk(1), vec, vec, vec, vec, vec],
        out_specs=blk(0),
        scratch_shapes=[seg_b16,
                        seg_f32,
                        seg_f32,
                        seg_b16,
                        seg_b16,
                        seg_b16,
                        seg_f32,
                        pltpu.VMEM((2, N_DIRS, seg * 8, LANES), f32),
                        pltpu.VMEM((t, LANES), f32),
                        pltpu.VMEM((t, LANES), f32),
                        pltpu.VMEM((N_DIRS, t, LANES), f32)],
        compiler_params=_cparams(2),
        name="rwkv_scan",
    )(rkv, rkv, rkv, gate, lw, lw, eta, eta, row(k_k), row(k_a), row(r_k), row(ln_w), row(ln_b))


def _tail_b_weights(w_in_t, layer):
    last = w_in_t[layer, MAIN_COLS + TAIL_A_COLS:, :]
    assert last.shape[0] == TAIL_B_USED
    return jnp.pad(last, ((0, TAIL_TN - TAIL_B_USED), (0, 0)))[None]


def kernel(x, norm_mix, w_in, w_out, na_rpb, na_gain, gdn_conv_w, gdn_a_log, gdn_dt_bias, gdn_gain, rwkv_mu, rwkv_w0, rwkv_w_up, rwkv_a0, rwkv_a_up, rwkv_g_up, rwkv_k_k, rwkv_k_a, rwkv_r_k, rwkv_ln_w, rwkv_ln_b, norm_ffn, ffn_w_gate, ffn_w_up, ffn_w_down, moe_router, moe_w_gate, moe_w_up, moe_w_down, norm_final):
    b, t, d = x.shape
    n = b * t
    depth = w_in.shape[0]
    h = x.reshape(n, d)
    w_in_t = jnp.swapaxes(w_in, 1, 2)
    for layer in range(depth):
        u = rmsnorm(h, norm_mix[layer], bf16)
        pm = proj(u, w_in_t, layer, MAIN_COLS, tn=512).reshape(b, t, MAIN_COLS)
        pta = proj(u, w_in_t, layer, TAIL_A_COLS, tn=TAIL_TN,
                   col_block0=MAIN_COLS // TAIL_TN).reshape(b, t, TAIL_A_COLS)
        ptb = proj(u, _tail_b_weights(w_in_t, layer), 0, TAIL_TN, tn=TAIL_TN).reshape(b, t, TAIL_TN)

        y_a = na_branch(pm, na_rpb[layer], na_gain[layer])
        qkv = gdn_prep(pm, gdn_conv_w, layer)
        y_b = gdn_branch(pm, pta, qkv, gdn_a_log[layer], gdn_dt_bias[layer], gdn_gain[layer])
        rkv, gate, lw, eta = rwkv_prep(pta, ptb, rwkv_mu[layer], rwkv_w0[layer], rwkv_w_up[layer],
                                       rwkv_a0[layer], rwkv_a_up[layer], rwkv_g_up[layer])
        y_c = rwkv_branch(rkv, gate, lw, eta, rwkv_k_k[layer], rwkv_k_a[layer], rwkv_r_k[layer],
                          rwkv_ln_w[layer], rwkv_ln_b[layer])
        h = outproj(y_a.reshape(n, -1), y_b.reshape(n, -1), y_c.reshape(n, -1), w_out, layer, h)

        i = layer // 2
        if layer % 2 == 0:
            u = rmsnorm(h, norm_ffn[layer], bf16)
            act = gateup(u, ffn_w_gate, ffn_w_up, i)
            half = ffn_w_down.shape[1] // 2
            assert half % LANES == 0
            for kb in range(2):
                h = down(act, ffn_w_down, i, h, kb, half)
        else:
            h = moe_ffn(h, norm_ffn[layer], moe_router[i], moe_w_gate, moe_w_up, moe_w_down, i)
    return rmsnorm(h, norm_final, f32).reshape(b, t, d)
```

```python
import functools

import jax
import jax.numpy as jnp
import numpy as np
from jax import lax
from jax.experimental import pallas as pl
from jax.experimental.pallas import tpu as pltpu

f32 = jnp.float32
bf16 = jnp.bfloat16
HI = lax.Precision.HIGHEST

LANES = 128
VMEM_LIMIT_BYTES = 56 * 1024 * 1024

D_MODEL = 4096
GRID_W = 64
NA_HEADS, NA_HD = 12, 128
NA_WIDTH = NA_HEADS * NA_HD
NA_WIN_ROWS, NA_WIN_COLS = 8, 16
GDN_HEADS, GDN_HD = 12, 128
GDN_WIDTH = GDN_HEADS * GDN_HD
GDN_CONV = 5
CHUNK = 64
RWKV_HEADS, RWKV_HD = 16, 64
RWKV_WIDTH = RWKV_HEADS * RWKV_HD
RWKV_PAIRS = RWKV_HEADS // 2
N_DIRS = 2
MAIN_COLS = 3 * NA_WIDTH + 4 * GDN_WIDTH
RWKV_COLS = 3 * RWKV_WIDTH + 64 + 64 + 128
SHIFT_COLS = RWKV_COLS
assert SHIFT_COLS % LANES == 0
GATE_COLS = 2 * N_DIRS * GDN_HEADS
TAIL_TN = 512
TAIL_A_COLS = 3072
TAIL_B_USED = GATE_COLS + RWKV_COLS - TAIL_A_COLS
assert MAIN_COLS % TAIL_TN == 0 and TAIL_A_COLS % TAIL_TN == 0 and 0 < TAIL_B_USED <= TAIL_TN
MOE_EXPERTS = 8
RMS_EPS = 1e-6
L2_EPS = 1e-6
RWKV_GN_EPS = 64e-5
NEG = -1e30


def _cparams(n_axes):
    return pltpu.CompilerParams(dimension_semantics=("arbitrary",) * n_axes,
                                vmem_limit_bytes=VMEM_LIMIT_BYTES)


def _nt(a, b, precision=None):
    return lax.dot_general(a, b, (((1,), (1,)), ((), ())), precision=precision,
                           preferred_element_type=f32)


def _tn(a, b, precision=None):
    return lax.dot_general(a, b, (((0,), (0,)), ((), ())), precision=precision,
                           preferred_element_type=f32)


def _mm(a, b, precision=None):
    return jnp.dot(a, b, precision=precision, preferred_element_type=f32)


def _sigmoid(x):
    return 1.0 / (1.0 + jnp.exp(-x))


def _softplus(x):
    return jnp.maximum(x, 0.0) + jnp.log(1.0 + jnp.exp(-jnp.abs(x)))


def _iota2(shape, axis):
    return lax.broadcasted_iota(jnp.int32, shape, axis)


def _rmsnorm_kernel(x_ref, g_ref, o_ref):
    x = x_ref[...]
    ms = jnp.mean(x * x, axis=-1, keepdims=True)
    o_ref[...] = (x * lax.rsqrt(ms + RMS_EPS) * g_ref[...]).astype(o_ref.dtype)


def rmsnorm(x, gain, out_dtype):
    n, d = x.shape
    tm = min(256, n)
    return pl.pallas_call(
        _rmsnorm_kernel,
        out_shape=jax.ShapeDtypeStruct((n, d), out_dtype),
        grid=(n // tm,),
        in_specs=[pl.BlockSpec((tm, d), lambda i: (i, 0)), pl.BlockSpec((1, d), lambda i: (0, 0))],
        out_specs=pl.BlockSpec((tm, d), lambda i: (i, 0)),
        compiler_params=_cparams(1),
        name="rmsnorm",
    )(x, gain.reshape(1, d))


def _rmsnorm_router_kernel(x_ref, g_ref, r_ref, o_ref, e_ref, w_ref):
    x = x_ref[...]
    ms = jnp.mean(x * x, axis=-1, keepdims=True)
    u = x * lax.rsqrt(ms + RMS_EPS) * g_ref[...]
    o_ref[...] = u
    logits = _mm(u, r_ref[...], HI)
    lane = _iota2(logits.shape, 1)
    lg = jnp.where(lane < MOE_EXPERTS, logits, NEG)
    m1 = jnp.max(lg, axis=-1, keepdims=True)
    i1 = jnp.min(jnp.where(lg == m1, lane, LANES), axis=-1, keepdims=True)
    lg2 = jnp.where(lane == i1, NEG, lg)
    m2 = jnp.max(lg2, axis=-1, keepdims=True)
    i2 = jnp.min(jnp.where(lg2 == m2, lane, LANES), axis=-1, keepdims=True)
    e2 = jnp.exp(m2 - m1)
    w1 = 1.0 / (1.0 + e2)
    w2 = e2 / (1.0 + e2)
    e_ref[...] = jnp.where(lane == 0, i1, jnp.where(lane == 1, i2, 0))
    w_ref[...] = jnp.where(lane == 0, w1, jnp.where(lane == 1, w2, 0.0))


def rmsnorm_router(x, gain, router):
    n, d = x.shape
    tm = min(256, n)
    r_pad = jnp.pad(router, ((0, 0), (0, LANES - router.shape[1])))
    return pl.pallas_call(
        _rmsnorm_router_kernel,
        out_shape=[jax.ShapeDtypeStruct((n, d), f32), jax.ShapeDtypeStruct((n, LANES), jnp.int32),
                   jax.ShapeDtypeStruct((n, LANES), f32)],
        grid=(n // tm,),
        in_specs=[pl.BlockSpec((tm, d), lambda i: (i, 0)), pl.BlockSpec((1, d), lambda i: (0, 0)),
                  pl.BlockSpec((d, LANES), lambda i: (0, 0))],
        out_specs=[pl.BlockSpec((tm, d), lambda i: (i, 0)), pl.BlockSpec((tm, LANES), lambda i: (i, 0)),
                   pl.BlockSpec((tm, LANES), lambda i: (i, 0))],
        compiler_params=_cparams(1),
        name="rmsnorm_router",
    )(x, gain.reshape(1, d), r_pad)


def _moe_rank_kernel(e_ref, rank_ref, cnt_ref, carry, *, tb):
    @pl.when(pl.program_id(0) == 0)
    def _():
        carry[...] = jnp.zeros(carry.shape, f32)

    ids = e_ref[...]
    lane = _iota2(ids.shape, 1)
    oh1 = lane == ids[:, 0:1]
    oh2 = lane == ids[:, 1:2]
    f1 = jnp.where(oh1, 1.0, 0.0)
    f2 = jnp.where(oh2, 1.0, 0.0)
    earlier = jnp.where(_iota2((tb, tb), 0) > _iota2((tb, tb), 1), 1.0, 0.0).astype(bf16)
    p1 = _mm(earlier, f1.astype(bf16))
    p2 = _mm(earlier, f2.astype(bf16))
    c1 = jnp.sum(f1, axis=0, keepdims=True)
    c2 = jnp.sum(f2, axis=0, keepdims=True)
    base = carry[0:1, :]
    r1 = jnp.sum(jnp.where(oh1, p1 + base, 0.0), axis=-1, keepdims=True)
    r2 = jnp.sum(jnp.where(oh2, p2 + (base + c1), 0.0), axis=-1, keepdims=True)
    rank_ref[...] = jnp.where(lane == 0, r1, jnp.where(lane == 1, r2, 0.0))
    carry[...] = jnp.broadcast_to(base + c1 + c2, carry.shape)
    cnt_ref[...] = carry[...]


def moe_rank(ids, tb=256):
    n = ids.shape[0]
    tb = min(tb, n)
    return pl.pallas_call(
        functools.partial(_moe_rank_kernel, tb=tb),
        out_shape=[jax.ShapeDtypeStruct((n, LANES), f32), jax.ShapeDtypeStruct((8, LANES), f32)],
        grid=(n // tb,),
        in_specs=[pl.BlockSpec((tb, LANES), lambda i: (i, 0))],
        out_specs=[pl.BlockSpec((tb, LANES), lambda i: (i, 0)), pl.BlockSpec((8, LANES), lambda i: (0, 0))],
        scratch_shapes=[pltpu.VMEM((8, LANES), f32)],
        compiler_params=_cparams(1),
        name="moe_rank",
    )(ids)


def _gather_rows_kernel(idx_ref, src_ref, o_ref, buf, sem, *, tm):
    base = pl.program_id(0) * tm

    def row_copy(r, row):
        return pltpu.make_async_copy(src_ref.at[pl.ds(row, 1), :], buf.at[pl.ds(r, 1), :], sem)

    def start(p, c):
        for u in range(2):
            r = 2 * p + u
            row_copy(r, idx_ref[base + r]).start(priority=u)
        return c

    def wait(r, c):
        row_copy(r, 0).wait()
        return c

    lax.fori_loop(0, tm // 2, start, 0, unroll=4)
    lax.fori_loop(0, tm, wait, 0, unroll=8)
    o_ref[...] = buf[...].astype(o_ref.dtype)


def gather_rows(src, idx, out_dtype, tm=256):
    m = idx.shape[0]
    d = src.shape[1]
    tm = min(tm, m)
    assert m % tm == 0
    return pl.pallas_call(
        functools.partial(_gather_rows_kernel, tm=tm),
        out_shape=jax.ShapeDtypeStruct((m, d), out_dtype),
        grid_spec=pltpu.PrefetchScalarGridSpec(
            num_scalar_prefetch=1, grid=(m // tm,),
            in_specs=[pl.BlockSpec(memory_space=pl.ANY)],
            out_specs=pl.BlockSpec((tm, d), lambda i, idx_ref: (i, 0)),
            scratch_shapes=[pltpu.VMEM((tm, d), f32), pltpu.SemaphoreType.DMA(())]),
        compiler_params=_cparams(1),
        name="gather_rows",
    )(idx, src)


def _moe_combine_kernel(d1_ref, d2_ref, y_ref, h_ref, w_ref, o_ref, buf, sems, *, tm):
    base = pl.program_id(0) * tm
    dests = (d1_ref, d2_ref)

    def row_copy(k, r, row):
        return pltpu.make_async_copy(y_ref.at[pl.ds(row, 1), :], buf.at[k, pl.ds(r, 1), :], sems.at[k])

    def start(r, c):
        for k in range(2):
            row_copy(k, r, dests[k][base + r]).start(priority=k)
        return c

    def wait(r, c):
        for k in range(2):
            row_copy(k, r, 0).wait()
        return c

    lax.fori_loop(0, tm, start, 0, unroll=8)
    lax.fori_loop(0, tm, wait, 0, unroll=8)
    w = w_ref[...]
    o_ref[...] = h_ref[...] + w[:, 0:1] * buf[0] + w[:, 1:2] * buf[1]


def moe_combine(y, d1, d2, h, wts, tm=256):
    n, d = h.shape
    tm = min(tm, n)
    return pl.pallas_call(
        functools.partial(_moe_combine_kernel, tm=tm),
        out_shape=jax.ShapeDtypeStruct((n, d), f32),
        grid_spec=pltpu.PrefetchScalarGridSpec(
            num_scalar_prefetch=2, grid=(n // tm,),
            in_specs=[pl.BlockSpec(memory_space=pl.ANY),
                      pl.BlockSpec((tm, d), lambda i, a, b: (i, 0)),
                      pl.BlockSpec((tm, LANES), lambda i, a, b: (i, 0))],
            out_specs=pl.BlockSpec((tm, d), lambda i, a, b: (i, 0)),
            scratch_shapes=[pltpu.VMEM((2, tm, d), f32), pltpu.SemaphoreType.DMA((2,))]),
        compiler_params=_cparams(1),
        name="moe_combine",
    )(d1, d2, y, h, wts)


def _proj_t_kernel(x_ref, w_ref, o_ref, wb_ref):
    @pl.when(pl.program_id(1) == 0)
    def _():
        wb_ref[...] = w_ref[...].T.astype(bf16)

    o_ref[...] = _mm(x_ref[...], wb_ref[...]).astype(o_ref.dtype)


def proj(x, wt, w_index, n_cols, tn, out_dtype=f32, tm=1024, col_block0=0):
    n, k = x.shape
    tm = min(tm, n)
    assert n_cols % tn == 0 and n % tm == 0 and (col_block0 * tn + n_cols) <= wt.shape[1]
    return pl.pallas_call(
        _proj_t_kernel,
        out_shape=jax.ShapeDtypeStruct((n, n_cols), out_dtype),
        grid=(n_cols // tn, n // tm),
        in_specs=[pl.BlockSpec((tm, k), lambda j, i: (i, 0)),
                  pl.BlockSpec((None, tn, k), lambda j, i: (w_index, col_block0 + j, 0))],
        out_specs=pl.BlockSpec((tm, tn), lambda j, i: (i, j)),
        scratch_shapes=[pltpu.VMEM((k, tn), bf16)],
        compiler_params=_cparams(2),
        name="proj",
    )(x, wt)


def _outproj_kernel(xa_ref, xb_ref, xc_ref, w_ref, res_ref, o_ref, wb_ref):
    @pl.when(pl.program_id(1) == 0)
    def _():
        wb_ref[...] = w_ref[...].astype(bf16)

    acc = _mm(xa_ref[...], wb_ref[0:NA_WIDTH, :])
    acc += _mm(xb_ref[...], wb_ref[NA_WIDTH:NA_WIDTH + GDN_WIDTH, :])
    acc += _mm(xc_ref[...], wb_ref[NA_WIDTH + GDN_WIDTH:, :])
    o_ref[...] = res_ref[...] + acc


def outproj(ya, yb, yc, w, layer, res, tn=512, tm=1024):
    n = ya.shape[0]
    d = w.shape[-1]
    tm = min(tm, n)
    return pl.pallas_call(
        _outproj_kernel,
        out_shape=jax.ShapeDtypeStruct((n, d), f32),
        grid=(d // tn, n // tm),
        in_specs=[pl.BlockSpec((tm, NA_WIDTH), lambda j, i: (i, 0)),
                  pl.BlockSpec((tm, GDN_WIDTH), lambda j, i: (i, 0)),
                  pl.BlockSpec((tm, RWKV_WIDTH), lambda j, i: (i, 0)),
                  pl.BlockSpec((None, D_MODEL, tn), lambda j, i: (layer, 0, j)),
                  pl.BlockSpec((tm, tn), lambda j, i: (i, j))],
        out_specs=pl.BlockSpec((tm, tn), lambda j, i: (i, j)),
        scratch_shapes=[pltpu.VMEM((D_MODEL, tn), bf16)],
        compiler_params=_cparams(2),
        name="outproj",
    )(ya, yb, yc, w, res)


def _gateup_kernel(x_ref, wg_ref, wu_ref, o_ref, wgb_ref, wub_ref):
    @pl.when(pl.program_id(1) == 0)
    def _():
        wgb_ref[...] = wg_ref[...].astype(bf16)
        wub_ref[...] = wu_ref[...].astype(bf16)

    x = x_ref[...]
    g = _mm(x, wgb_ref[...])
    u = _mm(x, wub_ref[...])
    o_ref[...] = (g * _sigmoid(g) * u).astype(o_ref.dtype)


def gateup(x, w_gate, w_up, w_index, tn=256, tm=1024):
    n, k = x.shape
    ff = w_gate.shape[-1]
    tm = min(tm, n)
    return pl.pallas_call(
        _gateup_kernel,
        out_shape=jax.ShapeDtypeStruct((n, ff), bf16),
        grid=(ff // tn, n // tm),
        in_specs=[pl.BlockSpec((tm, k), lambda j, i: (i, 0)),
                  pl.BlockSpec((None, k, tn), lambda j, i: (w_index, 0, j)),
                  pl.BlockSpec((None, k, tn), lambda j, i: (w_index, 0, j))],
        out_specs=pl.BlockSpec((tm, tn), lambda j, i: (i, j)),
        scratch_shapes=[pltpu.VMEM((k, tn), bf16), pltpu.VMEM((k, tn), bf16)],
        compiler_params=_cparams(2),
        name="gateup",
    )(x, w_gate, w_up)


def _tile_changed(te_ref, i):
    return (i == 0) | (te_ref[i] != te_ref[jnp.maximum(i - 1, 0)])


def _moe_gateup_kernel(te_ref, nu_ref, x_ref, wg_ref, wu_ref, o_ref, wgb_ref, wub_ref):
    i = pl.program_id(1)
    used = i < nu_ref[0]

    @pl.when(used & _tile_changed(te_ref, i))
    def _():
        wgb_ref[...] = wg_ref[...].astype(bf16)
        wub_ref[...] = wu_ref[...].astype(bf16)

    @pl.when(used)
    def _():
        x = x_ref[...]
        g = _mm(x, wgb_ref[...])
        u = _mm(x, wub_ref[...])
        o_ref[...] = (g * _sigmoid(g) * u).astype(o_ref.dtype)

    @pl.when(jnp.logical_not(used))
    def _():
        o_ref[...] = jnp.zeros(o_ref.shape, o_ref.dtype)


def _moe_down_kernel(te_ref, nu_ref, x_ref, w_ref, o_ref, wb_ref):
    i = pl.program_id(1)
    used = i < nu_ref[0]

    @pl.when(used & _tile_changed(te_ref, i))
    def _():
        wb_ref[...] = w_ref[...].astype(bf16)

    @pl.when(used)
    def _():
        o_ref[...] = _mm(x_ref[...], wb_ref[...])

    @pl.when(jnp.logical_not(used))
    def _():
        o_ref[...] = jnp.zeros(o_ref.shape, o_ref.dtype)


def _grouped_specs(tm, k, tn, w_index, n_w):
    last = lambda i, nu: jnp.minimum(i, nu[0] - 1)
    x_spec = pl.BlockSpec((tm, k), lambda j, i, te, nu: (last(i, nu), 0))
    w_spec = pl.BlockSpec((None, None, k, tn), lambda j, i, te, nu: (w_index, te[last(i, nu)], 0, j))
    o_spec = pl.BlockSpec((tm, tn), lambda j, i, te, nu: (i, j))
    return [x_spec] + [w_spec] * n_w, o_spec


def moe_gateup(x, tile_expert, n_used, w_gate, w_up, w_index, tm, tn=512):
    m, k = x.shape
    ff = w_gate.shape[-1]
    in_specs, out_spec = _grouped_specs(tm, k, tn, w_index, 2)
    return pl.pallas_call(
        _moe_gateup_kernel,
        out_shape=jax.ShapeDtypeStruct((m, ff), bf16),
        grid_spec=pltpu.PrefetchScalarGridSpec(
            num_scalar_prefetch=2, grid=(ff // tn, m // tm), in_specs=in_specs, out_specs=out_spec,
            scratch_shapes=[pltpu.VMEM((k, tn), bf16), pltpu.VMEM((k, tn), bf16)]),
        compiler_params=_cparams(2),
        name="moe_gateup",
    )(tile_expert, n_used, x, w_gate, w_up)


def moe_down(x, tile_expert, n_used, w_down, w_index, tm, tn=512):
    m, k = x.shape
    d = w_down.shape[-1]
    in_specs, out_spec = _grouped_specs(tm, k, tn, w_index, 1)
    return pl.pallas_call(
        _moe_down_kernel,
        out_shape=jax.ShapeDtypeStruct((m, d), f32),
        grid_spec=pltpu.PrefetchScalarGridSpec(
            num_scalar_prefetch=2, grid=(d // tn, m // tm), in_specs=in_specs, out_specs=out_spec,
            scratch_shapes=[pltpu.VMEM((k, tn), bf16)]),
        compiler_params=_cparams(2),
        name="moe_down",
    )(tile_expert, n_used, x, w_down)


def moe_ffn(h, gain, router, w_gate, w_up, w_down, w_index, tm=512):
    n, d = h.shape
    ne = w_gate.shape[1]
    tm = min(tm, n)
    u, ids, wts = rmsnorm_router(h, gain, router)
    rank, counts = moe_rank(ids)
    counts = counts[0, :ne].astype(jnp.int32)
    padded = (counts + tm - 1) // tm * tm
    ends = jnp.cumsum(padded)
    e2 = ids[:, :2]
    dest = (ends - padded)[e2] + rank[:, :2].astype(jnp.int32)
    n_tiles = 2 * n // tm + ne
    tile_expert = jnp.minimum(jnp.searchsorted(ends, jnp.arange(n_tiles) * tm, side="right"), ne - 1)
    n_used = (ends[-1] // tm).reshape(1)
    token = jnp.broadcast_to(jnp.arange(n, dtype=jnp.int32)[:, None], (n, 2))
    src = jnp.zeros((n_tiles * tm,), jnp.int32).at[dest.reshape(-1)].set(token.reshape(-1))
    xs = gather_rows(u, src, bf16)
    act = moe_gateup(xs, tile_expert.astype(jnp.int32), n_used.astype(jnp.int32), w_gate, w_up, w_index, tm)
    ys = moe_down(act, tile_expert.astype(jnp.int32), n_used.astype(jnp.int32), w_down, w_index, tm)
    return moe_combine(ys, dest[:, 0], dest[:, 1], h, wts)


def _down_kernel(x_ref, w_ref, res_ref, o_ref, wb_ref):
    @pl.when(pl.program_id(1) == 0)
    def _():
        wb_ref[...] = w_ref[...].astype(bf16)

    o_ref[...] = res_ref[...] + _mm(x_ref[...], wb_ref[...])


def down(x, w, w_index, res, k_block, tk, tn=512, tm=512):
    n = x.shape[0]
    d = w.shape[-1]
    tm = min(tm, n)
    return pl.pallas_call(
        _down_kernel,
        out_shape=jax.ShapeDtypeStruct((n, d), f32),
        grid=(d // tn, n // tm),
        in_specs=[pl.BlockSpec((tm, tk), lambda j, i: (i, k_block)),
                  pl.BlockSpec((None, tk, tn), lambda j, i: (w_index, k_block, j)),
                  pl.BlockSpec((tm, tn), lambda j, i: (i, j))],
        out_specs=pl.BlockSpec((tm, tn), lambda j, i: (i, j)),
        scratch_shapes=[pltpu.VMEM((tk, tn), bf16)],
        compiler_params=_cparams(2),
        name="down",
    )(x, w, res)


def _na_bias_table(rpb):
    c = np.arange(GRID_W)[:, None]
    kc = np.arange(GRID_W)[None, :]
    cs = np.clip(c - NA_WIN_COLS // 2, 0, GRID_W - NA_WIN_COLS)
    valid = (kc >= cs) & (kc < cs + NA_WIN_COLS)
    dc = kc - c + (NA_WIN_COLS - 1)
    onehot = (valid[None] & (dc[None] == np.arange(2 * NA_WIN_COLS - 1)[:, None, None])).astype(np.float32)
    cols = jnp.einsum("hrm,mck->hrck", rpb.astype(f32), jnp.asarray(onehot), precision=HI)
    lo = NA_WIN_ROWS - 1
    tab = jnp.stack([cols[:, lo - off:lo - off + NA_WIN_ROWS] for off in range(NA_WIN_ROWS)], axis=1)
    tab = tab.transpose(0, 1, 3, 2, 4) + jnp.asarray(np.where(valid, 0.0, NEG).astype(np.float32))[:, None, :]
    return tab.reshape(rpb.shape[0], NA_WIN_ROWS, GRID_W, NA_WIN_ROWS * GRID_W)


def _na_kernel(q_ref, k_ref, v_ref, bias_ref, gain_ref, o_ref, kb_ref, vb_ref, *, rows, rows_per_step, lockstep):
    g = pl.program_id(2)

    @pl.when(g == 0)
    def _():
        kb_ref[...] = k_ref[...].astype(bf16)
        vb_ref[...] = v_ref[...].astype(bf16)

    scale = NA_HD ** -0.5
    win = NA_WIN_ROWS * GRID_W

    def body(it, carry):
        qrows, krows, ss = [], [], []
        for li in range(lockstep):
            i = it * lockstep + li
            r = g * rows_per_step + i
            sr = jnp.clip(r - NA_WIN_ROWS // 2, 0, rows - NA_WIN_ROWS)
            qrows.append(pl.ds(pl.multiple_of(i * GRID_W, GRID_W), GRID_W))
            krows.append(pl.ds(pl.multiple_of(sr * GRID_W, GRID_W), win))
            q = q_ref[qrows[li], :].astype(bf16)
            ss.append(_nt(q, kb_ref[krows[li], :]) * scale + bias_ref[r - sr])
        ps, ls = [], []
        for s in ss:
            p = jnp.exp(s - jnp.max(s, axis=-1, keepdims=True))
            ls.append(jnp.sum(p, axis=-1, keepdims=True))
            ps.append(p)
        ovs = [_mm((p / l).astype(bf16), vb_ref[kr, :]) for p, l, kr in zip(ps, ls, krows)]
        for ov, qr in zip(ovs, qrows):
            ms = jnp.mean(ov * ov, axis=-1, keepdims=True)
            o_ref[qr, :] = (ov * lax.rsqrt(ms + RMS_EPS) * gain_ref[...]).astype(o_ref.dtype)
        return carry

    lax.fori_loop(0, rows_per_step // lockstep, body, 0)


def na_branch(pm, rpb, gain, rows_per_step=8):
    b, t, _ = pm.shape
    rows = t // GRID_W
    assert rows >= NA_WIN_ROWS and rows % rows_per_step == 0
    tq = rows_per_step * GRID_W
    bias = _na_bias_table(rpb)
    return pl.pallas_call(
        functools.partial(_na_kernel, rows=rows, rows_per_step=rows_per_step, lockstep=8),
        out_shape=jax.ShapeDtypeStruct((b, t, NA_WIDTH), bf16),
        grid=(b, NA_HEADS, rows // rows_per_step),
        in_specs=[pl.BlockSpec((None, tq, NA_HD), lambda bi, h, g: (bi, g, h)),
                  pl.BlockSpec((None, t, NA_HD), lambda bi, h, g: (bi, 0, NA_HEADS + h)),
                  pl.BlockSpec((None, t, NA_HD), lambda bi, h, g: (bi, 0, 2 * NA_HEADS + h)),
                  pl.BlockSpec((None, NA_WIN_ROWS, GRID_W, NA_WIN_ROWS * GRID_W), lambda bi, h, g: (h, 0, 0, 0)),
                  pl.BlockSpec((None, 1, NA_HD), lambda bi, h, g: (h, 0, 0))],
        out_specs=pl.BlockSpec((None, tq, NA_HD), lambda bi, h, g: (bi, g, h)),
        scratch_shapes=[pltpu.VMEM((t, NA_HD), bf16), pltpu.VMEM((t, NA_HD), bf16)],
        compiler_params=_cparams(3),
        name="na_attention",
    )(pm, pm, pm, bias, gain.reshape(NA_HEADS, 1, NA_HD))


def _gdn_prep_kernel(x_ref, w_ref, o_ref, xp_ref, *, t, tc):
    c = pl.program_id(1)
    pad = 8
    xp_ref[0:pad, :] = jnp.zeros((pad, LANES), f32)
    xp_ref[t + pad:t + 2 * pad, :] = jnp.zeros((pad, LANES), f32)
    xp_ref[pad:t + pad, :] = x_ref[...]
    half = GDN_CONV // 2
    for ci in range(t // tc):
        acc = jnp.zeros((tc, LANES), f32)
        for j in range(GDN_CONV):
            s0 = ci * tc + pad - half + j
            acc = acc + xp_ref[s0:s0 + tc, :] * w_ref[j:j + 1, :]
        y = acc * _sigmoid(acc)
        inv = lax.rsqrt(jnp.sum(y * y, axis=-1, keepdims=True) + L2_EPS)
        fac = jnp.where(c < GDN_HEADS, inv * GDN_HD ** -0.5, jnp.where(c < 2 * GDN_HEADS, inv, 1.0))
        o_ref[ci * tc:(ci + 1) * tc, :] = y * fac


def gdn_prep(pm, conv_w, layer):
    b, t, _ = pm.shape
    tc = min(512, t)
    col0 = 3 * NA_WIDTH // LANES
    nblk = 3 * GDN_WIDTH // LANES
    return pl.pallas_call(
        functools.partial(_gdn_prep_kernel, t=t, tc=tc),
        out_shape=jax.ShapeDtypeStruct((b, t, 3 * GDN_WIDTH), f32),
        grid=(b, nblk),
        in_specs=[pl.BlockSpec((None, t, LANES), lambda bi, c: (bi, 0, col0 + c)),
                  pl.BlockSpec((None, GDN_CONV, LANES), lambda bi, c: (layer, 0, c))],
        out_specs=pl.BlockSpec((None, t, LANES), lambda bi, c: (bi, 0, c)),
        scratch_shapes=[pltpu.VMEM((t + 16, LANES), f32)],
        compiler_params=_cparams(2),
        name="gdn_prep",
    )(pm, conv_w)


def _level_masks(ii, jj, size):
    masks = []
    s = 1
    while s < size:
        masks.append(((ii // (2 * s)) == (jj // (2 * s))) & ((ii // s) != (jj // s)))
        s *= 2
    return masks


def _split2(x):
    hi = x.astype(bf16)
    return hi, (x - hi.astype(f32)).astype(bf16)


def _split3(x):
    x1 = x.astype(bf16)
    r = x - x1.astype(f32)
    x2 = r.astype(bf16)
    return x1, x2, (r - x2.astype(f32)).astype(bf16)


def _mm_f32(a, b):
    a1, a2 = _split2(a)
    b1, b2 = _split2(b)
    return _mm(jnp.concatenate([a1, a2, a1], axis=1), jnp.concatenate([b1, b1, b2], axis=0))


def _mm_hl(a, b):
    a1, a2 = _split2(a)
    b1 = b.astype(bf16)
    return _mm(jnp.concatenate([a1, a2], axis=1), jnp.concatenate([b1, b1], axis=0))


def _mm_sel(sel, x):
    x1, x2, x3 = _split3(x)
    return _mm(jnp.concatenate([sel, sel, sel], axis=1), jnp.concatenate([x1, x2, x3], axis=0))


def _mm_rsel(x, sel):
    x1, x2, x3 = _split3(x)
    return _mm(jnp.concatenate([x1, x2, x3], axis=1), jnp.concatenate([sel, sel, sel], axis=0))


def _nt_sel(sel, x):
    x1, x2, x3 = _split3(x)
    return _nt(jnp.concatenate([sel, sel, sel], axis=1), jnp.concatenate([x1, x2, x3], axis=1))


def _tri_inverse(ms, eye, masks, hook):
    xs = [eye - jnp.where(masks[0], m, 0.0) for m in ms]
    for mask in masks[1:]:
        xcs = [_mm_hl(x, jnp.where(mask, m, 0.0)) for x, m in zip(xs, ms)]
        hook()
        xs = [x - _mm_hl(xc, x) for x, xc in zip(xs, xcs)]
        hook()
    return xs


def _stage_hook(gen, n_stages, n_hooks):
    state = {"calls": 0, "done": 0}

    def hook():
        state["calls"] += 1
        target = -(-state["calls"] * n_stages // n_hooks)
        while state["done"] < min(target, n_stages):
            next(gen, None)
            state["done"] += 1

    return hook


def _no_hook():
    return None


def _pair_masks():
    ii = _iota2((LANES, LANES), 0)
    jj = _iota2((LANES, LANES), 1)
    same = (ii // CHUNK) == (jj // CHUNK)
    ti, tj = ii % CHUNK, jj % CHUNK
    return ii, jj, same, ti, tj


def _gdn_gates_kernel(ab_ref, alog_ref, dtb_ref, gt_ref, grow_ref, *, tb):
    ii, jj, same, ti, tj = _pair_masks()
    tri = ((same & (ti >= tj)).astype(bf16), (same & (ti <= tj)).astype(bf16))
    eye = (ii == jj).astype(bf16)
    for p in range(tb // LANES):
        rows = slice(p * LANES, (p + 1) * LANES)
        ab = ab_ref[rows, :]
        g = -jnp.exp(alog_ref[...]) * _softplus(ab + dtb_ref[...])
        fwd = _mm_sel(tri[0], g)
        bwd = _mm_sel(tri[1], g)
        gt = jnp.where(jj < GDN_HEADS, fwd, jnp.where(jj < N_DIRS * GDN_HEADS, bwd, _sigmoid(ab)))
        gt_ref[rows, :] = gt
        grow_ref[p] = _nt_sel(eye, gt)


def gdn_gates(pt, a_log, dt_bias, tb=512):
    b, t, _ = pt.shape
    tb = min(tb, t)
    npad = LANES - N_DIRS * GDN_HEADS
    alog = jnp.pad(a_log.reshape(1, -1), ((0, 0), (0, npad)))
    dtb = jnp.pad(dt_bias.reshape(1, -1), ((0, 0), (0, npad)))
    vec = pl.BlockSpec((1, LANES), lambda bi, i: (0, 0))
    return pl.pallas_call(
        functools.partial(_gdn_gates_kernel, tb=tb),
        out_shape=[jax.ShapeDtypeStruct((b, t, LANES), f32),
                   jax.ShapeDtypeStruct((b, t // LANES, LANES, LANES), f32)],
        grid=(b, t // tb),
        in_specs=[pl.BlockSpec((None, tb, LANES), lambda bi, i: (bi, i, 0)), vec, vec],
        out_specs=[pl.BlockSpec((None, tb, LANES), lambda bi, i: (bi, i, 0)),
                   pl.BlockSpec((None, tb // LANES, LANES, LANES), lambda bi, i: (bi, i, 0, 0))],
        compiler_params=_cparams(2),
        name="gdn_gates",
    )(pt, alog, dtb)


def _gdn_kernel(q_ref, k_ref, v_ref, z_ref, gt_ref, grow_ref, gain_ref, o_ref,
                u_s, w_s, qg_s, kg_s, a_s, el_s, oacc, *, t, seg):
    h = pl.program_id(1)
    npairs = t // LANES
    nseg = npairs // seg
    assert npairs % seg == 0 and npairs % 2 == 0
    ii, jj, same, ti, tj = _pair_masks()
    eye = (ii == jj).astype(f32)
    lvl = _level_masks(ti, tj, CHUNK)
    masks = ((same & (ti >= tj), same & (ti > tj)), (same & (ti <= tj), same & (ti < tj)))
    dirs = range(N_DIRS)
    n_hooks = 2 * (len(lvl) - 1) + 1
    n_stages = seg * 2 * 2

    def tile_of(d, sg, j):
        p = sg * seg + j
        return p if d == 0 else npairs - 1 - p

    def prep_seg(sg, slot, hook):
        probs = []
        for j in range(seg):
            for d in dirs:
                p = tile_of(d, sg, j)
                rows = pl.ds(pl.multiple_of(p * LANES, LANES), LANES)
                k = k_ref[rows, :]
                probs.append(dict(d=d, j=j, p=p, q=q_ref[rows, :], k=k, v=v_ref[rows, :], gt=gt_ref[rows, :],
                                  kb16=k.astype(bf16)))
        for pb in probs:
            pb["qk"] = _nt(pb["q"].astype(bf16), pb["kb16"])
            pb["kk"] = _nt(pb["kb16"], pb["kb16"])
        hook()
        for pb in probs:
            d = pb["d"]
            la = d * GDN_HEADS + h
            lb = N_DIRS * GDN_HEADS + la
            incl, strict = masks[d]
            gcol = jnp.sum(jnp.where(jj == la, pb["gt"], 0.0), axis=-1, keepdims=True)
            bcol = jnp.sum(jnp.where(jj == lb, pb["gt"], 0.0), axis=-1, keepdims=True)
            grow = grow_ref[pb["p"], pl.ds(la, 1), :]
            pb["dec"] = jnp.exp(jnp.where(incl, gcol - grow, NEG))
            pb["gcol"], pb["bcol"] = gcol, bcol
            pb["m"] = jnp.where(strict, pb["kk"] * bcol * pb["dec"], 0.0)
        tinvs = _tri_inverse([pb["m"] for pb in probs], eye, lvl, hook)
        uws = []
        for pb, tinv in zip(probs, tinvs):
            gcol, bcol = pb["gcol"], pb["bcol"]
            pb["eg"] = jnp.exp(gcol)
            rhs = jnp.concatenate([pb["v"] * bcol, pb["k"] * (bcol * pb["eg"])], axis=1).astype(bf16)
            t1, t2 = _split2(tinv)
            uws.append(_mm(jnp.concatenate([t1, t2], axis=1), jnp.concatenate([rhs, rhs], axis=0)))
        for pb, uw in zip(probs, uws):
            d, gcol = pb["d"], pb["gcol"]
            srows = slice(pb["j"] * LANES, (pb["j"] + 1) * LANES)
            ends = (CHUNK - 1, 2 * CHUNK - 1) if d == 0 else (0, CHUNK)
            gl = [gcol[e:e + 1, :] for e in ends]
            gtot = jnp.concatenate([jnp.broadcast_to(g, (CHUNK, 1)) for g in gl], axis=0)
            a2 = jnp.where(masks[d][0], pb["qk"] * pb["dec"], 0.0)
            u_s[slot, d, srows, :] = uw[:, :GDN_HD]
            w_s[slot, d, srows, :] = uw[:, GDN_HD:].astype(bf16)
            a_s[slot, d, srows, :] = jnp.concatenate([a2[:CHUNK, :CHUNK], a2[CHUNK:, CHUNK:]], axis=0).astype(bf16)
            qg_s[slot, d, srows, :] = (pb["q"] * pb["eg"]).astype(bf16)
            kg_s[slot, d, srows, :] = (pb["k"] * jnp.exp(gtot - gcol)).astype(bf16)
            el_s[slot, d, pb["j"] * 16:(pb["j"] + 1) * 16, :] = jnp.concatenate(
                [jnp.broadcast_to(jnp.exp(g), (8, LANES)) for g in gl], axis=0)

    def scan_seg(sg, slot, st):
        halves = ((0, 1), (1, 0))
        for j in range(seg):
            for step in range(2):
                sl = [slice(j * LANES + halves[d][step] * CHUNK, j * LANES + (halves[d][step] + 1) * CHUNK)
                      for d in dirs]
                sb = [st[d].astype(bf16) for d in dirs]
                ws = [_mm(w_s[slot, d, sl[d], :], sb[d]) for d in dirs]
                qs = [_mm(qg_s[slot, d, sl[d], :], sb[d]) for d in dirs]
                yield
                vnb = [(u_s[slot, d, sl[d], :] - ws[d]).astype(bf16) for d in dirs]
                av = [_mm(a_s[slot, d, sl[d], :], vnb[d]) for d in dirs]
                kv = [_tn(kg_s[slot, d, sl[d], :], vnb[d]) for d in dirs]
                yield
                for d in dirs:
                    e0 = j * 16 + halves[d][step] * 8
                    st[d] = st[d] * el_s[slot, d, e0:e0 + 8, :][0:1, :] + kv[d]
                    start = tile_of(d, sg, j) * LANES + halves[d][step] * CHUNK
                    oacc[pl.ds(pl.multiple_of(start, CHUNK), CHUNK), :] += qs[d] + av[d]

    oacc[...] = jnp.zeros(oacc.shape, f32)
    prep_seg(0, 0, _no_hook)

    def body(sg, states):
        st = list(states)
        gen = scan_seg(sg, sg % 2, st)
        prep_seg(sg + 1, (sg + 1) % 2, _stage_hook(gen, n_stages, n_hooks))
        for _ in gen:
            pass
        return tuple(st)

    s0 = jnp.zeros((GDN_HD, GDN_HD), f32)
    st = list(lax.fori_loop(0, nseg - 1, body, (s0, s0)))
    for _ in scan_seg(nseg - 1, (nseg - 1) % 2, st):
        pass

    tb = min(512, t)
    for ci in range(t // tb):
        rows = slice(ci * tb, (ci + 1) * tb)
        o = oacc[rows, :]
        z = z_ref[rows, :]
        ms = jnp.mean(o * o, axis=-1, keepdims=True)
        o_ref[rows, :] = (o * lax.rsqrt(ms + RMS_EPS) * gain_ref[...] * (z * _sigmoid(z))).astype(o_ref.dtype)


def gdn_branch(pm, pt, qkv, a_log, dt_bias, gain):
    b, t, _ = pm.shape
    assert t % LANES == 0
    seg = min(4, t // LANES // 2)
    rows = seg * LANES
    zcol0 = (3 * NA_WIDTH + 3 * GDN_WIDTH) // LANES
    gt, grow = gdn_gates(pt, a_log, dt_bias)
    hd = lambda off: (lambda bi, h: (bi, 0, off + h))
    return pl.pallas_call(
        functools.partial(_gdn_kernel, t=t, seg=seg),
        out_shape=jax.ShapeDtypeStruct((b, t, GDN_WIDTH), bf16),
        grid=(b, GDN_HEADS),
        in_specs=[pl.BlockSpec((None, t, GDN_HD), hd(0)),
                  pl.BlockSpec((None, t, GDN_HD), hd(GDN_HEADS)),
                  pl.BlockSpec((None, t, GDN_HD), hd(2 * GDN_HEADS)),
                  pl.BlockSpec((None, t, GDN_HD), hd(zcol0)),
                  pl.BlockSpec((None, t, LANES), lambda bi, h: (bi, 0, 0)),
                  pl.BlockSpec((None, t // LANES, LANES, LANES), lambda bi, h: (bi, 0, 0, 0)),
                  pl.BlockSpec((1, LANES), lambda bi, h: (0, 0))],
        out_specs=pl.BlockSpec((None, t, GDN_HD), hd(0)),
        scratch_shapes=[pltpu.VMEM((2, N_DIRS, rows, GDN_HD), f32),
                        pltpu.VMEM((2, N_DIRS, rows, GDN_HD), bf16),
                        pltpu.VMEM((2, N_DIRS, rows, GDN_HD), bf16),
                        pltpu.VMEM((2, N_DIRS, rows, GDN_HD), bf16),
                        pltpu.VMEM((2, N_DIRS, rows, CHUNK), bf16),
                        pltpu.VMEM((2, N_DIRS, seg * 16, LANES), f32),
                        pltpu.VMEM((t, GDN_HD), f32)],
        compiler_params=_cparams(2),
        name="gdn_scan",
    )(qkv, qkv, qkv, pm, gt, grow, gain.reshape(1, GDN_HD))


def _rwkv_prep_kernel(cur_a, prev_a, next_a, cur_b, prev_b, next_b, mu_ref, wup_ref, aup_ref, gup_ref, w0_ref,
                      a0_ref, rkv_ref, gate_ref, lw_ref, eta_ref, buf, *, tb):
    i = pl.program_id(1)
    nt = pl.num_programs(1)
    na = TAIL_A_COLS - GATE_COLS
    buf[8:8 + tb, 0:na] = cur_a[:, GATE_COLS:]
    buf[8:8 + tb, na:] = cur_b[:, 0:TAIL_B_USED]
    has_prev = i > 0
    has_next = i < nt - 1
    buf[7:8, 0:na] = jnp.where(has_prev, prev_a[7:8, GATE_COLS:], 0.0)
    buf[7:8, na:] = jnp.where(has_prev, prev_b[7:8, 0:TAIL_B_USED], 0.0)
    buf[8 + tb:9 + tb, 0:na] = jnp.where(has_next, next_a[0:1, GATE_COLS:], 0.0)
    buf[8 + tb:9 + tb, na:] = jnp.where(has_next, next_b[0:1, 0:TAIL_B_USED], 0.0)

    def mixed(c0, c1):
        x = buf[8:8 + tb, c0:c1]
        sh = 0.5 * (buf[7:7 + tb, c0:c1] + buf[9:9 + tb, c0:c1])
        return x + (sh - x) * mu_ref[:, c0:c1]

    for gi in range(3):
        c0 = gi * RWKV_WIDTH
        rkv_ref[:, c0:c0 + RWKV_WIDTH] = mixed(c0, c0 + RWKV_WIDTH)
    xwa = mixed(3 * RWKV_WIDTH, 3 * RWKV_WIDTH + LANES)
    xg = mixed(3 * RWKV_WIDTH + LANES, 3 * RWKV_WIDTH + 2 * LANES)
    gate_ref[...] = _mm(_sigmoid(xg).astype(bf16), gup_ref[...].astype(bf16))
    txw = jnp.tanh(xwa)
    for d in range(N_DIRS):
        log_w = -_softplus(-(w0_ref[d] + _mm_f32(txw, wup_ref[d]))) - 0.5
        lw_ref[d] = -jnp.exp(log_w)
        eta_ref[d] = _sigmoid(a0_ref[d] + _mm_f32(xwa, aup_ref[d]))


def rwkv_prep(pta, ptb, mu, w0, w_up, a0, a_up, g_up, tb=256):
    b, t, _ = pta.shape
    tb = min(tb, t)
    nb8 = tb // 8
    mu_p = mu.reshape(1, SHIFT_COLS)
    lora = w_up.shape[1]
    wup_p = jnp.pad(w_up, ((0, 0), (0, LANES - lora), (0, 0)))
    aup_p = jnp.pad(a_up, ((0, 0), (lora, LANES - lora - a_up.shape[1]), (0, 0)))
    out_sds = lambda *lead: jax.ShapeDtypeStruct(lead + (b, t, RWKV_WIDTH), f32)
    full = lambda shape: pl.BlockSpec(shape, lambda bi, i: (0,) * len(shape))

    def halo(cols, side):
        if side == 0:
            return pl.BlockSpec((None, tb, cols), lambda bi, i: (bi, i, 0))
        if side < 0:
            return pl.BlockSpec((None, 8, cols), lambda bi, i: (bi, jnp.maximum(i * nb8 - 1, 0), 0))
        return pl.BlockSpec((None, 8, cols), lambda bi, i: (bi, jnp.minimum((i + 1) * nb8, t // 8 - 1), 0))

    return pl.pallas_call(
        functools.partial(_rwkv_prep_kernel, tb=tb),
        out_shape=[jax.ShapeDtypeStruct((b, t, 3 * RWKV_WIDTH), f32), out_sds(), out_sds(N_DIRS), out_sds(N_DIRS)],
        grid=(b, t // tb),
        in_specs=[halo(TAIL_A_COLS, 0), halo(TAIL_A_COLS, -1), halo(TAIL_A_COLS, 1),
                  halo(TAIL_TN, 0), halo(TAIL_TN, -1), halo(TAIL_TN, 1),
                  full((1, SHIFT_COLS)), full((N_DIRS, LANES, RWKV_WIDTH)), full((N_DIRS, LANES, RWKV_WIDTH)),
                  full((LANES, RWKV_WIDTH)), full((N_DIRS, 1, RWKV_WIDTH)), full((N_DIRS, 1, RWKV_WIDTH))],
        out_specs=[pl.BlockSpec((None, tb, 3 * RWKV_WIDTH), lambda bi, i: (bi, i, 0)),
                   pl.BlockSpec((None, tb, RWKV_WIDTH), lambda bi, i: (bi, i, 0)),
                   pl.BlockSpec((N_DIRS, None, tb, RWKV_WIDTH), lambda bi, i: (0, bi, i, 0)),
                   pl.BlockSpec((N_DIRS, None, tb, RWKV_WIDTH), lambda bi, i: (0, bi, i, 0))],
        scratch_shapes=[pltpu.VMEM((tb + 16, SHIFT_COLS), f32)],
        compiler_params=_cparams(2),
        name="rwkv_prep",
    )(pta, pta, pta, ptb, ptb, ptb, mu_p, wup_p, aup_p, g_up, w0.reshape(N_DIRS, 1, -1), a0.reshape(N_DIRS, 1, -1))


def _stack2(x, m0, m1):
    return jnp.concatenate([x * m0, x * m1], axis=0)


def _rwkv_kernel(r_ref, k_ref, v_ref, gate_ref, lw0_ref, lw1_ref, eta0_ref, eta1_ref,
                 kk_ref, ka_ref, rk_ref, lnw_ref, lnb_ref, o_ref,
                 wt_s, uv_s, yv_s, arb_s, rt_s, bh_s, sk_s, eg_s, yacc, kk_s, gc_s, *, t, seg):
    n = t // CHUNK
    nseg = n // seg
    assert n % seg == 0 and n % 2 == 0 and nseg >= 2
    c2 = 2 * CHUNK
    lane = _iota2((1, LANES), 1)
    m0 = (lane < RWKV_HD).astype(f32)
    m1 = 1.0 - m0
    ii, jj, same, ti, tj = _pair_masks()
    bd = same.astype(bf16)
    eye = (ii == jj).astype(f32)
    lvl = _level_masks(ti, tj, CHUNK)
    masks = ((same & (ti >= tj), same & (ti > tj)), (same & (ti <= tj), same & (ti < tj)))
    lw_refs = (lw0_ref, lw1_ref)
    eta_refs = (eta0_ref, eta1_ref)
    kkw = kk_ref[...]
    kaw = ka_ref[...]

    def pre(p, carry):
        rows = pl.ds(pl.multiple_of(p * LANES, LANES), LANES)
        kkr = k_ref[rows, :] * kkw
        kk_s[rows, :] = kkr * lax.rsqrt(_mm_rsel(kkr * kkr, bd) + L2_EPS)
        for d in range(N_DIRS):
            gc_s[d, rows, :] = _mm_sel(masks[d][0].astype(bf16), lw_refs[d][rows, :])
        return carry

    lax.fori_loop(0, t // LANES, pre, 0, unroll=2)

    def prep_chunks(problems, slot, hook):
        pbs = []
        for d, c, j in problems:
            rows = pl.ds(pl.multiple_of(c * CHUNK, CHUNK), CHUNK)
            r = r_ref[rows, :]
            k = k_ref[rows, :]
            lw = lw_refs[d][rows, :]
            eta = eta_refs[d][rows, :]
            kk = kk_s[rows, :]
            kd = k * (1.0 + (eta - 1.0) * kaw)
            bvec = -kk * eta
            gcum = gc_s[d, rows, :]
            gtot = gcum[CHUNK - 1:CHUNK, :] if d == 0 else gcum[0:1, :]
            eneg = jnp.exp(-gcum)
            erem = jnp.exp(gtot - gcum)
            at2 = _stack2(kk * jnp.exp(gcum - lw), m0, m1)
            rt2 = _stack2(r * jnp.exp(gcum), m0, m1)
            kt2 = _stack2(kd * eneg, m0, m1)
            bt2 = _stack2(bvec * eneg, m0, m1)
            pbs.append(dict(d=d, slot=j, at2=at2, rt2=rt2, gtot=gtot,
                            v2b=_stack2(v_ref[rows, :], m0, m1).astype(bf16),
                            bh=_stack2(bvec * erem, m0, m1).astype(bf16),
                            kh=_stack2(kd * erem, m0, m1).astype(bf16),
                            lhs=jnp.concatenate([at2, rt2], axis=0).astype(bf16),
                            rhs=jnp.concatenate([kt2, bt2], axis=0).astype(bf16)))
        ggs = [_nt(pb["lhs"], pb["rhs"]) for pb in pbs]
        hook()
        for pb, gg in zip(pbs, ggs):
            incl, strict = masks[pb["d"]]
            pb["a_ak"] = jnp.where(strict, gg[0:c2, 0:c2], 0.0).astype(bf16)
            pb["a_ab"] = jnp.where(strict, gg[0:c2, c2:2 * c2], 0.0)
            pb["a_rk"] = jnp.where(incl, gg[c2:2 * c2, 0:c2], 0.0).astype(bf16)
            pb["a_rb"] = jnp.where(incl, gg[c2:2 * c2, c2:2 * c2], 0.0).astype(bf16)
        akv = [_mm(pb["a_ak"], pb["v2b"]) for pb in pbs]
        yv = [_mm(pb["a_rk"], pb["v2b"]) for pb in pbs]
        sk = [_tn(pb["v2b"], pb["kh"]) for pb in pbs]
        hook()
        tinvs = _tri_inverse([-pb["a_ab"] for pb in pbs], eye, lvl, hook)
        wus = []
        for pb, tinv, av in zip(pbs, tinvs, akv):
            rhs = jnp.concatenate([pb["at2"], av], axis=1).astype(bf16)
            t1, t2 = _split2(tinv)
            wus.append(_mm(jnp.concatenate([t1, t2], axis=1), jnp.concatenate([rhs, rhs], axis=0)))
        for i, pb in enumerate(pbs):
            d, j = pb["d"], pb["slot"]
            srows = slice(j * c2, (j + 1) * c2)
            wt_s[slot, d, srows, :] = wus[i][:, :LANES].astype(bf16)
            uv_s[slot, d, srows, :] = wus[i][:, LANES:]
            yv_s[slot, d, srows, :] = yv[i]
            arb_s[slot, d, srows, :] = pb["a_rb"]
            rt_s[slot, d, srows, :] = pb["rt2"].astype(bf16)
            bh_s[slot, d, srows, :] = pb["bh"]
            sk_s[slot, d, srows, :] = sk[i]
            eg_s[slot, d, j * 8:(j + 1) * 8, :] = jnp.broadcast_to(jnp.exp(pb["gtot"]), (8, LANES))

    n_hooks = 2 * (len(lvl) - 1) + 2
    n_stages = seg * 2
    dirs = range(N_DIRS)

    def chunk_of(d, sg, j):
        c = sg * seg + j
        return c if d == 0 else n - 1 - c

    def prep_seg(sg, slot, hook):
        prep_chunks([(d, chunk_of(d, sg, j), j) for j in range(seg) for d in dirs], slot, hook)

    def scan_seg(sg, slot, st):
        for j in range(seg):
            srows = slice(j * c2, (j + 1) * c2)
            sb = [st[d].astype(bf16) for d in dirs]
            ws = [_nt(wt_s[slot, d, srows, :], sb[d]) for d in dirs]
            rs = [_nt(rt_s[slot, d, srows, :], sb[d]) for d in dirs]
            yield
            u2b = [(uv_s[slot, d, srows, :] + ws[d]).astype(bf16) for d in dirs]
            au = [_mm(arb_s[slot, d, srows, :], u2b[d]) for d in dirs]
            ub = [_tn(u2b[d], bh_s[slot, d, srows, :]) for d in dirs]
            yield
            for d in dirs:
                rows = pl.ds(pl.multiple_of(chunk_of(d, sg, j) * CHUNK, CHUNK), CHUNK)
                y2 = yv_s[slot, d, srows, :] + rs[d] + au[d]
                st[d] = st[d] * eg_s[slot, d, j * 8:(j + 1) * 8, :][0:1, :] + sk_s[slot, d, srows, :] + ub[d]
                yacc[rows, :] += y2[0:CHUNK, :] + y2[CHUNK:c2, :]

    yacc[...] = jnp.zeros(yacc.shape, f32)
    prep_seg(0, 0, _no_hook)

    def body(sg, states):
        st = list(states)
        gen = scan_seg(sg, sg % 2, st)
        prep_seg(sg + 1, (sg + 1) % 2, _stage_hook(gen, n_stages, n_hooks))
        for _ in gen:
            pass
        return tuple(st)

    s0 = jnp.zeros((LANES, LANES), f32)
    st = list(lax.fori_loop(0, nseg - 1, body, (s0, s0)))
    for _ in scan_seg(nseg - 1, (nseg - 1) % 2, st):
        pass

    tb = min(512, t)
    inv_hd = 1.0 / RWKV_HD
    for ci in range(t // tb):
        rows = slice(ci * tb, (ci + 1) * tb)
        y = yacc[rows, :]
        mean = _mm_rsel(y, bd) * inv_hd
        dlt = y - mean
        var = _mm_rsel(dlt * dlt, bd) * inv_hd
        yn = dlt * lax.rsqrt(var + RWKV_GN_EPS) * lnw_ref[...] + lnb_ref[...]
        k = k_ref[rows, :]
        kd_sum = k * (2.0 + (eta0_ref[rows, :] + eta1_ref[rows, :] - 2.0) * kaw)
        bonus = _mm_rsel(r_ref[rows, :] * kd_sum * rk_ref[...], bd) * v_ref[rows, :]
        o_ref[rows, :] = ((yn + bonus) * gate_ref[rows, :]).astype(o_ref.dtype)


def rwkv_branch(rkv, gate, lw, eta, k_k, k_a, r_k, ln_w, ln_b, seg=4):
    b, t, _ = gate.shape
    n = t // CHUNK
    seg = min(seg, n // 2)
    c2 = 2 * CHUNK
    blk = lambda off: pl.BlockSpec((None, t, LANES), lambda bi, p: (bi, 0, off + p))
    dblk = lambda d: pl.BlockSpec((None, None, t, LANES), lambda bi, p: (d, bi, 0, p))
    vec = pl.BlockSpec((1, LANES), lambda bi, p: (0, p))
    row = lambda x: x.reshape(1, RWKV_WIDTH)
    seg_f32 = pltpu.VMEM((2, N_DIRS, seg * c2, LANES), f32)
    seg_b16 = pltpu.VMEM((2, N_DIRS, seg * c2, LANES), bf16)
    return pl.pallas_call(
        functools.partial(_rwkv_kernel, t=t, seg=seg),
        out_shape=jax.ShapeDtypeStruct((b, t, RWKV_WIDTH), bf16),
        grid=(b, RWKV_PAIRS),
        in_specs=[blk(0), blk(RWKV_PAIRS), blk(2 * RWKV_PAIRS), blk(0),
                  dblk(0), dblk(1), dblk(0), dblk(1), vec, vec, vec, vec, vec],
        out_specs=blk(0),
        scratch_shapes=[seg_b16,
                        seg_f32,
                        seg_f32,
                        seg_b16,
                        seg_b16,
                        seg_b16,
                        seg_f32,
                        pltpu.VMEM((2, N_DIRS, seg * 8, LANES), f32),
                        pltpu.VMEM((t, LANES), f32),
                        pltpu.VMEM((t, LANES), f32),
                        pltpu.VMEM((N_DIRS, t, LANES), f32)],
        compiler_params=_cparams(2),
        name="rwkv_scan",
    )(rkv, rkv, rkv, gate, lw, lw, eta, eta, row(k_k), row(k_a), row(r_k), row(ln_w), row(ln_b))


def _tail_b_weights(w_in_t, layer):
    last = w_in_t[layer, MAIN_COLS + TAIL_A_COLS:, :]
    assert last.shape[0] == TAIL_B_USED
    return jnp.pad(last, ((0, TAIL_TN - TAIL_B_USED), (0, 0)))[None]


def kernel(x, norm_mix, w_in, w_out, na_rpb, na_gain, gdn_conv_w, gdn_a_log, gdn_dt_bias, gdn_gain, rwkv_mu, rwkv_w0, rwkv_w_up, rwkv_a0, rwkv_a_up, rwkv_g_up, rwkv_k_k, rwkv_k_a, rwkv_r_k, rwkv_ln_w, rwkv_ln_b, norm_ffn, ffn_w_gate, ffn_w_up, ffn_w_down, moe_router, moe_w_gate, moe_w_up, moe_w_down, norm_final):
    b, t, d = x.shape
    n = b * t
    depth = w_in.shape[0]
    h = x.reshape(n, d)
    w_in_t = jnp.swapaxes(w_in, 1, 2)
    for layer in range(depth):
        u = rmsnorm(h, norm_mix[layer], bf16)
        pm = proj(u, w_in_t, layer, MAIN_COLS, tn=512).reshape(b, t, MAIN_COLS)
        pta = proj(u, w_in_t, layer, TAIL_A_COLS, tn=TAIL_TN,
                   col_block0=MAIN_COLS // TAIL_TN).reshape(b, t, TAIL_A_COLS)
        ptb = proj(u, _tail_b_weights(w_in_t, layer), 0, TAIL_TN, tn=TAIL_TN).reshape(b, t, TAIL_TN)

        y_a = na_branch(pm, na_rpb[layer], na_gain[layer])
        qkv = gdn_prep(pm, gdn_conv_w, layer)
        y_b = gdn_branch(pm, pta, qkv, gdn_a_log[layer], gdn_dt_bias[layer], gdn_gain[layer])
        rkv, gate, lw, eta = rwkv_prep(pta, ptb, rwkv_mu[layer], rwkv_w0[layer], rwkv_w_up[layer],
                                       rwkv_a0[layer], rwkv_a_up[layer], rwkv_g_up[layer])
        y_c = rwkv_branch(rkv, gate, lw, eta, rwkv_k_k[layer], rwkv_k_a[layer], rwkv_r_k[layer],
                          rwkv_ln_w[layer], rwkv_ln_b[layer])
        h = outproj(y_a.reshape(n, -1), y_b.reshape(n, -1), y_c.reshape(n, -1), w_out, layer, h)

        i = layer // 2
        if layer % 2 == 0:
            u = rmsnorm(h, norm_ffn[layer], bf16)
            act = gateup(u, ffn_w_gate, ffn_w_up, i)
            half = ffn_w_down.shape[1] // 2
            assert half % LANES == 0
            for kb in range(2):
                h = down(act, ffn_w_down, i, h, kb, half)
        else:
            h = moe_ffn(h, norm_ffn[layer], moe_router[i], moe_w_gate, moe_w_up, moe_w_down, i)
    return rmsnorm(h, norm_final, f32).reshape(b, t, d)
```

```python
import functools

import jax
import jax.numpy as jnp
import numpy as np
from jax import lax
from jax.experimental import pallas as pl
from jax.experimental.pallas import tpu as pltpu

f32 = jnp.float32
bf16 = jnp.bfloat16
HI = lax.Precision.HIGHEST

LANES = 128
VMEM_LIMIT_BYTES = 56 * 1024 * 1024

D_MODEL = 4096
GRID_W = 64
NA_HEADS, NA_HD = 12, 128
NA_WIDTH = NA_HEADS * NA_HD
NA_WIN_ROWS, NA_WIN_COLS = 8, 16
GDN_HEADS, GDN_HD = 12, 128
GDN_WIDTH = GDN_HEADS * GDN_HD
GDN_CONV = 5
CHUNK = 64
RWKV_HEADS, RWKV_HD = 16, 64
RWKV_WIDTH = RWKV_HEADS * RWKV_HD
RWKV_PAIRS = RWKV_HEADS // 2
N_DIRS = 2
MAIN_COLS = 3 * NA_WIDTH + 4 * GDN_WIDTH
RWKV_COLS = 3 * RWKV_WIDTH + 64 + 64 + 128
SHIFT_COLS = RWKV_COLS
assert SHIFT_COLS % LANES == 0
GATE_COLS = 2 * N_DIRS * GDN_HEADS
TAIL_TN = 512
TAIL_A_COLS = 3072
TAIL_B_USED = GATE_COLS + RWKV_COLS - TAIL_A_COLS
assert MAIN_COLS % TAIL_TN == 0 and TAIL_A_COLS % TAIL_TN == 0 and 0 < TAIL_B_USED <= TAIL_TN
MOE_EXPERTS = 8
RMS_EPS = 1e-6
L2_EPS = 1e-6
RWKV_GN_EPS = 64e-5
NEG = -1e30


def _cparams(n_axes):
    return pltpu.CompilerParams(dimension_semantics=("arbitrary",) * n_axes,
                                vmem_limit_bytes=VMEM_LIMIT_BYTES)


def _nt(a, b, precision=None):
    return lax.dot_general(a, b, (((1,), (1,)), ((), ())), precision=precision,
                           preferred_element_type=f32)


def _tn(a, b, precision=None):
    return lax.dot_general(a, b, (((0,), (0,)), ((), ())), precision=precision,
                           preferred_element_type=f32)


def _mm(a, b, precision=None):
    return jnp.dot(a, b, precision=precision, preferred_element_type=f32)


def _sigmoid(x):
    return 1.0 / (1.0 + jnp.exp(-x))


def _softplus(x):
    return jnp.maximum(x, 0.0) + jnp.log(1.0 + jnp.exp(-jnp.abs(x)))


def _iota2(shape, axis):
    return lax.broadcasted_iota(jnp.int32, shape, axis)


def _rmsnorm_kernel(x_ref, g_ref, o_ref):
    x = x_ref[...]
    ms = jnp.mean(x * x, axis=-1, keepdims=True)
    o_ref[...] = (x * lax.rsqrt(ms + RMS_EPS) * g_ref[...]).astype(o_ref.dtype)


def rmsnorm(x, gain, out_dtype):
    n, d = x.shape
    tm = min(256, n)
    return pl.pallas_call(
        _rmsnorm_kernel,
        out_shape=jax.ShapeDtypeStruct((n, d), out_dtype),
        grid=(n // tm,),
        in_specs=[pl.BlockSpec((tm, d), lambda i: (i, 0)), pl.BlockSpec((1, d), lambda i: (0, 0))],
        out_specs=pl.BlockSpec((tm, d), lambda i: (i, 0)),
        compiler_params=_cparams(1),
        name="rmsnorm",
    )(x, gain.reshape(1, d))


def _rmsnorm_router_kernel(x_ref, g_ref, r_ref, o_ref, e_ref, w_ref):
    x = x_ref[...]
    ms = jnp.mean(x * x, axis=-1, keepdims=True)
    u = x * lax.rsqrt(ms + RMS_EPS) * g_ref[...]
    o_ref[...] = u
    logits = _mm(u, r_ref[...], HI)
    lane = _iota2(logits.shape, 1)
    lg = jnp.where(lane < MOE_EXPERTS, logits, NEG)
    m1 = jnp.max(lg, axis=-1, keepdims=True)
    i1 = jnp.min(jnp.where(lg == m1, lane, LANES), axis=-1, keepdims=True)
    lg2 = jnp.where(lane == i1, NEG, lg)
    m2 = jnp.max(lg2, axis=-1, keepdims=True)
    i2 = jnp.min(jnp.where(lg2 == m2, lane, LANES), axis=-1, keepdims=True)
    e2 = jnp.exp(m2 - m1)
    w1 = 1.0 / (1.0 + e2)
    w2 = e2 / (1.0 + e2)
    e_ref[...] = jnp.where(lane == 0, i1, jnp.where(lane == 1, i2, 0))
    w_ref[...] = jnp.where(lane == 0, w1, jnp.where(lane == 1, w2, 0.0))


def rmsnorm_router(x, gain, router):
    n, d = x.shape
    tm = min(256, n)
    r_pad = jnp.pad(router, ((0, 0), (0, LANES - router.shape[1])))
    return pl.pallas_call(
        _rmsnorm_router_kernel,
        out_shape=[jax.ShapeDtypeStruct((n, d), f32), jax.ShapeDtypeStruct((n, LANES), jnp.int32),
                   jax.ShapeDtypeStruct((n, LANES), f32)],
        grid=(n // tm,),
        in_specs=[pl.BlockSpec((tm, d), lambda i: (i, 0)), pl.BlockSpec((1, d), lambda i: (0, 0)),
                  pl.BlockSpec((d, LANES), lambda i: (0, 0))],
        out_specs=[pl.BlockSpec((tm, d), lambda i: (i, 0)), pl.BlockSpec((tm, LANES), lambda i: (i, 0)),
                   pl.BlockSpec((tm, LANES), lambda i: (i, 0))],
        compiler_params=_cparams(1),
        name="rmsnorm_router",
    )(x, gain.reshape(1, d), r_pad)


def _moe_rank_kernel(e_ref, rank_ref, cnt_ref, carry, *, tb):
    @pl.when(pl.program_id(0) == 0)
    def _():
        carry[...] = jnp.zeros(carry.shape, f32)

    ids = e_ref[...]
    lane = _iota2(ids.shape, 1)
    oh1 = lane == ids[:, 0:1]
    oh2 = lane == ids[:, 1:2]
    f1 = jnp.where(oh1, 1.0, 0.0)
    f2 = jnp.where(oh2, 1.0, 0.0)
    earlier = jnp.where(_iota2((tb, tb), 0) > _iota2((tb, tb), 1), 1.0, 0.0).astype(bf16)
    p1 = _mm(earlier, f1.astype(bf16))
    p2 = _mm(earlier, f2.astype(bf16))
    c1 = jnp.sum(f1, axis=0, keepdims=True)
    c2 = jnp.sum(f2, axis=0, keepdims=True)
    base = carry[0:1, :]
    r1 = jnp.sum(jnp.where(oh1, p1 + base, 0.0), axis=-1, keepdims=True)
    r2 = jnp.sum(jnp.where(oh2, p2 + (base + c1), 0.0), axis=-1, keepdims=True)
    rank_ref[...] = jnp.where(lane == 0, r1, jnp.where(lane == 1, r2, 0.0))
    carry[...] = jnp.broadcast_to(base + c1 + c2, carry.shape)
    cnt_ref[...] = carry[...]


def moe_rank(ids, tb=256):
    n = ids.shape[0]
    tb = min(tb, n)
    return pl.pallas_call(
        functools.partial(_moe_rank_kernel, tb=tb),
        out_shape=[jax.ShapeDtypeStruct((n, LANES), f32), jax.ShapeDtypeStruct((8, LANES), f32)],
        grid=(n // tb,),
        in_specs=[pl.BlockSpec((tb, LANES), lambda i: (i, 0))],
        out_specs=[pl.BlockSpec((tb, LANES), lambda i: (i, 0)), pl.BlockSpec((8, LANES), lambda i: (0, 0))],
        scratch_shapes=[pltpu.VMEM((8, LANES), f32)],
        compiler_params=_cparams(1),
        name="moe_rank",
    )(ids)


def _gather_rows_kernel(idx_ref, src_ref, o_ref, buf, sem, *, tm):
    base = pl.program_id(0) * tm

    def row_copy(r, row):
        return pltpu.make_async_copy(src_ref.at[pl.ds(row, 1), :], buf.at[pl.ds(r, 1), :], sem)

    def start(r, c):
        row_copy(r, idx_ref[base + r]).start()
        return c

    def wait(r, c):
        row_copy(r, 0).wait()
        return c

    lax.fori_loop(0, tm, start, 0, unroll=8)
    lax.fori_loop(0, tm, wait, 0, unroll=8)
    o_ref[...] = buf[...].astype(o_ref.dtype)


def gather_rows(src, idx, out_dtype, tm=256):
    m = idx.shape[0]
    d = src.shape[1]
    tm = min(tm, m)
    assert m % tm == 0
    return pl.pallas_call(
        functools.partial(_gather_rows_kernel, tm=tm),
        out_shape=jax.ShapeDtypeStruct((m, d), out_dtype),
        grid_spec=pltpu.PrefetchScalarGridSpec(
            num_scalar_prefetch=1, grid=(m // tm,),
            in_specs=[pl.BlockSpec(memory_space=pl.ANY)],
            out_specs=pl.BlockSpec((tm, d), lambda i, idx_ref: (i, 0)),
            scratch_shapes=[pltpu.VMEM((tm, d), f32), pltpu.SemaphoreType.DMA(())]),
        compiler_params=_cparams(1),
        name="gather_rows",
    )(idx, src)


def _moe_combine_kernel(d1_ref, d2_ref, y_ref, h_ref, w_ref, g_ref, o_ref, buf, sems, *, tm, final_norm):
    base = pl.program_id(0) * tm
    dests = (d1_ref, d2_ref)

    def row_copy(k, r, row):
        return pltpu.make_async_copy(y_ref.at[pl.ds(row, 1), :], buf.at[k, pl.ds(r, 1), :], sems.at[k])

    def start(r, c):
        for k in range(2):
            row_copy(k, r, dests[k][base + r]).start(priority=k)
        return c

    def wait(r, c):
        for k in range(2):
            row_copy(k, r, 0).wait()
        return c

    lax.fori_loop(0, tm, start, 0, unroll=8)
    lax.fori_loop(0, tm, wait, 0, unroll=8)
    w = w_ref[...]
    o = h_ref[...] + w[:, 0:1] * buf[0] + w[:, 1:2] * buf[1]
    if final_norm:
        o = o * lax.rsqrt(jnp.mean(o * o, axis=-1, keepdims=True) + RMS_EPS) * g_ref[...]
    o_ref[...] = o


def moe_combine(y, d1, d2, h, wts, gain, final_norm, tm=256):
    n, d = h.shape
    tm = min(tm, n)
    return pl.pallas_call(
        functools.partial(_moe_combine_kernel, tm=tm, final_norm=final_norm),
        out_shape=jax.ShapeDtypeStruct((n, d), f32),
        grid_spec=pltpu.PrefetchScalarGridSpec(
            num_scalar_prefetch=2, grid=(n // tm,),
            in_specs=[pl.BlockSpec(memory_space=pl.ANY),
                      pl.BlockSpec((tm, d), lambda i, a, b: (i, 0)),
                      pl.BlockSpec((tm, LANES), lambda i, a, b: (i, 0)),
                      pl.BlockSpec((1, d), lambda i, a, b: (0, 0))],
            out_specs=pl.BlockSpec((tm, d), lambda i, a, b: (i, 0)),
            scratch_shapes=[pltpu.VMEM((2, tm, d), f32), pltpu.SemaphoreType.DMA((2,))]),
        compiler_params=_cparams(1),
        name="moe_combine",
    )(d1, d2, y, h, wts, gain.reshape(1, d))


def _proj_t_kernel(x_ref, w_ref, o_ref, wb_ref):
    @pl.when(pl.program_id(1) == 0)
    def _():
        wb_ref[...] = w_ref[...].T.astype(bf16)

    o_ref[...] = _mm(x_ref[...], wb_ref[...]).astype(o_ref.dtype)


def proj(x, wt, w_index, n_cols, tn, out_dtype=f32, tm=1024, col_block0=0):
    n, k = x.shape
    tm = min(tm, n)
    assert n_cols % tn == 0 and n % tm == 0 and (col_block0 * tn + n_cols) <= wt.shape[1]
    return pl.pallas_call(
        _proj_t_kernel,
        out_shape=jax.ShapeDtypeStruct((n, n_cols), out_dtype),
        grid=(n_cols // tn, n // tm),
        in_specs=[pl.BlockSpec((tm, k), lambda j, i: (i, 0)),
                  pl.BlockSpec((None, tn, k), lambda j, i: (w_index, col_block0 + j, 0))],
        out_specs=pl.BlockSpec((tm, tn), lambda j, i: (i, j)),
        scratch_shapes=[pltpu.VMEM((k, tn), bf16)],
        compiler_params=_cparams(2),
        name="proj",
    )(x, wt)


def _outproj_kernel(xa_ref, xb_ref, xc_ref, w_ref, res_ref, o_ref, wb_ref):
    @pl.when(pl.program_id(1) == 0)
    def _():
        wb_ref[...] = w_ref[...].astype(bf16)

    acc = _mm(xa_ref[...], wb_ref[0:NA_WIDTH, :])
    acc += _mm(xb_ref[...], wb_ref[NA_WIDTH:NA_WIDTH + GDN_WIDTH, :])
    acc += _mm(xc_ref[...], wb_ref[NA_WIDTH + GDN_WIDTH:, :])
    o_ref[...] = res_ref[...] + acc


def outproj(ya, yb, yc, w, layer, res, tn=512, tm=1024):
    n = ya.shape[0]
    d = w.shape[-1]
    tm = min(tm, n)
    return pl.pallas_call(
        _outproj_kernel,
        out_shape=jax.ShapeDtypeStruct((n, d), f32),
        grid=(d // tn, n // tm),
        in_specs=[pl.BlockSpec((tm, NA_WIDTH), lambda j, i: (i, 0)),
                  pl.BlockSpec((tm, GDN_WIDTH), lambda j, i: (i, 0)),
                  pl.BlockSpec((tm, RWKV_WIDTH), lambda j, i: (i, 0)),
                  pl.BlockSpec((None, D_MODEL, tn), lambda j, i: (layer, 0, j)),
                  pl.BlockSpec((tm, tn), lambda j, i: (i, j))],
        out_specs=pl.BlockSpec((tm, tn), lambda j, i: (i, j)),
        scratch_shapes=[pltpu.VMEM((D_MODEL, tn), bf16)],
        compiler_params=_cparams(2),
        name="outproj",
    )(ya, yb, yc, w, res)


def _gateup_kernel(x_ref, wg_ref, wu_ref, o_ref, wgb_ref, wub_ref):
    @pl.when(pl.program_id(1) == 0)
    def _():
        wgb_ref[...] = wg_ref[...].astype(bf16)
        wub_ref[...] = wu_ref[...].astype(bf16)

    x = x_ref[...]
    g = _mm(x, wgb_ref[...])
    u = _mm(x, wub_ref[...])
    o_ref[...] = (g * _sigmoid(g) * u).astype(o_ref.dtype)


def gateup(x, w_gate, w_up, w_index, tn=256, tm=1024):
    n, k = x.shape
    ff = w_gate.shape[-1]
    tm = min(tm, n)
    return pl.pallas_call(
        _gateup_kernel,
        out_shape=jax.ShapeDtypeStruct((n, ff), bf16),
        grid=(ff // tn, n // tm),
        in_specs=[pl.BlockSpec((tm, k), lambda j, i: (i, 0)),
                  pl.BlockSpec((None, k, tn), lambda j, i: (w_index, 0, j)),
                  pl.BlockSpec((None, k, tn), lambda j, i: (w_index, 0, j))],
        out_specs=pl.BlockSpec((tm, tn), lambda j, i: (i, j)),
        scratch_shapes=[pltpu.VMEM((k, tn), bf16), pltpu.VMEM((k, tn), bf16)],
        compiler_params=_cparams(2),
        name="gateup",
    )(x, w_gate, w_up)


def _tile_changed(te_ref, i):
    return (i == 0) | (te_ref[i] != te_ref[jnp.maximum(i - 1, 0)])


def _moe_gateup_kernel(te_ref, nu_ref, x_ref, wg_ref, wu_ref, o_ref, wgb_ref, wub_ref):
    i = pl.program_id(1)
    used = i < nu_ref[0]

    @pl.when(used & _tile_changed(te_ref, i))
    def _():
        wgb_ref[...] = wg_ref[...].astype(bf16)
        wub_ref[...] = wu_ref[...].astype(bf16)

    @pl.when(used)
    def _():
        x = x_ref[...]
        g = _mm(x, wgb_ref[...])
        u = _mm(x, wub_ref[...])
        o_ref[...] = (g * _sigmoid(g) * u).astype(o_ref.dtype)

    @pl.when(jnp.logical_not(used))
    def _():
        o_ref[...] = jnp.zeros(o_ref.shape, o_ref.dtype)


def _moe_down_kernel(te_ref, nu_ref, x_ref, w_ref, o_ref, wb_ref):
    i = pl.program_id(1)
    used = i < nu_ref[0]

    @pl.when(used & _tile_changed(te_ref, i))
    def _():
        wb_ref[...] = w_ref[...].astype(bf16)

    @pl.when(used)
    def _():
        o_ref[...] = _mm(x_ref[...], wb_ref[...])

    @pl.when(jnp.logical_not(used))
    def _():
        o_ref[...] = jnp.zeros(o_ref.shape, o_ref.dtype)


def _grouped_specs(tm, k, tn, w_index, n_w):
    last = lambda i, nu: jnp.minimum(i, nu[0] - 1)
    x_spec = pl.BlockSpec((tm, k), lambda j, i, te, nu: (last(i, nu), 0))
    w_spec = pl.BlockSpec((None, None, k, tn), lambda j, i, te, nu: (w_index, te[last(i, nu)], 0, j))
    o_spec = pl.BlockSpec((tm, tn), lambda j, i, te, nu: (i, j))
    return [x_spec] + [w_spec] * n_w, o_spec


def moe_gateup(x, tile_expert, n_used, w_gate, w_up, w_index, tm, tn=512):
    m, k = x.shape
    ff = w_gate.shape[-1]
    in_specs, out_spec = _grouped_specs(tm, k, tn, w_index, 2)
    return pl.pallas_call(
        _moe_gateup_kernel,
        out_shape=jax.ShapeDtypeStruct((m, ff), bf16),
        grid_spec=pltpu.PrefetchScalarGridSpec(
            num_scalar_prefetch=2, grid=(ff // tn, m // tm), in_specs=in_specs, out_specs=out_spec,
            scratch_shapes=[pltpu.VMEM((k, tn), bf16), pltpu.VMEM((k, tn), bf16)]),
        compiler_params=_cparams(2),
        name="moe_gateup",
    )(tile_expert, n_used, x, w_gate, w_up)


def moe_down(x, tile_expert, n_used, w_down, w_index, tm, tn=512):
    m, k = x.shape
    d = w_down.shape[-1]
    in_specs, out_spec = _grouped_specs(tm, k, tn, w_index, 1)
    return pl.pallas_call(
        _moe_down_kernel,
        out_shape=jax.ShapeDtypeStruct((m, d), f32),
        grid_spec=pltpu.PrefetchScalarGridSpec(
            num_scalar_prefetch=2, grid=(d // tn, m // tm), in_specs=in_specs, out_specs=out_spec,
            scratch_shapes=[pltpu.VMEM((k, tn), bf16)]),
        compiler_params=_cparams(2),
        name="moe_down",
    )(tile_expert, n_used, x, w_down)


def moe_ffn(h, gain, router, w_gate, w_up, w_down, w_index, closing_gain, closing_norm, tm=512):
    n, d = h.shape
    ne = w_gate.shape[1]
    tm = min(tm, n)
    u, ids, wts = rmsnorm_router(h, gain, router)
    rank, counts = moe_rank(ids)
    counts = counts[0, :ne].astype(jnp.int32)
    padded = (counts + tm - 1) // tm * tm
    ends = jnp.cumsum(padded)
    e2 = ids[:, :2]
    dest = (ends - padded)[e2] + rank[:, :2].astype(jnp.int32)
    n_tiles = 2 * n // tm + ne
    tile_expert = jnp.minimum(jnp.searchsorted(ends, jnp.arange(n_tiles) * tm, side="right"), ne - 1)
    n_used = (ends[-1] // tm).reshape(1)
    token = jnp.broadcast_to(jnp.arange(n, dtype=jnp.int32)[:, None], (n, 2))
    src = jnp.zeros((n_tiles * tm,), jnp.int32).at[dest.reshape(-1)].set(token.reshape(-1))
    xs = gather_rows(u, src, bf16)
    act = moe_gateup(xs, tile_expert.astype(jnp.int32), n_used.astype(jnp.int32), w_gate, w_up, w_index, tm)
    ys = moe_down(act, tile_expert.astype(jnp.int32), n_used.astype(jnp.int32), w_down, w_index, tm)
    return moe_combine(ys, dest[:, 0], dest[:, 1], h, wts, closing_gain, closing_norm)


def _down_kernel(x_ref, w_ref, res_ref, o_ref, wb_ref):
    @pl.when(pl.program_id(1) == 0)
    def _():
        wb_ref[...] = w_ref[...].astype(bf16)

    o_ref[...] = res_ref[...] + _mm(x_ref[...], wb_ref[...])


def down(x, w, w_index, res, k_block, tk, tn=512, tm=512):
    n = x.shape[0]
    d = w.shape[-1]
    tm = min(tm, n)
    return pl.pallas_call(
        _down_kernel,
        out_shape=jax.ShapeDtypeStruct((n, d), f32),
        grid=(d // tn, n // tm),
        in_specs=[pl.BlockSpec((tm, tk), lambda j, i: (i, k_block)),
                  pl.BlockSpec((None, tk, tn), lambda j, i: (w_index, k_block, j)),
                  pl.BlockSpec((tm, tn), lambda j, i: (i, j))],
        out_specs=pl.BlockSpec((tm, tn), lambda j, i: (i, j)),
        scratch_shapes=[pltpu.VMEM((tk, tn), bf16)],
        compiler_params=_cparams(2),
        name="down",
    )(x, w, res)


def _na_bias_table(rpb):
    c = np.arange(GRID_W)[:, None]
    kc = np.arange(GRID_W)[None, :]
    cs = np.clip(c - NA_WIN_COLS // 2, 0, GRID_W - NA_WIN_COLS)
    valid = (kc >= cs) & (kc < cs + NA_WIN_COLS)
    dc = kc - c + (NA_WIN_COLS - 1)
    onehot = (valid[None] & (dc[None] == np.arange(2 * NA_WIN_COLS - 1)[:, None, None])).astype(np.float32)
    cols = jnp.einsum("hrm,mck->hrck", rpb.astype(f32), jnp.asarray(onehot), precision=HI)
    lo = NA_WIN_ROWS - 1
    tab = jnp.stack([cols[:, lo - off:lo - off + NA_WIN_ROWS] for off in range(NA_WIN_ROWS)], axis=1)
    tab = tab.transpose(0, 1, 3, 2, 4) + jnp.asarray(np.where(valid, 0.0, NEG).astype(np.float32))[:, None, :]
    return tab.reshape(rpb.shape[0], NA_WIN_ROWS, GRID_W, NA_WIN_ROWS * GRID_W)


def _na_kernel(q_ref, k_ref, v_ref, bias_ref, gain_ref, o_ref, kb_ref, vb_ref, *, rows, rows_per_step, lockstep):
    g = pl.program_id(2)

    @pl.when(g == 0)
    def _():
        kb_ref[...] = k_ref[...].astype(bf16)
        vb_ref[...] = v_ref[...].astype(bf16)

    scale = NA_HD ** -0.5
    win = NA_WIN_ROWS * GRID_W

    def body(it, carry):
        qrows, krows, ss = [], [], []
        for li in range(lockstep):
            i = it * lockstep + li
            r = g * rows_per_step + i
            sr = jnp.clip(r - NA_WIN_ROWS // 2, 0, rows - NA_WIN_ROWS)
            qrows.append(pl.ds(pl.multiple_of(i * GRID_W, GRID_W), GRID_W))
            krows.append(pl.ds(pl.multiple_of(sr * GRID_W, GRID_W), win))
            q = q_ref[qrows[li], :].astype(bf16)
            ss.append(_nt(q, kb_ref[krows[li], :]) * scale + bias_ref[r - sr])
        ps, ls = [], []
        for s in ss:
            p = jnp.exp(s - jnp.max(s, axis=-1, keepdims=True))
            ls.append(jnp.sum(p, axis=-1, keepdims=True))
            ps.append(p)
        ovs = [_mm((p / l).astype(bf16), vb_ref[kr, :]) for p, l, kr in zip(ps, ls, krows)]
        for ov, qr in zip(ovs, qrows):
            ms = jnp.mean(ov * ov, axis=-1, keepdims=True)
            o_ref[qr, :] = (ov * lax.rsqrt(ms + RMS_EPS) * gain_ref[...]).astype(o_ref.dtype)
        return carry

    lax.fori_loop(0, rows_per_step // lockstep, body, 0)


def na_branch(pm, rpb, gain, rows_per_step=8):
    b, t, _ = pm.shape
    rows = t // GRID_W
    assert rows >= NA_WIN_ROWS and rows % rows_per_step == 0
    tq = rows_per_step * GRID_W
    bias = _na_bias_table(rpb)
    return pl.pallas_call(
        functools.partial(_na_kernel, rows=rows, rows_per_step=rows_per_step, lockstep=8),
        out_shape=jax.ShapeDtypeStruct((b, t, NA_WIDTH), bf16),
        grid=(b, NA_HEADS, rows // rows_per_step),
        in_specs=[pl.BlockSpec((None, tq, NA_HD), lambda bi, h, g: (bi, g, h)),
                  pl.BlockSpec((None, t, NA_HD), lambda bi, h, g: (bi, 0, NA_HEADS + h)),
                  pl.BlockSpec((None, t, NA_HD), lambda bi, h, g: (bi, 0, 2 * NA_HEADS + h)),
                  pl.BlockSpec((None, NA_WIN_ROWS, GRID_W, NA_WIN_ROWS * GRID_W), lambda bi, h, g: (h, 0, 0, 0)),
                  pl.BlockSpec((None, 1, NA_HD), lambda bi, h, g: (h, 0, 0))],
        out_specs=pl.BlockSpec((None, tq, NA_HD), lambda bi, h, g: (bi, g, h)),
        scratch_shapes=[pltpu.VMEM((t, NA_HD), bf16), pltpu.VMEM((t, NA_HD), bf16)],
        compiler_params=_cparams(3),
        name="na_attention",
    )(pm, pm, pm, bias, gain.reshape(NA_HEADS, 1, NA_HD))


def _gdn_prep_kernel(x_ref, w_ref, o_ref, xp_ref, *, t, tc):
    c = pl.program_id(1)
    pad = 8
    xp_ref[0:pad, :] = jnp.zeros((pad, LANES), f32)
    xp_ref[t + pad:t + 2 * pad, :] = jnp.zeros((pad, LANES), f32)
    xp_ref[pad:t + pad, :] = x_ref[...]
    half = GDN_CONV // 2
    for ci in range(t // tc):
        acc = jnp.zeros((tc, LANES), f32)
        for j in range(GDN_CONV):
            s0 = ci * tc + pad - half + j
            acc = acc + xp_ref[s0:s0 + tc, :] * w_ref[j:j + 1, :]
        y = acc * _sigmoid(acc)
        inv = lax.rsqrt(jnp.sum(y * y, axis=-1, keepdims=True) + L2_EPS)
        fac = jnp.where(c < GDN_HEADS, inv * GDN_HD ** -0.5, jnp.where(c < 2 * GDN_HEADS, inv, 1.0))
        o_ref[ci * tc:(ci + 1) * tc, :] = y * fac


def gdn_prep(pm, conv_w, layer):
    b, t, _ = pm.shape
    tc = min(512, t)
    col0 = 3 * NA_WIDTH // LANES
    nblk = 3 * GDN_WIDTH // LANES
    return pl.pallas_call(
        functools.partial(_gdn_prep_kernel, t=t, tc=tc),
        out_shape=jax.ShapeDtypeStruct((b, t, 3 * GDN_WIDTH), f32),
        grid=(b, nblk),
        in_specs=[pl.BlockSpec((None, t, LANES), lambda bi, c: (bi, 0, col0 + c)),
                  pl.BlockSpec((None, GDN_CONV, LANES), lambda bi, c: (layer, 0, c))],
        out_specs=pl.BlockSpec((None, t, LANES), lambda bi, c: (bi, 0, c)),
        scratch_shapes=[pltpu.VMEM((t + 16, LANES), f32)],
        compiler_params=_cparams(2),
        name="gdn_prep",
    )(pm, conv_w)


def _level_masks(ii, jj, size):
    masks = []
    s = 1
    while s < size:
        masks.append(((ii // (2 * s)) == (jj // (2 * s))) & ((ii // s) != (jj // s)))
        s *= 2
    return masks


def _split2(x):
    hi = x.astype(bf16)
    return hi, (x - hi.astype(f32)).astype(bf16)


def _split3(x):
    x1 = x.astype(bf16)
    r = x - x1.astype(f32)
    x2 = r.astype(bf16)
    return x1, x2, (r - x2.astype(f32)).astype(bf16)


def _mm_f32(a, b):
    a1, a2 = _split2(a)
    b1, b2 = _split2(b)
    return _mm(jnp.concatenate([a1, a2, a1], axis=1), jnp.concatenate([b1, b1, b2], axis=0))


def _mm_hl(a, b):
    a1, a2 = _split2(a)
    b1 = b.astype(bf16)
    return _mm(jnp.concatenate([a1, a2], axis=1), jnp.concatenate([b1, b1], axis=0))


def _mm_sel(sel, x):
    x1, x2, x3 = _split3(x)
    return _mm(jnp.concatenate([sel, sel, sel], axis=1), jnp.concatenate([x1, x2, x3], axis=0))


def _mm_rsel(x, sel):
    x1, x2 = _split2(x)
    return _mm(jnp.concatenate([x1, x2], axis=1), jnp.concatenate([sel, sel], axis=0))


def _nt_sel(sel, x):
    x1, x2, x3 = _split3(x)
    return _nt(jnp.concatenate([sel, sel, sel], axis=1), jnp.concatenate([x1, x2, x3], axis=1))


def _tri_inverse(ms, eye, masks, hook):
    xs = [eye - jnp.where(masks[0], m, 0.0) for m in ms]
    for mask in masks[1:]:
        xcs = [_mm_hl(x, jnp.where(mask, m, 0.0)) for x, m in zip(xs, ms)]
        hook()
        xs = [x - _mm_hl(xc, x) for x, xc in zip(xs, xcs)]
        hook()
    return xs


def _stage_hook(gen, n_stages, n_hooks):
    state = {"calls": 0, "done": 0}

    def hook():
        state["calls"] += 1
        target = -(-state["calls"] * n_stages // n_hooks)
        while state["done"] < min(target, n_stages):
            next(gen, None)
            state["done"] += 1

    return hook


def _no_hook():
    return None


def _pair_masks():
    ii = _iota2((LANES, LANES), 0)
    jj = _iota2((LANES, LANES), 1)
    same = (ii // CHUNK) == (jj // CHUNK)
    ti, tj = ii % CHUNK, jj % CHUNK
    return ii, jj, same, ti, tj


def _gdn_gates_kernel(ab_ref, alog_ref, dtb_ref, gt_ref, grow_ref, *, tb):
    ii, jj, same, ti, tj = _pair_masks()
    tri = ((same & (ti >= tj)).astype(bf16), (same & (ti <= tj)).astype(bf16))
    eye = (ii == jj).astype(bf16)
    for p in range(tb // LANES):
        rows = slice(p * LANES, (p + 1) * LANES)
        ab = ab_ref[rows, :]
        g = -jnp.exp(alog_ref[...]) * _softplus(ab + dtb_ref[...])
        fwd = _mm_sel(tri[0], g)
        bwd = _mm_sel(tri[1], g)
        gt = jnp.where(jj < GDN_HEADS, fwd, jnp.where(jj < N_DIRS * GDN_HEADS, bwd, _sigmoid(ab)))
        gt_ref[rows, :] = gt
        grow_ref[p] = _nt_sel(eye, gt)


def gdn_gates(pt, a_log, dt_bias, tb=512):
    b, t, _ = pt.shape
    tb = min(tb, t)
    npad = LANES - N_DIRS * GDN_HEADS
    alog = jnp.pad(a_log.reshape(1, -1), ((0, 0), (0, npad)))
    dtb = jnp.pad(dt_bias.reshape(1, -1), ((0, 0), (0, npad)))
    vec = pl.BlockSpec((1, LANES), lambda bi, i: (0, 0))
    return pl.pallas_call(
        functools.partial(_gdn_gates_kernel, tb=tb),
        out_shape=[jax.ShapeDtypeStruct((b, t, LANES), f32),
                   jax.ShapeDtypeStruct((b, t // LANES, LANES, LANES), f32)],
        grid=(b, t // tb),
        in_specs=[pl.BlockSpec((None, tb, LANES), lambda bi, i: (bi, i, 0)), vec, vec],
        out_specs=[pl.BlockSpec((None, tb, LANES), lambda bi, i: (bi, i, 0)),
                   pl.BlockSpec((None, tb // LANES, LANES, LANES), lambda bi, i: (bi, i, 0, 0))],
        compiler_params=_cparams(2),
        name="gdn_gates",
    )(pt, alog, dtb)


def _gdn_kernel(q_ref, k_ref, v_ref, z_ref, gt_ref, grow_ref, gain_ref, o_ref,
                u_s, w_s, qg_s, kg_s, a_s, el_s, oacc, *, t, seg):
    h = pl.program_id(1)
    npairs = t // LANES
    nseg = npairs // seg
    assert npairs % seg == 0 and npairs % 2 == 0
    ii, jj, same, ti, tj = _pair_masks()
    eye = (ii == jj).astype(f32)
    lvl = _level_masks(ti, tj, CHUNK)
    masks = ((same & (ti >= tj), same & (ti > tj)), (same & (ti <= tj), same & (ti < tj)))
    dirs = range(N_DIRS)
    n_hooks = 2 * (len(lvl) - 1) + 1
    n_stages = seg * 2 * 2

    def tile_of(d, sg, j):
        p = sg * seg + j
        return p if d == 0 else npairs - 1 - p

    def prep_seg(sg, slot, hook):
        probs = []
        for j in range(seg):
            for d in dirs:
                p = tile_of(d, sg, j)
                rows = pl.ds(pl.multiple_of(p * LANES, LANES), LANES)
                k = k_ref[rows, :]
                probs.append(dict(d=d, j=j, p=p, q=q_ref[rows, :], k=k, v=v_ref[rows, :], gt=gt_ref[rows, :],
                                  kb16=k.astype(bf16)))
        for pb in probs:
            pb["qk"] = _nt(pb["q"].astype(bf16), pb["kb16"])
            pb["kk"] = _nt(pb["kb16"], pb["kb16"])
        hook()
        for pb in probs:
            d = pb["d"]
            la = d * GDN_HEADS + h
            lb = N_DIRS * GDN_HEADS + la
            incl, strict = masks[d]
            gcol = jnp.sum(jnp.where(jj == la, pb["gt"], 0.0), axis=-1, keepdims=True)
            bcol = jnp.sum(jnp.where(jj == lb, pb["gt"], 0.0), axis=-1, keepdims=True)
            grow = grow_ref[pb["p"], pl.ds(la, 1), :]
            pb["dec"] = jnp.exp(jnp.where(incl, gcol - grow, NEG))
            pb["gcol"], pb["bcol"] = gcol, bcol
            pb["m"] = jnp.where(strict, pb["kk"] * bcol * pb["dec"], 0.0)
        tinvs = _tri_inverse([pb["m"] for pb in probs], eye, lvl, hook)
        uws = []
        for pb, tinv in zip(probs, tinvs):
            gcol, bcol = pb["gcol"], pb["bcol"]
            pb["eg"] = jnp.exp(gcol)
            rhs = jnp.concatenate([pb["v"] * bcol, pb["k"] * (bcol * pb["eg"])], axis=1).astype(bf16)
            t1, t2 = _split2(tinv)
            uws.append(_mm(jnp.concatenate([t1, t2], axis=1), jnp.concatenate([rhs, rhs], axis=0)))
        for pb, uw in zip(probs, uws):
            d, gcol = pb["d"], pb["gcol"]
            srows = slice(pb["j"] * LANES, (pb["j"] + 1) * LANES)
            ends = (CHUNK - 1, 2 * CHUNK - 1) if d == 0 else (0, CHUNK)
            gl = [gcol[e:e + 1, :] for e in ends]
            gtot = jnp.concatenate([jnp.broadcast_to(g, (CHUNK, 1)) for g in gl], axis=0)
            a2 = jnp.where(masks[d][0], pb["qk"] * pb["dec"], 0.0)
            u_s[slot, d, srows, :] = uw[:, :GDN_HD]
            w_s[slot, d, srows, :] = uw[:, GDN_HD:].astype(bf16)
            a_s[slot, d, srows, :] = jnp.concatenate([a2[:CHUNK, :CHUNK], a2[CHUNK:, CHUNK:]], axis=0).astype(bf16)
            qg_s[slot, d, srows, :] = (pb["q"] * pb["eg"]).astype(bf16)
            kg_s[slot, d, srows, :] = (pb["k"] * jnp.exp(gtot - gcol)).astype(bf16)
            el_s[slot, d, pb["j"] * 16:(pb["j"] + 1) * 16, :] = jnp.concatenate(
                [jnp.broadcast_to(jnp.exp(g), (8, LANES)) for g in gl], axis=0)

    def scan_seg(sg, slot, st):
        halves = ((0, 1), (1, 0))
        for j in range(seg):
            for step in range(2):
                sl = [slice(j * LANES + halves[d][step] * CHUNK, j * LANES + (halves[d][step] + 1) * CHUNK)
                      for d in dirs]
                sb = [st[d].astype(bf16) for d in dirs]
                ws = [_mm(w_s[slot, d, sl[d], :], sb[d]) for d in dirs]
                qs = [_mm(qg_s[slot, d, sl[d], :], sb[d]) for d in dirs]
                yield
                vnb = [(u_s[slot, d, sl[d], :] - ws[d]).astype(bf16) for d in dirs]
                av = [_mm(a_s[slot, d, sl[d], :], vnb[d]) for d in dirs]
                kv = [_tn(kg_s[slot, d, sl[d], :], vnb[d]) for d in dirs]
                yield
                for d in dirs:
                    e0 = j * 16 + halves[d][step] * 8
                    st[d] = st[d] * el_s[slot, d, e0:e0 + 8, :][0:1, :] + kv[d]
                    start = tile_of(d, sg, j) * LANES + halves[d][step] * CHUNK
                    oacc[pl.ds(pl.multiple_of(start, CHUNK), CHUNK), :] += qs[d] + av[d]

    oacc[...] = jnp.zeros(oacc.shape, f32)
    prep_seg(0, 0, _no_hook)

    def body(sg, states):
        st = list(states)
        gen = scan_seg(sg, sg % 2, st)
        prep_seg(sg + 1, (sg + 1) % 2, _stage_hook(gen, n_stages, n_hooks))
        for _ in gen:
            pass
        return tuple(st)

    s0 = jnp.zeros((GDN_HD, GDN_HD), f32)
    st = list(lax.fori_loop(0, nseg - 1, body, (s0, s0)))
    for _ in scan_seg(nseg - 1, (nseg - 1) % 2, st):
        pass

    tb = min(512, t)
    for ci in range(t // tb):
        rows = slice(ci * tb, (ci + 1) * tb)
        o = oacc[rows, :]
        z = z_ref[rows, :]
        ms = jnp.mean(o * o, axis=-1, keepdims=True)
        o_ref[rows, :] = (o * lax.rsqrt(ms + RMS_EPS) * gain_ref[...] * (z * _sigmoid(z))).astype(o_ref.dtype)


def gdn_branch(pm, pt, qkv, a_log, dt_bias, gain):
    b, t, _ = pm.shape
    assert t % LANES == 0
    seg = min(4, t // LANES // 2)
    rows = seg * LANES
    zcol0 = (3 * NA_WIDTH + 3 * GDN_WIDTH) // LANES
    gt, grow = gdn_gates(pt, a_log, dt_bias)
    hd = lambda off: (lambda bi, h: (bi, 0, off + h))
    return pl.pallas_call(
        functools.partial(_gdn_kernel, t=t, seg=seg),
        out_shape=jax.ShapeDtypeStruct((b, t, GDN_WIDTH), bf16),
        grid=(b, GDN_HEADS),
        in_specs=[pl.BlockSpec((None, t, GDN_HD), hd(0)),
                  pl.BlockSpec((None, t, GDN_HD), hd(GDN_HEADS)),
                  pl.BlockSpec((None, t, GDN_HD), hd(2 * GDN_HEADS)),
                  pl.BlockSpec((None, t, GDN_HD), hd(zcol0)),
                  pl.BlockSpec((None, t, LANES), lambda bi, h: (bi, 0, 0)),
                  pl.BlockSpec((None, t // LANES, LANES, LANES), lambda bi, h: (bi, 0, 0, 0)),
                  pl.BlockSpec((1, LANES), lambda bi, h: (0, 0))],
        out_specs=pl.BlockSpec((None, t, GDN_HD), hd(0)),
        scratch_shapes=[pltpu.VMEM((2, N_DIRS, rows, GDN_HD), f32),
                        pltpu.VMEM((2, N_DIRS, rows, GDN_HD), bf16),
                        pltpu.VMEM((2, N_DIRS, rows, GDN_HD), bf16),
                        pltpu.VMEM((2, N_DIRS, rows, GDN_HD), bf16),
                        pltpu.VMEM((2, N_DIRS, rows, CHUNK), bf16),
                        pltpu.VMEM((2, N_DIRS, seg * 16, LANES), f32),
                        pltpu.VMEM((t, GDN_HD), f32)],
        compiler_params=_cparams(2),
        name="gdn_scan",
    )(qkv, qkv, qkv, pm, gt, grow, gain.reshape(1, GDN_HD))


def _rwkv_prep_kernel(cur_a, prev_a, next_a, cur_b, prev_b, next_b, mu_ref, wup_ref, aup_ref, gup_ref, w0_ref,
                      a0_ref, rkv_ref, gate_ref, lw_ref, eta_ref, buf, *, tb):
    i = pl.program_id(1)
    nt = pl.num_programs(1)
    na = TAIL_A_COLS - GATE_COLS
    buf[8:8 + tb, 0:na] = cur_a[:, GATE_COLS:]
    buf[8:8 + tb, na:] = cur_b[:, 0:TAIL_B_USED]
    has_prev = i > 0
    has_next = i < nt - 1
    buf[7:8, 0:na] = jnp.where(has_prev, prev_a[7:8, GATE_COLS:], 0.0)
    buf[7:8, na:] = jnp.where(has_prev, prev_b[7:8, 0:TAIL_B_USED], 0.0)
    buf[8 + tb:9 + tb, 0:na] = jnp.where(has_next, next_a[0:1, GATE_COLS:], 0.0)
    buf[8 + tb:9 + tb, na:] = jnp.where(has_next, next_b[0:1, 0:TAIL_B_USED], 0.0)

    def mixed(c0, c1):
        x = buf[8:8 + tb, c0:c1]
        sh = 0.5 * (buf[7:7 + tb, c0:c1] + buf[9:9 + tb, c0:c1])
        return x + (sh - x) * mu_ref[:, c0:c1]

    for gi in range(3):
        c0 = gi * RWKV_WIDTH
        rkv_ref[:, c0:c0 + RWKV_WIDTH] = mixed(c0, c0 + RWKV_WIDTH)
    xwa = mixed(3 * RWKV_WIDTH, 3 * RWKV_WIDTH + LANES)
    xg = mixed(3 * RWKV_WIDTH + LANES, 3 * RWKV_WIDTH + 2 * LANES)
    gate_ref[...] = _mm(_sigmoid(xg).astype(bf16), gup_ref[...].astype(bf16))
    txw = jnp.tanh(xwa)
    for d in range(N_DIRS):
        log_w = -_softplus(-(w0_ref[d] + _mm_f32(txw, wup_ref[d]))) - 0.5
        lw_ref[d] = -jnp.exp(log_w)
        eta_ref[d] = _sigmoid(a0_ref[d] + _mm_f32(xwa, aup_ref[d]))


def rwkv_prep(pta, ptb, mu, w0, w_up, a0, a_up, g_up, tb=256):
    b, t, _ = pta.shape
    tb = min(tb, t)
    nb8 = tb // 8
    mu_p = mu.reshape(1, SHIFT_COLS)
    lora = w_up.shape[1]
    wup_p = jnp.pad(w_up, ((0, 0), (0, LANES - lora), (0, 0)))
    aup_p = jnp.pad(a_up, ((0, 0), (lora, LANES - lora - a_up.shape[1]), (0, 0)))
    out_sds = lambda *lead: jax.ShapeDtypeStruct(lead + (b, t, RWKV_WIDTH), f32)
    full = lambda shape: pl.BlockSpec(shape, lambda bi, i: (0,) * len(shape))

    def halo(cols, side):
        if side == 0:
            return pl.BlockSpec((None, tb, cols), lambda bi, i: (bi, i, 0))
        if side < 0:
            return pl.BlockSpec((None, 8, cols), lambda bi, i: (bi, jnp.maximum(i * nb8 - 1, 0), 0))
        return pl.BlockSpec((None, 8, cols), lambda bi, i: (bi, jnp.minimum((i + 1) * nb8, t // 8 - 1), 0))

    return pl.pallas_call(
        functools.partial(_rwkv_prep_kernel, tb=tb),
        out_shape=[jax.ShapeDtypeStruct((b, t, 3 * RWKV_WIDTH), f32), out_sds(), out_sds(N_DIRS), out_sds(N_DIRS)],
        grid=(b, t // tb),
        in_specs=[halo(TAIL_A_COLS, 0), halo(TAIL_A_COLS, -1), halo(TAIL_A_COLS, 1),
                  halo(TAIL_TN, 0), halo(TAIL_TN, -1), halo(TAIL_TN, 1),
                  full((1, SHIFT_COLS)), full((N_DIRS, LANES, RWKV_WIDTH)), full((N_DIRS, LANES, RWKV_WIDTH)),
                  full((LANES, RWKV_WIDTH)), full((N_DIRS, 1, RWKV_WIDTH)), full((N_DIRS, 1, RWKV_WIDTH))],
        out_specs=[pl.BlockSpec((None, tb, 3 * RWKV_WIDTH), lambda bi, i: (bi, i, 0)),
                   pl.BlockSpec((None, tb, RWKV_WIDTH), lambda bi, i: (bi, i, 0)),
                   pl.BlockSpec((N_DIRS, None, tb, RWKV_WIDTH), lambda bi, i: (0, bi, i, 0)),
                   pl.BlockSpec((N_DIRS, None, tb, RWKV_WIDTH), lambda bi, i: (0, bi, i, 0))],
        scratch_shapes=[pltpu.VMEM((tb + 16, SHIFT_COLS), f32)],
        compiler_params=_cparams(2),
        name="rwkv_prep",
    )(pta, pta, pta, ptb, ptb, ptb, mu_p, wup_p, aup_p, g_up, w0.reshape(N_DIRS, 1, -1), a0.reshape(N_DIRS, 1, -1))


def _stack2(x, m0, m1):
    return jnp.concatenate([x * m0, x * m1], axis=0)


def _rwkv_kernel(r_ref, k_ref, v_ref, gate_ref, lw0_ref, lw1_ref, eta0_ref, eta1_ref,
                 kk_ref, ka_ref, rk_ref, lnw_ref, lnb_ref, o_ref,
                 wt_s, uv_s, yv_s, arb_s, rt_s, bh_s, sk_s, eg_s, yacc, kk_s, gc_s, *, t, seg):
    n = t // CHUNK
    nseg = n // seg
    assert n % seg == 0 and n % 2 == 0 and nseg >= 2
    c2 = 2 * CHUNK
    lane = _iota2((1, LANES), 1)
    m0 = (lane < RWKV_HD).astype(f32)
    m1 = 1.0 - m0
    ii, jj, same, ti, tj = _pair_masks()
    bd = same.astype(bf16)
    eye = (ii == jj).astype(f32)
    lvl = _level_masks(ti, tj, CHUNK)
    masks = ((same & (ti >= tj), same & (ti > tj)), (same & (ti <= tj), same & (ti < tj)))
    lw_refs = (lw0_ref, lw1_ref)
    eta_refs = (eta0_ref, eta1_ref)
    kkw = kk_ref[...]
    kaw = ka_ref[...]

    def pre(p, carry):
        rows = pl.ds(pl.multiple_of(p * LANES, LANES), LANES)
        kkr = k_ref[rows, :] * kkw
        kk_s[rows, :] = kkr * lax.rsqrt(_mm_rsel(kkr * kkr, bd) + L2_EPS)
        for d in range(N_DIRS):
            gc_s[d, rows, :] = _mm_sel(masks[d][0].astype(bf16), lw_refs[d][rows, :])
        return carry

    lax.fori_loop(0, t // LANES, pre, 0, unroll=2)

    def prep_chunks(problems, slot, hook):
        pbs = []
        for d, c, j in problems:
            rows = pl.ds(pl.multiple_of(c * CHUNK, CHUNK), CHUNK)
            r = r_ref[rows, :]
            k = k_ref[rows, :]
            lw = lw_refs[d][rows, :]
            eta = eta_refs[d][rows, :]
            kk = kk_s[rows, :]
            kd = k * (1.0 + (eta - 1.0) * kaw)
            bvec = -kk * eta
            gcum = gc_s[d, rows, :]
            gtot = gcum[CHUNK - 1:CHUNK, :] if d == 0 else gcum[0:1, :]
            eneg = jnp.exp(-gcum)
            erem = jnp.exp(gtot - gcum)
            at2 = _stack2(kk * jnp.exp(gcum - lw), m0, m1)
            rt2 = _stack2(r * jnp.exp(gcum), m0, m1)
            kt2 = _stack2(kd * eneg, m0, m1)
            bt2 = _stack2(bvec * eneg, m0, m1)
            pbs.append(dict(d=d, slot=j, at2=at2, rt2=rt2, gtot=gtot,
                            v2b=_stack2(v_ref[rows, :], m0, m1).astype(bf16),
                            bh=_stack2(bvec * erem, m0, m1).astype(bf16),
                            kh=_stack2(kd * erem, m0, m1).astype(bf16),
                            lhs=jnp.concatenate([at2, rt2], axis=0).astype(bf16),
                            rhs=jnp.concatenate([kt2, bt2], axis=0).astype(bf16)))
        ggs = [_nt(pb["lhs"], pb["rhs"]) for pb in pbs]
        hook()
        for pb, gg in zip(pbs, ggs):
            incl, strict = masks[pb["d"]]
            pb["a_ak"] = jnp.where(strict, gg[0:c2, 0:c2], 0.0).astype(bf16)
            pb["a_ab"] = jnp.where(strict, gg[0:c2, c2:2 * c2], 0.0)
            pb["a_rk"] = jnp.where(incl, gg[c2:2 * c2, 0:c2], 0.0).astype(bf16)
            pb["a_rb"] = jnp.where(incl, gg[c2:2 * c2, c2:2 * c2], 0.0).astype(bf16)
        akv = [_mm(pb["a_ak"], pb["v2b"]) for pb in pbs]
        yv = [_mm(pb["a_rk"], pb["v2b"]) for pb in pbs]
        sk = [_tn(pb["v2b"], pb["kh"]) for pb in pbs]
        hook()
        tinvs = _tri_inverse([-pb["a_ab"] for pb in pbs], eye, lvl, hook)
        wus = []
        for pb, tinv, av in zip(pbs, tinvs, akv):
            rhs = jnp.concatenate([pb["at2"], av], axis=1).astype(bf16)
            t1, t2 = _split2(tinv)
            wus.append(_mm(jnp.concatenate([t1, t2], axis=1), jnp.concatenate([rhs, rhs], axis=0)))
        for i, pb in enumerate(pbs):
            d, j = pb["d"], pb["slot"]
            srows = slice(j * c2, (j + 1) * c2)
            wt_s[slot, d, srows, :] = wus[i][:, :LANES].astype(bf16)
            uv_s[slot, d, srows, :] = wus[i][:, LANES:]
            yv_s[slot, d, srows, :] = yv[i]
            arb_s[slot, d, srows, :] = pb["a_rb"]
            rt_s[slot, d, srows, :] = pb["rt2"].astype(bf16)
            bh_s[slot, d, srows, :] = pb["bh"]
            sk_s[slot, d, srows, :] = sk[i]
            eg_s[slot, d, j * 8:(j + 1) * 8, :] = jnp.broadcast_to(jnp.exp(pb["gtot"]), (8, LANES))

    n_hooks = 2 * (len(lvl) - 1) + 2
    n_stages = seg * 2
    dirs = range(N_DIRS)

    def chunk_of(d, sg, j):
        c = sg * seg + j
        return c if d == 0 else n - 1 - c

    def prep_seg(sg, slot, hook):
        prep_chunks([(d, chunk_of(d, sg, j), j) for j in range(seg) for d in dirs], slot, hook)

    def scan_seg(sg, slot, st):
        for j in range(seg):
            srows = slice(j * c2, (j + 1) * c2)
            sb = [st[d].astype(bf16) for d in dirs]
            ws = [_nt(wt_s[slot, d, srows, :], sb[d]) for d in dirs]
            rs = [_nt(rt_s[slot, d, srows, :], sb[d]) for d in dirs]
            yield
            u2b = [(uv_s[slot, d, srows, :] + ws[d]).astype(bf16) for d in dirs]
            au = [_mm(arb_s[slot, d, srows, :], u2b[d]) for d in dirs]
            ub = [_tn(u2b[d], bh_s[slot, d, srows, :]) for d in dirs]
            yield
            for d in dirs:
                rows = pl.ds(pl.multiple_of(chunk_of(d, sg, j) * CHUNK, CHUNK), CHUNK)
                y2 = yv_s[slot, d, srows, :] + rs[d] + au[d]
                st[d] = st[d] * eg_s[slot, d, j * 8:(j + 1) * 8, :][0:1, :] + sk_s[slot, d, srows, :] + ub[d]
                yacc[rows, :] += y2[0:CHUNK, :] + y2[CHUNK:c2, :]

    yacc[...] = jnp.zeros(yacc.shape, f32)
    prep_seg(0, 0, _no_hook)

    def body(sg, states):
        st = list(states)
        gen = scan_seg(sg, sg % 2, st)
        prep_seg(sg + 1, (sg + 1) % 2, _stage_hook(gen, n_stages, n_hooks))
        for _ in gen:
            pass
        return tuple(st)

    s0 = jnp.zeros((LANES, LANES), f32)
    st = list(lax.fori_loop(0, nseg - 1, body, (s0, s0)))
    for _ in scan_seg(nseg - 1, (nseg - 1) % 2, st):
        pass

    tb = min(512, t)
    inv_hd = 1.0 / RWKV_HD
    for ci in range(t // tb):
        rows = slice(ci * tb, (ci + 1) * tb)
        y = yacc[rows, :]
        mean = _mm_rsel(y, bd) * inv_hd
        dlt = y - mean
        var = _mm_rsel(dlt * dlt, bd) * inv_hd
        yn = dlt * lax.rsqrt(var + RWKV_GN_EPS) * lnw_ref[...] + lnb_ref[...]
        k = k_ref[rows, :]
        kd_sum = k * (2.0 + (eta0_ref[rows, :] + eta1_ref[rows, :] - 2.0) * kaw)
        bonus = _mm_rsel(r_ref[rows, :] * kd_sum * rk_ref[...], bd) * v_ref[rows, :]
        o_ref[rows, :] = ((yn + bonus) * gate_ref[rows, :]).astype(o_ref.dtype)


def rwkv_branch(rkv, gate, lw, eta, k_k, k_a, r_k, ln_w, ln_b, seg=4):
    b, t, _ = gate.shape
    n = t // CHUNK
    seg = min(seg, n // 2)
    c2 = 2 * CHUNK
    blk = lambda off: pl.BlockSpec((None, t, LANES), lambda bi, p: (bi, 0, off + p))
    dblk = lambda d: pl.BlockSpec((None, None, t, LANES), lambda bi, p: (d, bi, 0, p))
    vec = pl.BlockSpec((1, LANES), lambda bi, p: (0, p))
    row = lambda x: x.reshape(1, RWKV_WIDTH)
    seg_f32 = pltpu.VMEM((2, N_DIRS, seg * c2, LANES), f32)
    seg_b16 = pltpu.VMEM((2, N_DIRS, seg * c2, LANES), bf16)
    return pl.pallas_call(
        functools.partial(_rwkv_kernel, t=t, seg=seg),
        out_shape=jax.ShapeDtypeStruct((b, t, RWKV_WIDTH), bf16),
        grid=(b, RWKV_PAIRS),
        in_specs=[blk(0), blk(RWKV_PAIRS), blk(2 * RWKV_PAIRS), blk(0),
                  dblk(0), dblk(1), dblk(0), dblk(1), vec, vec, vec, vec, vec],
        out_specs=blk(0),
        scratch_shapes=[seg_b16,
                        seg_f32,
                        seg_f32,
                        seg_b16,
                        seg_b16,
                        seg_b16,
                        seg_f32,
                        pltpu.VMEM((2, N_DIRS, seg * 8, LANES), f32),
                        pltpu.VMEM((t, LANES), f32),
                        pltpu.VMEM((t, LANES), f32),
                        pltpu.VMEM((N_DIRS, t, LANES), f32)],
        compiler_params=_cparams(2),
        name="rwkv_scan",
    )(rkv, rkv, rkv, gate, lw, lw, eta, eta, row(k_k), row(k_a), row(r_k), row(ln_w), row(ln_b))


def _tail_b_weights(w_in_t, layer):
    last = w_in_t[layer, MAIN_COLS + TAIL_A_COLS:, :]
    assert last.shape[0] == TAIL_B_USED
    return jnp.pad(last, ((0, TAIL_TN - TAIL_B_USED), (0, 0)))[None]


def kernel(x, norm_mix, w_in, w_out, na_rpb, na_gain, gdn_conv_w, gdn_a_log, gdn_dt_bias, gdn_gain, rwkv_mu, rwkv_w0, rwkv_w_up, rwkv_a0, rwkv_a_up, rwkv_g_up, rwkv_k_k, rwkv_k_a, rwkv_r_k, rwkv_ln_w, rwkv_ln_b, norm_ffn, ffn_w_gate, ffn_w_up, ffn_w_down, moe_router, moe_w_gate, moe_w_up, moe_w_down, norm_final):
    b, t, d = x.shape
    n = b * t
    depth = w_in.shape[0]
    h = x.reshape(n, d)
    w_in_t = jnp.swapaxes(w_in, 1, 2)
    for layer in range(depth):
        u = rmsnorm(h, norm_mix[layer], bf16)
        pm = proj(u, w_in_t, layer, MAIN_COLS, tn=512).reshape(b, t, MAIN_COLS)
        pta = proj(u, w_in_t, layer, TAIL_A_COLS, tn=TAIL_TN,
                   col_block0=MAIN_COLS // TAIL_TN).reshape(b, t, TAIL_A_COLS)
        ptb = proj(u, _tail_b_weights(w_in_t, layer), 0, TAIL_TN, tn=TAIL_TN).reshape(b, t, TAIL_TN)

        y_a = na_branch(pm, na_rpb[layer], na_gain[layer])
        qkv = gdn_prep(pm, gdn_conv_w, layer)
        y_b = gdn_branch(pm, pta, qkv, gdn_a_log[layer], gdn_dt_bias[layer], gdn_gain[layer])
        rkv, gate, lw, eta = rwkv_prep(pta, ptb, rwkv_mu[layer], rwkv_w0[layer], rwkv_w_up[layer],
                                       rwkv_a0[layer], rwkv_a_up[layer], rwkv_g_up[layer])
        y_c = rwkv_branch(rkv, gate, lw, eta, rwkv_k_k[layer], rwkv_k_a[layer], rwkv_r_k[layer],
                          rwkv_ln_w[layer], rwkv_ln_b[layer])
        h = outproj(y_a.reshape(n, -1), y_b.reshape(n, -1), y_c.reshape(n, -1), w_out, layer, h)

        i = layer // 2
        if layer % 2 == 0:
            u = rmsnorm(h, norm_ffn[layer], bf16)
            act = gateup(u, ffn_w_gate, ffn_w_up, i)
            half = ffn_w_down.shape[1] // 2
            assert half % LANES == 0
            for kb in range(2):
                h = down(act, ffn_w_down, i, h, kb, half)
        else:
            closing = layer == depth - 1
            h = moe_ffn(h, norm_ffn[layer], moe_router[i], moe_w_gate, moe_w_up, moe_w_down, i, norm_final, closing)
            if closing:
                return h.reshape(b, t, d)
    return rmsnorm(h, norm_final, f32).reshape(b, t, d)
```

```python
import functools

import jax
import jax.numpy as jnp
import numpy as np
from jax import lax
from jax.experimental import pallas as pl
from jax.experimental.pallas import tpu as pltpu

f32 = jnp.float32
bf16 = jnp.bfloat16
HI = lax.Precision.HIGHEST

LANES = 128
VMEM_LIMIT_BYTES = 56 * 1024 * 1024

D_MODEL = 4096
GRID_W = 64
NA_HEADS, NA_HD = 12, 128
NA_WIDTH = NA_HEADS * NA_HD
NA_WIN_ROWS, NA_WIN_COLS = 8, 16
GDN_HEADS, GDN_HD = 12, 128
GDN_WIDTH = GDN_HEADS * GDN_HD
GDN_CONV = 5
CHUNK = 64
RWKV_HEADS, RWKV_HD = 16, 64
RWKV_WIDTH = RWKV_HEADS * RWKV_HD
RWKV_PAIRS = RWKV_HEADS // 2
N_DIRS = 2
MAIN_COLS = 3 * NA_WIDTH + 4 * GDN_WIDTH
RWKV_COLS = 3 * RWKV_WIDTH + 64 + 64 + 128
SHIFT_COLS = RWKV_COLS
assert SHIFT_COLS % LANES == 0
GATE_COLS = 2 * N_DIRS * GDN_HEADS
TAIL_TN = 512
TAIL_A_COLS = 3072
TAIL_B_USED = GATE_COLS + RWKV_COLS - TAIL_A_COLS
assert MAIN_COLS % TAIL_TN == 0 and TAIL_A_COLS % TAIL_TN == 0 and 0 < TAIL_B_USED <= TAIL_TN
MOE_EXPERTS = 8
RMS_EPS = 1e-6
L2_EPS = 1e-6
RWKV_GN_EPS = 64e-5
NEG = -1e30


def _cparams(n_axes):
    return pltpu.CompilerParams(dimension_semantics=("arbitrary",) * n_axes,
                                vmem_limit_bytes=VMEM_LIMIT_BYTES)


def _nt(a, b, precision=None):
    return lax.dot_general(a, b, (((1,), (1,)), ((), ())), precision=precision,
                           preferred_element_type=f32)


def _tn(a, b, precision=None):
    return lax.dot_general(a, b, (((0,), (0,)), ((), ())), precision=precision,
                           preferred_element_type=f32)


def _mm(a, b, precision=None):
    return jnp.dot(a, b, precision=precision, preferred_element_type=f32)


def _sigmoid(x):
    return 1.0 / (1.0 + jnp.exp(-x))


def _softplus(x):
    return jnp.maximum(x, 0.0) + jnp.log(1.0 + jnp.exp(-jnp.abs(x)))


def _iota2(shape, axis):
    return lax.broadcasted_iota(jnp.int32, shape, axis)


def _rmsnorm_kernel(x_ref, g_ref, o_ref):
    x = x_ref[...]
    ms = jnp.mean(x * x, axis=-1, keepdims=True)
    o_ref[...] = (x * lax.rsqrt(ms + RMS_EPS) * g_ref[...]).astype(o_ref.dtype)


def rmsnorm(x, gain, out_dtype):
    n, d = x.shape
    tm = min(256, n)
    return pl.pallas_call(
        _rmsnorm_kernel,
        out_shape=jax.ShapeDtypeStruct((n, d), out_dtype),
        grid=(n // tm,),
        in_specs=[pl.BlockSpec((tm, d), lambda i: (i, 0)), pl.BlockSpec((1, d), lambda i: (0, 0))],
        out_specs=pl.BlockSpec((tm, d), lambda i: (i, 0)),
        compiler_params=_cparams(1),
        name="rmsnorm",
    )(x, gain.reshape(1, d))


def _rmsnorm_router_kernel(x_ref, g_ref, r_ref, o_ref, e_ref, w_ref):
    x = x_ref[...]
    ms = jnp.mean(x * x, axis=-1, keepdims=True)
    u = x * lax.rsqrt(ms + RMS_EPS) * g_ref[...]
    o_ref[...] = _pack_bf16_halves(u)
    logits = _mm(u, r_ref[...], HI)
    lane = _iota2(logits.shape, 1)
    lg = jnp.where(lane < MOE_EXPERTS, logits, NEG)
    m1 = jnp.max(lg, axis=-1, keepdims=True)
    i1 = jnp.min(jnp.where(lg == m1, lane, LANES), axis=-1, keepdims=True)
    lg2 = jnp.where(lane == i1, NEG, lg)
    m2 = jnp.max(lg2, axis=-1, keepdims=True)
    i2 = jnp.min(jnp.where(lg2 == m2, lane, LANES), axis=-1, keepdims=True)
    e2 = jnp.exp(m2 - m1)
    w1 = 1.0 / (1.0 + e2)
    w2 = e2 / (1.0 + e2)
    e_ref[...] = jnp.where(lane == 0, i1, jnp.where(lane == 1, i2, 0))
    w_ref[...] = jnp.where(lane == 0, w1, jnp.where(lane == 1, w2, 0.0))


_HI16 = 0xFFFF0000


def _pack_bf16_halves(u):
    half = u.shape[1] // 2
    bits = lambda v: lax.bitcast_convert_type(v.astype(bf16).astype(f32), jnp.uint32)
    return (bits(u[:, half:]) & jnp.uint32(_HI16)) | (bits(u[:, :half]) >> 16)


def _unpack_bf16_halves(p):
    lo = lax.bitcast_convert_type(p << 16, f32)
    hi = lax.bitcast_convert_type(p & jnp.uint32(_HI16), f32)
    return lo.astype(bf16), hi.astype(bf16)


def rmsnorm_router(x, gain, router):
    n, d = x.shape
    tm = min(256, n)
    r_pad = jnp.pad(router, ((0, 0), (0, LANES - router.shape[1])))
    return pl.pallas_call(
        _rmsnorm_router_kernel,
        out_shape=[jax.ShapeDtypeStruct((n, d // 2), jnp.uint32), jax.ShapeDtypeStruct((n, LANES), jnp.int32),
                   jax.ShapeDtypeStruct((n, LANES), f32)],
        grid=(n // tm,),
        in_specs=[pl.BlockSpec((tm, d), lambda i: (i, 0)), pl.BlockSpec((1, d), lambda i: (0, 0)),
                  pl.BlockSpec((d, LANES), lambda i: (0, 0))],
        out_specs=[pl.BlockSpec((tm, d // 2), lambda i: (i, 0)), pl.BlockSpec((tm, LANES), lambda i: (i, 0)),
                   pl.BlockSpec((tm, LANES), lambda i: (i, 0))],
        compiler_params=_cparams(1),
        name="rmsnorm_router",
    )(x, gain.reshape(1, d), r_pad)


def _moe_rank_kernel(e_ref, rank_ref, cnt_ref, carry, *, tb):
    @pl.when(pl.program_id(0) == 0)
    def _():
        carry[...] = jnp.zeros(carry.shape, f32)

    ids = e_ref[...]
    lane = _iota2(ids.shape, 1)
    oh1 = lane == ids[:, 0:1]
    oh2 = lane == ids[:, 1:2]
    f1 = jnp.where(oh1, 1.0, 0.0)
    f2 = jnp.where(oh2, 1.0, 0.0)
    earlier = jnp.where(_iota2((tb, tb), 0) > _iota2((tb, tb), 1), 1.0, 0.0).astype(bf16)
    p1 = _mm(earlier, f1.astype(bf16))
    p2 = _mm(earlier, f2.astype(bf16))
    c1 = jnp.sum(f1, axis=0, keepdims=True)
    c2 = jnp.sum(f2, axis=0, keepdims=True)
    base = carry[0:1, :]
    r1 = jnp.sum(jnp.where(oh1, p1 + base, 0.0), axis=-1, keepdims=True)
    r2 = jnp.sum(jnp.where(oh2, p2 + (base + c1), 0.0), axis=-1, keepdims=True)
    rank_ref[...] = jnp.where(lane == 0, r1, jnp.where(lane == 1, r2, 0.0))
    carry[...] = jnp.broadcast_to(base + c1 + c2, carry.shape)
    cnt_ref[...] = carry[...]


def moe_rank(ids, tb=256):
    n = ids.shape[0]
    tb = min(tb, n)
    return pl.pallas_call(
        functools.partial(_moe_rank_kernel, tb=tb),
        out_shape=[jax.ShapeDtypeStruct((n, LANES), f32), jax.ShapeDtypeStruct((8, LANES), f32)],
        grid=(n // tb,),
        in_specs=[pl.BlockSpec((tb, LANES), lambda i: (i, 0))],
        out_specs=[pl.BlockSpec((tb, LANES), lambda i: (i, 0)), pl.BlockSpec((8, LANES), lambda i: (0, 0))],
        scratch_shapes=[pltpu.VMEM((8, LANES), f32)],
        compiler_params=_cparams(1),
        name="moe_rank",
    )(ids)


def _gather_rows_kernel(idx_ref, src_ref, o_ref, buf, sem, *, tm):
    base = pl.program_id(0) * tm

    def row_copy(r, row):
        return pltpu.make_async_copy(src_ref.at[pl.ds(row, 1), :], buf.at[pl.ds(r, 1), :], sem)

    def start(r, c):
        row_copy(r, idx_ref[base + r]).start()
        return c

    def wait(r, c):
        row_copy(r, 0).wait()
        return c

    lax.fori_loop(0, tm, start, 0, unroll=8)
    lax.fori_loop(0, tm, wait, 0, unroll=8)
    half = buf.shape[1]
    lo, hi = _unpack_bf16_halves(buf[...])
    o_ref[:, :half] = lo
    o_ref[:, half:] = hi


def gather_rows(src, idx, tm=256):
    m = idx.shape[0]
    half = src.shape[1]
    d = 2 * half
    tm = min(tm, m)
    assert m % tm == 0
    return pl.pallas_call(
        functools.partial(_gather_rows_kernel, tm=tm),
        out_shape=jax.ShapeDtypeStruct((m, d), bf16),
        grid_spec=pltpu.PrefetchScalarGridSpec(
            num_scalar_prefetch=1, grid=(m // tm,),
            in_specs=[pl.BlockSpec(memory_space=pl.ANY)],
            out_specs=pl.BlockSpec((tm, d), lambda i, idx_ref: (i, 0)),
            scratch_shapes=[pltpu.VMEM((tm, half), jnp.uint32), pltpu.SemaphoreType.DMA(())]),
        compiler_params=_cparams(1),
        name="gather_rows",
    )(idx, src)


def _moe_combine_kernel(d1_ref, d2_ref, y_ref, h_ref, w_ref, g_ref, o_ref, buf, sems, *, tm, final_norm):
    base = pl.program_id(0) * tm
    dests = (d1_ref, d2_ref)

    def row_copy(k, r, row):
        return pltpu.make_async_copy(y_ref.at[pl.ds(row, 1), :], buf.at[k, pl.ds(r, 1), :], sems.at[k])

    def start(r, c):
        for k in range(2):
            row_copy(k, r, dests[k][base + r]).start(priority=k)
        return c

    def wait(r, c):
        for k in range(2):
            row_copy(k, r, 0).wait()
        return c

    lax.fori_loop(0, tm, start, 0, unroll=8)
    lax.fori_loop(0, tm, wait, 0, unroll=8)
    w = w_ref[...]
    o = h_ref[...] + w[:, 0:1] * buf[0] + w[:, 1:2] * buf[1]
    if final_norm:
        o = o * lax.rsqrt(jnp.mean(o * o, axis=-1, keepdims=True) + RMS_EPS) * g_ref[...]
    o_ref[...] = o


def moe_combine(y, d1, d2, h, wts, gain, final_norm, tm=256):
    n, d = h.shape
    tm = min(tm, n)
    return pl.pallas_call(
        functools.partial(_moe_combine_kernel, tm=tm, final_norm=final_norm),
        out_shape=jax.ShapeDtypeStruct((n, d), f32),
        grid_spec=pltpu.PrefetchScalarGridSpec(
            num_scalar_prefetch=2, grid=(n // tm,),
            in_specs=[pl.BlockSpec(memory_space=pl.ANY),
                      pl.BlockSpec((tm, d), lambda i, a, b: (i, 0)),
                      pl.BlockSpec((tm, LANES), lambda i, a, b: (i, 0)),
                      pl.BlockSpec((1, d), lambda i, a, b: (0, 0))],
            out_specs=pl.BlockSpec((tm, d), lambda i, a, b: (i, 0)),
            scratch_shapes=[pltpu.VMEM((2, tm, d), f32), pltpu.SemaphoreType.DMA((2,))]),
        compiler_params=_cparams(1),
        name="moe_combine",
    )(d1, d2, y, h, wts, gain.reshape(1, d))


def _proj_t_kernel(x_ref, w_ref, o_ref, wb_ref):
    @pl.when(pl.program_id(1) == 0)
    def _():
        wb_ref[...] = w_ref[...].T.astype(bf16)

    o_ref[...] = _mm(x_ref[...], wb_ref[...]).astype(o_ref.dtype)


def proj(x, wt, w_index, n_cols, tn, out_dtype=f32, tm=1024, col_block0=0):
    n, k = x.shape
    tm = min(tm, n)
    assert n_cols % tn == 0 and n % tm == 0 and (col_block0 * tn + n_cols) <= wt.shape[1]
    return pl.pallas_call(
        _proj_t_kernel,
        out_shape=jax.ShapeDtypeStruct((n, n_cols), out_dtype),
        grid=(n_cols // tn, n // tm),
        in_specs=[pl.BlockSpec((tm, k), lambda j, i: (i, 0)),
                  pl.BlockSpec((None, tn, k), lambda j, i: (w_index, col_block0 + j, 0))],
        out_specs=pl.BlockSpec((tm, tn), lambda j, i: (i, j)),
        scratch_shapes=[pltpu.VMEM((k, tn), bf16)],
        compiler_params=_cparams(2),
        name="proj",
    )(x, wt)


def _outproj_kernel(xa_ref, xb_ref, xc_ref, w_ref, res_ref, o_ref, wb_ref):
    @pl.when(pl.program_id(1) == 0)
    def _():
        wb_ref[...] = w_ref[...].astype(bf16)

    acc = _mm(xa_ref[...], wb_ref[0:NA_WIDTH, :])
    acc += _mm(xb_ref[...], wb_ref[NA_WIDTH:NA_WIDTH + GDN_WIDTH, :])
    acc += _mm(xc_ref[...], wb_ref[NA_WIDTH + GDN_WIDTH:, :])
    o_ref[...] = res_ref[...] + acc


def outproj(ya, yb, yc, w, layer, res, tn=512, tm=1024):
    n = ya.shape[0]
    d = w.shape[-1]
    tm = min(tm, n)
    return pl.pallas_call(
        _outproj_kernel,
        out_shape=jax.ShapeDtypeStruct((n, d), f32),
        grid=(d // tn, n // tm),
        in_specs=[pl.BlockSpec((tm, NA_WIDTH), lambda j, i: (i, 0)),
                  pl.BlockSpec((tm, GDN_WIDTH), lambda j, i: (i, 0)),
                  pl.BlockSpec((tm, RWKV_WIDTH), lambda j, i: (i, 0)),
                  pl.BlockSpec((None, D_MODEL, tn), lambda j, i: (layer, 0, j)),
                  pl.BlockSpec((tm, tn), lambda j, i: (i, j))],
        out_specs=pl.BlockSpec((tm, tn), lambda j, i: (i, j)),
        scratch_shapes=[pltpu.VMEM((D_MODEL, tn), bf16)],
        compiler_params=_cparams(2),
        name="outproj",
    )(ya, yb, yc, w, res)


def _gateup_kernel(x_ref, wg_ref, wu_ref, o_ref, wgb_ref, wub_ref):
    @pl.when(pl.program_id(1) == 0)
    def _():
        wgb_ref[...] = wg_ref[...].astype(bf16)
        wub_ref[...] = wu_ref[...].astype(bf16)

    x = x_ref[...]
    g = _mm(x, wgb_ref[...])
    u = _mm(x, wub_ref[...])
    o_ref[...] = (g * _sigmoid(g) * u).astype(o_ref.dtype)


def gateup(x, w_gate, w_up, w_index, tn=256, tm=1024):
    n, k = x.shape
    ff = w_gate.shape[-1]
    tm = min(tm, n)
    return pl.pallas_call(
        _gateup_kernel,
        out_shape=jax.ShapeDtypeStruct((n, ff), bf16),
        grid=(ff // tn, n // tm),
        in_specs=[pl.BlockSpec((tm, k), lambda j, i: (i, 0)),
                  pl.BlockSpec((None, k, tn), lambda j, i: (w_index, 0, j)),
                  pl.BlockSpec((None, k, tn), lambda j, i: (w_index, 0, j))],
        out_specs=pl.BlockSpec((tm, tn), lambda j, i: (i, j)),
        scratch_shapes=[pltpu.VMEM((k, tn), bf16), pltpu.VMEM((k, tn), bf16)],
        compiler_params=_cparams(2),
        name="gateup",
    )(x, w_gate, w_up)


def _tile_changed(te_ref, i):
    return (i == 0) | (te_ref[i] != te_ref[jnp.maximum(i - 1, 0)])


def _moe_gateup_kernel(te_ref, nu_ref, x_ref, wg_ref, wu_ref, o_ref, wgb_ref, wub_ref):
    i = pl.program_id(1)
    used = i < nu_ref[0]

    @pl.when(used & _tile_changed(te_ref, i))
    def _():
        wgb_ref[...] = wg_ref[...].astype(bf16)
        wub_ref[...] = wu_ref[...].astype(bf16)

    @pl.when(used)
    def _():
        x = x_ref[...]
        g = _mm(x, wgb_ref[...])
        u = _mm(x, wub_ref[...])
        o_ref[...] = (g * _sigmoid(g) * u).astype(o_ref.dtype)

    @pl.when(jnp.logical_not(used))
    def _():
        o_ref[...] = jnp.zeros(o_ref.shape, o_ref.dtype)


def _moe_down_kernel(te_ref, nu_ref, x_ref, w_ref, o_ref, wb_ref):
    i = pl.program_id(1)
    used = i < nu_ref[0]

    @pl.when(used & _tile_changed(te_ref, i))
    def _():
        wb_ref[...] = w_ref[...].astype(bf16)

    @pl.when(used)
    def _():
        o_ref[...] = _mm(x_ref[...], wb_ref[...])

    @pl.when(jnp.logical_not(used))
    def _():
        o_ref[...] = jnp.zeros(o_ref.shape, o_ref.dtype)


def _grouped_specs(tm, k, tn, w_index, n_w):
    last = lambda i, nu: jnp.minimum(i, nu[0] - 1)
    x_spec = pl.BlockSpec((tm, k), lambda j, i, te, nu: (last(i, nu), 0))
    w_spec = pl.BlockSpec((None, None, k, tn), lambda j, i, te, nu: (w_index, te[last(i, nu)], 0, j))
    o_spec = pl.BlockSpec((tm, tn), lambda j, i, te, nu: (i, j))
    return [x_spec] + [w_spec] * n_w, o_spec


def moe_gateup(x, tile_expert, n_used, w_gate, w_up, w_index, tm, tn=512):
    m, k = x.shape
    ff = w_gate.shape[-1]
    in_specs, out_spec = _grouped_specs(tm, k, tn, w_index, 2)
    return pl.pallas_call(
        _moe_gateup_kernel,
        out_shape=jax.ShapeDtypeStruct((m, ff), bf16),
        grid_spec=pltpu.PrefetchScalarGridSpec(
            num_scalar_prefetch=2, grid=(ff // tn, m // tm), in_specs=in_specs, out_specs=out_spec,
            scratch_shapes=[pltpu.VMEM((k, tn), bf16), pltpu.VMEM((k, tn), bf16)]),
        compiler_params=_cparams(2),
        name="moe_gateup",
    )(tile_expert, n_used, x, w_gate, w_up)


def moe_down(x, tile_expert, n_used, w_down, w_index, tm, tn=512):
    m, k = x.shape
    d = w_down.shape[-1]
    in_specs, out_spec = _grouped_specs(tm, k, tn, w_index, 1)
    return pl.pallas_call(
        _moe_down_kernel,
        out_shape=jax.ShapeDtypeStruct((m, d), f32),
        grid_spec=pltpu.PrefetchScalarGridSpec(
            num_scalar_prefetch=2, grid=(d // tn, m // tm), in_specs=in_specs, out_specs=out_spec,
            scratch_shapes=[pltpu.VMEM((k, tn), bf16)]),
        compiler_params=_cparams(2),
        name="moe_down",
    )(tile_expert, n_used, x, w_down)


def moe_ffn(h, gain, router, w_gate, w_up, w_down, w_index, closing_gain, closing_norm, tm=512):
    n, d = h.shape
    ne = w_gate.shape[1]
    tm = min(tm, n)
    u, ids, wts = rmsnorm_router(h, gain, router)
    rank, counts = moe_rank(ids)
    counts = counts[0, :ne].astype(jnp.int32)
    padded = (counts + tm - 1) // tm * tm
    ends = jnp.cumsum(padded)
    e2 = ids[:, :2]
    dest = (ends - padded)[e2] + rank[:, :2].astype(jnp.int32)
    n_tiles = 2 * n // tm + ne
    tile_expert = jnp.minimum(jnp.searchsorted(ends, jnp.arange(n_tiles) * tm, side="right"), ne - 1)
    n_used = (ends[-1] // tm).reshape(1)
    token = jnp.broadcast_to(jnp.arange(n, dtype=jnp.int32)[:, None], (n, 2))
    src = jnp.zeros((n_tiles * tm,), jnp.int32).at[dest.reshape(-1)].set(token.reshape(-1))
    xs = gather_rows(u, src)
    act = moe_gateup(xs, tile_expert.astype(jnp.int32), n_used.astype(jnp.int32), w_gate, w_up, w_index, tm)
    ys = moe_down(act, tile_expert.astype(jnp.int32), n_used.astype(jnp.int32), w_down, w_index, tm)
    return moe_combine(ys, dest[:, 0], dest[:, 1], h, wts, closing_gain, closing_norm)


def _down_kernel(x_ref, w_ref, res_ref, o_ref, wb_ref):
    @pl.when(pl.program_id(1) == 0)
    def _():
        wb_ref[...] = w_ref[...].astype(bf16)

    o_ref[...] = res_ref[...] + _mm(x_ref[...], wb_ref[...])


def down(x, w, w_index, res, k_block, tk, tn=512, tm=512):
    n = x.shape[0]
    d = w.shape[-1]
    tm = min(tm, n)
    return pl.pallas_call(
        _down_kernel,
        out_shape=jax.ShapeDtypeStruct((n, d), f32),
        grid=(d // tn, n // tm),
        in_specs=[pl.BlockSpec((tm, tk), lambda j, i: (i, k_block)),
                  pl.BlockSpec((None, tk, tn), lambda j, i: (w_index, k_block, j)),
                  pl.BlockSpec((tm, tn), lambda j, i: (i, j))],
        out_specs=pl.BlockSpec((tm, tn), lambda j, i: (i, j)),
        scratch_shapes=[pltpu.VMEM((tk, tn), bf16)],
        compiler_params=_cparams(2),
        name="down",
    )(x, w, res)


def _na_bias_table(rpb):
    c = np.arange(GRID_W)[:, None]
    kc = np.arange(GRID_W)[None, :]
    cs = np.clip(c - NA_WIN_COLS // 2, 0, GRID_W - NA_WIN_COLS)
    valid = (kc >= cs) & (kc < cs + NA_WIN_COLS)
    dc = kc - c + (NA_WIN_COLS - 1)
    onehot = (valid[None] & (dc[None] == np.arange(2 * NA_WIN_COLS - 1)[:, None, None])).astype(np.float32)
    cols = jnp.einsum("hrm,mck->hrck", rpb.astype(f32), jnp.asarray(onehot), precision=HI)
    lo = NA_WIN_ROWS - 1
    tab = jnp.stack([cols[:, lo - off:lo - off + NA_WIN_ROWS] for off in range(NA_WIN_ROWS)], axis=1)
    tab = tab.transpose(0, 1, 3, 2, 4) + jnp.asarray(np.where(valid, 0.0, NEG).astype(np.float32))[:, None, :]
    return tab.reshape(rpb.shape[0], NA_WIN_ROWS, GRID_W, NA_WIN_ROWS * GRID_W)


def _na_kernel(q_ref, k_ref, v_ref, bias_ref, gain_ref, o_ref, kb_ref, vb_ref, *, rows, rows_per_step, lockstep):
    g = pl.program_id(2)

    @pl.when(g == 0)
    def _():
        kb_ref[...] = k_ref[...].astype(bf16)
        vb_ref[...] = v_ref[...].astype(bf16)

    scale = NA_HD ** -0.5
    win = NA_WIN_ROWS * GRID_W

    def body(it, carry):
        qrows, krows, ss = [], [], []
        for li in range(lockstep):
            i = it * lockstep + li
            r = g * rows_per_step + i
            sr = jnp.clip(r - NA_WIN_ROWS // 2, 0, rows - NA_WIN_ROWS)
            qrows.append(pl.ds(pl.multiple_of(i * GRID_W, GRID_W), GRID_W))
            krows.append(pl.ds(pl.multiple_of(sr * GRID_W, GRID_W), win))
            q = q_ref[qrows[li], :].astype(bf16)
            ss.append(_nt(q, kb_ref[krows[li], :]) * scale + bias_ref[r - sr])
        ps, ls = [], []
        for s in ss:
            p = jnp.exp(s - jnp.max(s, axis=-1, keepdims=True))
            ls.append(jnp.sum(p, axis=-1, keepdims=True))
            ps.append(p)
        ovs = [_mm((p / l).astype(bf16), vb_ref[kr, :]) for p, l, kr in zip(ps, ls, krows)]
        for ov, qr in zip(ovs, qrows):
            ms = jnp.mean(ov * ov, axis=-1, keepdims=True)
            o_ref[qr, :] = (ov * lax.rsqrt(ms + RMS_EPS) * gain_ref[...]).astype(o_ref.dtype)
        return carry

    lax.fori_loop(0, rows_per_step // lockstep, body, 0)


def na_branch(pm, rpb, gain, rows_per_step=8):
    b, t, _ = pm.shape
    rows = t // GRID_W
    assert rows >= NA_WIN_ROWS and rows % rows_per_step == 0
    tq = rows_per_step * GRID_W
    bias = _na_bias_table(rpb)
    return pl.pallas_call(
        functools.partial(_na_kernel, rows=rows, rows_per_step=rows_per_step, lockstep=8),
        out_shape=jax.ShapeDtypeStruct((b, t, NA_WIDTH), bf16),
        grid=(b, NA_HEADS, rows // rows_per_step),
        in_specs=[pl.BlockSpec((None, tq, NA_HD), lambda bi, h, g: (bi, g, h)),
                  pl.BlockSpec((None, t, NA_HD), lambda bi, h, g: (bi, 0, NA_HEADS + h)),
                  pl.BlockSpec((None, t, NA_HD), lambda bi, h, g: (bi, 0, 2 * NA_HEADS + h)),
                  pl.BlockSpec((None, NA_WIN_ROWS, GRID_W, NA_WIN_ROWS * GRID_W), lambda bi, h, g: (h, 0, 0, 0)),
                  pl.BlockSpec((None, 1, NA_HD), lambda bi, h, g: (h, 0, 0))],
        out_specs=pl.BlockSpec((None, tq, NA_HD), lambda bi, h, g: (bi, g, h)),
        scratch_shapes=[pltpu.VMEM((t, NA_HD), bf16), pltpu.VMEM((t, NA_HD), bf16)],
        compiler_params=_cparams(3),
        name="na_attention",
    )(pm, pm, pm, bias, gain.reshape(NA_HEADS, 1, NA_HD))


def _gdn_prep_kernel(x_ref, w_ref, o_ref, xp_ref, *, t, tc):
    c = pl.program_id(1)
    pad = 8
    xp_ref[0:pad, :] = jnp.zeros((pad, LANES), f32)
    xp_ref[t + pad:t + 2 * pad, :] = jnp.zeros((pad, LANES), f32)
    xp_ref[pad:t + pad, :] = x_ref[...]
    half = GDN_CONV // 2
    for ci in range(t // tc):
        acc = jnp.zeros((tc, LANES), f32)
        for j in range(GDN_CONV):
            s0 = ci * tc + pad - half + j
            acc = acc + xp_ref[s0:s0 + tc, :] * w_ref[j:j + 1, :]
        y = acc * _sigmoid(acc)
        inv = lax.rsqrt(jnp.sum(y * y, axis=-1, keepdims=True) + L2_EPS)
        fac = jnp.where(c < GDN_HEADS, inv * GDN_HD ** -0.5, jnp.where(c < 2 * GDN_HEADS, inv, 1.0))
        o_ref[ci * tc:(ci + 1) * tc, :] = y * fac


def gdn_prep(pm, conv_w, layer):
    b, t, _ = pm.shape
    tc = min(512, t)
    col0 = 3 * NA_WIDTH // LANES
    nblk = 3 * GDN_WIDTH // LANES
    return pl.pallas_call(
        functools.partial(_gdn_prep_kernel, t=t, tc=tc),
        out_shape=jax.ShapeDtypeStruct((b, t, 3 * GDN_WIDTH), f32),
        grid=(b, nblk),
        in_specs=[pl.BlockSpec((None, t, LANES), lambda bi, c: (bi, 0, col0 + c)),
                  pl.BlockSpec((None, GDN_CONV, LANES), lambda bi, c: (layer, 0, c))],
        out_specs=pl.BlockSpec((None, t, LANES), lambda bi, c: (bi, 0, c)),
        scratch_shapes=[pltpu.VMEM((t + 16, LANES), f32)],
        compiler_params=_cparams(2),
        name="gdn_prep",
    )(pm, conv_w)


def _level_masks(ii, jj, size):
    masks = []
    s = 1
    while s < size:
        masks.append(((ii // (2 * s)) == (jj // (2 * s))) & ((ii // s) != (jj // s)))
        s *= 2
    return masks


def _split2(x):
    hi = x.astype(bf16)
    return hi, (x - hi.astype(f32)).astype(bf16)


def _split3(x):
    x1 = x.astype(bf16)
    r = x - x1.astype(f32)
    x2 = r.astype(bf16)
    return x1, x2, (r - x2.astype(f32)).astype(bf16)


def _mm_f32(a, b):
    a1, a2 = _split2(a)
    b1, b2 = _split2(b)
    return _mm(jnp.concatenate([a1, a2, a1], axis=1), jnp.concatenate([b1, b1, b2], axis=0))


def _mm_hl(a, b):
    a1, a2 = _split2(a)
    b1 = b.astype(bf16)
    return _mm(jnp.concatenate([a1, a2], axis=1), jnp.concatenate([b1, b1], axis=0))


def _mm_sel(sel, x):
    x1, x2, x3 = _split3(x)
    return _mm(jnp.concatenate([sel, sel, sel], axis=1), jnp.concatenate([x1, x2, x3], axis=0))


def _mm_rsel(x, sel):
    x1, x2 = _split2(x)
    return _mm(jnp.concatenate([x1, x2], axis=1), jnp.concatenate([sel, sel], axis=0))


def _nt_sel(sel, x):
    x1, x2, x3 = _split3(x)
    return _nt(jnp.concatenate([sel, sel, sel], axis=1), jnp.concatenate([x1, x2, x3], axis=1))


def _tri_inverse(ms, eye, masks, hook):
    xs = [eye - jnp.where(masks[0], m, 0.0) for m in ms]
    for mask in masks[1:]:
        xcs = [_mm_hl(x, jnp.where(mask, m, 0.0)) for x, m in zip(xs, ms)]
        hook()
        xs = [x - _mm_hl(xc, x) for x, xc in zip(xs, xcs)]
        hook()
    return xs


def _stage_hook(gen, n_stages, n_hooks):
    state = {"calls": 0, "done": 0}

    def hook():
        state["calls"] += 1
        target = -(-state["calls"] * n_stages // n_hooks)
        while state["done"] < min(target, n_stages):
            next(gen, None)
            state["done"] += 1

    return hook


def _no_hook():
    return None


def _pair_masks():
    ii = _iota2((LANES, LANES), 0)
    jj = _iota2((LANES, LANES), 1)
    same = (ii // CHUNK) == (jj // CHUNK)
    ti, tj = ii % CHUNK, jj % CHUNK
    return ii, jj, same, ti, tj


def _gdn_gates_kernel(ab_ref, alog_ref, dtb_ref, gt_ref, grow_ref, *, tb):
    ii, jj, same, ti, tj = _pair_masks()
    tri = ((same & (ti >= tj)).astype(bf16), (same & (ti <= tj)).astype(bf16))
    eye = (ii == jj).astype(bf16)
    for p in range(tb // LANES):
        rows = slice(p * LANES, (p + 1) * LANES)
        ab = ab_ref[rows, :]
        g = -jnp.exp(alog_ref[...]) * _softplus(ab + dtb_ref[...])
        fwd = _mm_sel(tri[0], g)
        bwd = _mm_sel(tri[1], g)
        gt = jnp.where(jj < GDN_HEADS, fwd, jnp.where(jj < N_DIRS * GDN_HEADS, bwd, _sigmoid(ab)))
        gt_ref[rows, :] = gt
        grow_ref[p] = _nt_sel(eye, gt)


def gdn_gates(pt, a_log, dt_bias, tb=512):
    b, t, _ = pt.shape
    tb = min(tb, t)
    npad = LANES - N_DIRS * GDN_HEADS
    alog = jnp.pad(a_log.reshape(1, -1), ((0, 0), (0, npad)))
    dtb = jnp.pad(dt_bias.reshape(1, -1), ((0, 0), (0, npad)))
    vec = pl.BlockSpec((1, LANES), lambda bi, i: (0, 0))
    return pl.pallas_call(
        functools.partial(_gdn_gates_kernel, tb=tb),
        out_shape=[jax.ShapeDtypeStruct((b, t, LANES), f32),
                   jax.ShapeDtypeStruct((b, t // LANES, LANES, LANES), f32)],
        grid=(b, t // tb),
        in_specs=[pl.BlockSpec((None, tb, LANES), lambda bi, i: (bi, i, 0)), vec, vec],
        out_specs=[pl.BlockSpec((None, tb, LANES), lambda bi, i: (bi, i, 0)),
                   pl.BlockSpec((None, tb // LANES, LANES, LANES), lambda bi, i: (bi, i, 0, 0))],
        compiler_params=_cparams(2),
        name="gdn_gates",
    )(pt, alog, dtb)


def _gdn_kernel(q_ref, k_ref, v_ref, z_ref, gt_ref, grow_ref, gain_ref, o_ref,
                u_s, w_s, qg_s, kg_s, a_s, el_s, oacc, *, t, seg):
    h = pl.program_id(1)
    npairs = t // LANES
    nseg = npairs // seg
    assert npairs % seg == 0 and npairs % 2 == 0
    ii, jj, same, ti, tj = _pair_masks()
    eye = (ii == jj).astype(f32)
    lvl = _level_masks(ti, tj, CHUNK)
    masks = ((same & (ti >= tj), same & (ti > tj)), (same & (ti <= tj), same & (ti < tj)))
    dirs = range(N_DIRS)
    n_hooks = 2 * (len(lvl) - 1) + 1
    n_stages = seg * 2 * 2

    def tile_of(d, sg, j):
        p = sg * seg + j
        return p if d == 0 else npairs - 1 - p

    def prep_seg(sg, slot, hook):
        probs = []
        for j in range(seg):
            for d in dirs:
                p = tile_of(d, sg, j)
                rows = pl.ds(pl.multiple_of(p * LANES, LANES), LANES)
                k = k_ref[rows, :]
                probs.append(dict(d=d, j=j, p=p, q=q_ref[rows, :], k=k, v=v_ref[rows, :], gt=gt_ref[rows, :],
                                  kb16=k.astype(bf16)))
        for pb in probs:
            pb["qk"] = _nt(pb["q"].astype(bf16), pb["kb16"])
            pb["kk"] = _nt(pb["kb16"], pb["kb16"])
        hook()
        for pb in probs:
            d = pb["d"]
            la = d * GDN_HEADS + h
            lb = N_DIRS * GDN_HEADS + la
            incl, strict = masks[d]
            gcol = jnp.sum(jnp.where(jj == la, pb["gt"], 0.0), axis=-1, keepdims=True)
            bcol = jnp.sum(jnp.where(jj == lb, pb["gt"], 0.0), axis=-1, keepdims=True)
            grow = grow_ref[pb["p"], pl.ds(la, 1), :]
            pb["dec"] = jnp.exp(jnp.where(incl, gcol - grow, NEG))
            pb["gcol"], pb["bcol"] = gcol, bcol
            pb["m"] = jnp.where(strict, pb["kk"] * bcol * pb["dec"], 0.0)
        tinvs = _tri_inverse([pb["m"] for pb in probs], eye, lvl, hook)
        uws = []
        for pb, tinv in zip(probs, tinvs):
            gcol, bcol = pb["gcol"], pb["bcol"]
            pb["eg"] = jnp.exp(gcol)
            rhs = jnp.concatenate([pb["v"] * bcol, pb["k"] * (bcol * pb["eg"])], axis=1).astype(bf16)
            t1, t2 = _split2(tinv)
            uws.append(_mm(jnp.concatenate([t1, t2], axis=1), jnp.concatenate([rhs, rhs], axis=0)))
        for pb, uw in zip(probs, uws):
            d, gcol = pb["d"], pb["gcol"]
            srows = slice(pb["j"] * LANES, (pb["j"] + 1) * LANES)
            ends = (CHUNK - 1, 2 * CHUNK - 1) if d == 0 else (0, CHUNK)
            gl = [gcol[e:e + 1, :] for e in ends]
            gtot = jnp.concatenate([jnp.broadcast_to(g, (CHUNK, 1)) for g in gl], axis=0)
            a2 = jnp.where(masks[d][0], pb["qk"] * pb["dec"], 0.0)
            u_s[slot, d, srows, :] = uw[:, :GDN_HD]
            w_s[slot, d, srows, :] = uw[:, GDN_HD:].astype(bf16)
            a_s[slot, d, srows, :] = jnp.concatenate([a2[:CHUNK, :CHUNK], a2[CHUNK:, CHUNK:]], axis=0).astype(bf16)
            qg_s[slot, d, srows, :] = (pb["q"] * pb["eg"]).astype(bf16)
            kg_s[slot, d, srows, :] = (pb["k"] * jnp.exp(gtot - gcol)).astype(bf16)
            el_s[slot, d, pb["j"] * 16:(pb["j"] + 1) * 16, :] = jnp.concatenate(
                [jnp.broadcast_to(jnp.exp(g), (8, LANES)) for g in gl], axis=0)

    def scan_seg(sg, slot, st):
        halves = ((0, 1), (1, 0))
        for j in range(seg):
            for step in range(2):
                sl = [slice(j * LANES + halves[d][step] * CHUNK, j * LANES + (halves[d][step] + 1) * CHUNK)
                      for d in dirs]
                sb = [st[d].astype(bf16) for d in dirs]
                ws = [_mm(w_s[slot, d, sl[d], :], sb[d]) for d in dirs]
                qs = [_mm(qg_s[slot, d, sl[d], :], sb[d]) for d in dirs]
                yield
                vnb = [(u_s[slot, d, sl[d], :] - ws[d]).astype(bf16) for d in dirs]
                av = [_mm(a_s[slot, d, sl[d], :], vnb[d]) for d in dirs]
                kv = [_tn(kg_s[slot, d, sl[d], :], vnb[d]) for d in dirs]
                yield
                for d in dirs:
                    e0 = j * 16 + halves[d][step] * 8
                    st[d] = st[d] * el_s[slot, d, e0:e0 + 8, :][0:1, :] + kv[d]
                    start = tile_of(d, sg, j) * LANES + halves[d][step] * CHUNK
                    oacc[pl.ds(pl.multiple_of(start, CHUNK), CHUNK), :] += qs[d] + av[d]

    oacc[...] = jnp.zeros(oacc.shape, f32)
    prep_seg(0, 0, _no_hook)

    def body(sg, states):
        st = list(states)
        gen = scan_seg(sg, sg % 2, st)
        prep_seg(sg + 1, (sg + 1) % 2, _stage_hook(gen, n_stages, n_hooks))
        for _ in gen:
            pass
        return tuple(st)

    s0 = jnp.zeros((GDN_HD, GDN_HD), f32)
    st = list(lax.fori_loop(0, nseg - 1, body, (s0, s0)))
    for _ in scan_seg(nseg - 1, (nseg - 1) % 2, st):
        pass

    tb = min(512, t)
    for ci in range(t // tb):
        rows = slice(ci * tb, (ci + 1) * tb)
        o = oacc[rows, :]
        z = z_ref[rows, :]
        ms = jnp.mean(o * o, axis=-1, keepdims=True)
        o_ref[rows, :] = (o * lax.rsqrt(ms + RMS_EPS) * gain_ref[...] * (z * _sigmoid(z))).astype(o_ref.dtype)


def gdn_branch(pm, pt, qkv, a_log, dt_bias, gain):
    b, t, _ = pm.shape
    assert t % LANES == 0
    seg = min(4, t // LANES // 2)
    rows = seg * LANES
    zcol0 = (3 * NA_WIDTH + 3 * GDN_WIDTH) // LANES
    gt, grow = gdn_gates(pt, a_log, dt_bias)
    hd = lambda off: (lambda bi, h: (bi, 0, off + h))
    return pl.pallas_call(
        functools.partial(_gdn_kernel, t=t, seg=seg),
        out_shape=jax.ShapeDtypeStruct((b, t, GDN_WIDTH), bf16),
        grid=(b, GDN_HEADS),
        in_specs=[pl.BlockSpec((None, t, GDN_HD), hd(0)),
                  pl.BlockSpec((None, t, GDN_HD), hd(GDN_HEADS)),
                  pl.BlockSpec((None, t, GDN_HD), hd(2 * GDN_HEADS)),
                  pl.BlockSpec((None, t, GDN_HD), hd(zcol0)),
                  pl.BlockSpec((None, t, LANES), lambda bi, h: (bi, 0, 0)),
                  pl.BlockSpec((None, t // LANES, LANES, LANES), lambda bi, h: (bi, 0, 0, 0)),
                  pl.BlockSpec((1, LANES), lambda bi, h: (0, 0))],
        out_specs=pl.BlockSpec((None, t, GDN_HD), hd(0)),
        scratch_shapes=[pltpu.VMEM((2, N_DIRS, rows, GDN_HD), f32),
                        pltpu.VMEM((2, N_DIRS, rows, GDN_HD), bf16),
                        pltpu.VMEM((2, N_DIRS, rows, GDN_HD), bf16),
                        pltpu.VMEM((2, N_DIRS, rows, GDN_HD), bf16),
                        pltpu.VMEM((2, N_DIRS, rows, CHUNK), bf16),
                        pltpu.VMEM((2, N_DIRS, seg * 16, LANES), f32),
                        pltpu.VMEM((t, GDN_HD), f32)],
        compiler_params=_cparams(2),
        name="gdn_scan",
    )(qkv, qkv, qkv, pm, gt, grow, gain.reshape(1, GDN_HD))


def _rwkv_prep_kernel(cur_a, prev_a, next_a, cur_b, prev_b, next_b, mu_ref, wup_ref, aup_ref, gup_ref, w0_ref,
                      a0_ref, rkv_ref, gate_ref, lw_ref, eta_ref, buf, *, tb):
    i = pl.program_id(1)
    nt = pl.num_programs(1)
    na = TAIL_A_COLS - GATE_COLS
    buf[8:8 + tb, 0:na] = cur_a[:, GATE_COLS:]
    buf[8:8 + tb, na:] = cur_b[:, 0:TAIL_B_USED]
    has_prev = i > 0
    has_next = i < nt - 1
    buf[7:8, 0:na] = jnp.where(has_prev, prev_a[7:8, GATE_COLS:], 0.0)
    buf[7:8, na:] = jnp.where(has_prev, prev_b[7:8, 0:TAIL_B_USED], 0.0)
    buf[8 + tb:9 + tb, 0:na] = jnp.where(has_next, next_a[0:1, GATE_COLS:], 0.0)
    buf[8 + tb:9 + tb, na:] = jnp.where(has_next, next_b[0:1, 0:TAIL_B_USED], 0.0)

    def mixed(c0, c1):
        x = buf[8:8 + tb, c0:c1]
        sh = 0.5 * (buf[7:7 + tb, c0:c1] + buf[9:9 + tb, c0:c1])
        return x + (sh - x) * mu_ref[:, c0:c1]

    for gi in range(3):
        c0 = gi * RWKV_WIDTH
        rkv_ref[:, c0:c0 + RWKV_WIDTH] = mixed(c0, c0 + RWKV_WIDTH)
    xwa = mixed(3 * RWKV_WIDTH, 3 * RWKV_WIDTH + LANES)
    xg = mixed(3 * RWKV_WIDTH + LANES, 3 * RWKV_WIDTH + 2 * LANES)
    gate_ref[...] = _mm(_sigmoid(xg).astype(bf16), gup_ref[...].astype(bf16))
    txw = jnp.tanh(xwa)
    for d in range(N_DIRS):
        log_w = -_softplus(-(w0_ref[d] + _mm_f32(txw, wup_ref[d]))) - 0.5
        lw_ref[d] = -jnp.exp(log_w)
        eta_ref[d] = _sigmoid(a0_ref[d] + _mm_f32(xwa, aup_ref[d]))


def rwkv_prep(pta, ptb, mu, w0, w_up, a0, a_up, g_up, tb=256):
    b, t, _ = pta.shape
    tb = min(tb, t)
    nb8 = tb // 8
    mu_p = mu.reshape(1, SHIFT_COLS)
    lora = w_up.shape[1]
    wup_p = jnp.pad(w_up, ((0, 0), (0, LANES - lora), (0, 0)))
    aup_p = jnp.pad(a_up, ((0, 0), (lora, LANES - lora - a_up.shape[1]), (0, 0)))
    out_sds = lambda *lead: jax.ShapeDtypeStruct(lead + (b, t, RWKV_WIDTH), f32)
    full = lambda shape: pl.BlockSpec(shape, lambda bi, i: (0,) * len(shape))

    def halo(cols, side):
        if side == 0:
            return pl.BlockSpec((None, tb, cols), lambda bi, i: (bi, i, 0))
        if side < 0:
            return pl.BlockSpec((None, 8, cols), lambda bi, i: (bi, jnp.maximum(i * nb8 - 1, 0), 0))
        return pl.BlockSpec((None, 8, cols), lambda bi, i: (bi, jnp.minimum((i + 1) * nb8, t // 8 - 1), 0))

    return pl.pallas_call(
        functools.partial(_rwkv_prep_kernel, tb=tb),
        out_shape=[jax.ShapeDtypeStruct((b, t, 3 * RWKV_WIDTH), f32), out_sds(), out_sds(N_DIRS), out_sds(N_DIRS)],
        grid=(b, t // tb),
        in_specs=[halo(TAIL_A_COLS, 0), halo(TAIL_A_COLS, -1), halo(TAIL_A_COLS, 1),
                  halo(TAIL_TN, 0), halo(TAIL_TN, -1), halo(TAIL_TN, 1),
                  full((1, SHIFT_COLS)), full((N_DIRS, LANES, RWKV_WIDTH)), full((N_DIRS, LANES, RWKV_WIDTH)),
                  full((LANES, RWKV_WIDTH)), full((N_DIRS, 1, RWKV_WIDTH)), full((N_DIRS, 1, RWKV_WIDTH))],
        out_specs=[pl.BlockSpec((None, tb, 3 * RWKV_WIDTH), lambda bi, i: (bi, i, 0)),
                   pl.BlockSpec((None, tb, RWKV_WIDTH), lambda bi, i: (bi, i, 0)),
                   pl.BlockSpec((N_DIRS, None, tb, RWKV_WIDTH), lambda bi, i: (0, bi, i, 0)),
                   pl.BlockSpec((N_DIRS, None, tb, RWKV_WIDTH), lambda bi, i: (0, bi, i, 0))],
        scratch_shapes=[pltpu.VMEM((tb + 16, SHIFT_COLS), f32)],
        compiler_params=_cparams(2),
        name="rwkv_prep",
    )(pta, pta, pta, ptb, ptb, ptb, mu_p, wup_p, aup_p, g_up, w0.reshape(N_DIRS, 1, -1), a0.reshape(N_DIRS, 1, -1))


def _stack2(x, m0, m1):
    return jnp.concatenate([x * m0, x * m1], axis=0)


def _rwkv_kernel(r_ref, k_ref, v_ref, gate_ref, lw0_ref, lw1_ref, eta0_ref, eta1_ref,
                 kk_ref, ka_ref, rk_ref, lnw_ref, lnb_ref, o_ref,
                 wt_s, uv_s, yv_s, arb_s, rt_s, bh_s, sk_s, eg_s, yacc, kk_s, gc_s, *, t, seg):
    n = t // CHUNK
    nseg = n // seg
    assert n % seg == 0 and n % 2 == 0 and nseg >= 2
    c2 = 2 * CHUNK
    lane = _iota2((1, LANES), 1)
    m0 = (lane < RWKV_HD).astype(f32)
    m1 = 1.0 - m0
    ii, jj, same, ti, tj = _pair_masks()
    bd = same.astype(bf16)
    eye = (ii == jj).astype(f32)
    lvl = _level_masks(ti, tj, CHUNK)
    masks = ((same & (ti >= tj), same & (ti > tj)), (same & (ti <= tj), same & (ti < tj)))
    lw_refs = (lw0_ref, lw1_ref)
    eta_refs = (eta0_ref, eta1_ref)
    kkw = kk_ref[...]
    kaw = ka_ref[...]

    def pre(p, carry):
        rows = pl.ds(pl.multiple_of(p * LANES, LANES), LANES)
        kkr = k_ref[rows, :] * kkw
        kk_s[rows, :] = kkr * lax.rsqrt(_mm_rsel(kkr * kkr, bd) + L2_EPS)
        for d in range(N_DIRS):
            gc_s[d, rows, :] = _mm_sel(masks[d][0].astype(bf16), lw_refs[d][rows, :])
        return carry

    lax.fori_loop(0, t // LANES, pre, 0, unroll=2)

    def prep_chunks(problems, slot, hook):
        pbs = []
        for d, c, j in problems:
            rows = pl.ds(pl.multiple_of(c * CHUNK, CHUNK), CHUNK)
            r = r_ref[rows, :]
            k = k_ref[rows, :]
            lw = lw_refs[d][rows, :]
            eta = eta_refs[d][rows, :]
            kk = kk_s[rows, :]
            kd = k * (1.0 + (eta - 1.0) * kaw)
            bvec = -kk * eta
            gcum = gc_s[d, rows, :]
            gtot = gcum[CHUNK - 1:CHUNK, :] if d == 0 else gcum[0:1, :]
            eneg = jnp.exp(-gcum)
            erem = jnp.exp(gtot - gcum)
            at2 = _stack2(kk * jnp.exp(gcum - lw), m0, m1)
            rt2 = _stack2(r * jnp.exp(gcum), m0, m1)
            kt2 = _stack2(kd * eneg, m0, m1)
            bt2 = _stack2(bvec * eneg, m0, m1)
            pbs.append(dict(d=d, slot=j, at2=at2, rt2=rt2, gtot=gtot,
                            v2b=_stack2(v_ref[rows, :], m0, m1).astype(bf16),
                            bh=_stack2(bvec * erem, m0, m1).astype(bf16),
                            kh=_stack2(kd * erem, m0, m1).astype(bf16),
                            lhs=jnp.concatenate([at2, rt2], axis=0).astype(bf16),
                            rhs=jnp.concatenate([kt2, bt2], axis=0).astype(bf16)))
        ggs = [_nt(pb["lhs"], pb["rhs"]) for pb in pbs]
        hook()
        for pb, gg in zip(pbs, ggs):
            incl, strict = masks[pb["d"]]
            pb["a_ak"] = jnp.where(strict, gg[0:c2, 0:c2], 0.0).astype(bf16)
            pb["a_ab"] = jnp.where(strict, gg[0:c2, c2:2 * c2], 0.0)
            pb["a_rk"] = jnp.where(incl, gg[c2:2 * c2, 0:c2], 0.0).astype(bf16)
            pb["a_rb"] = jnp.where(incl, gg[c2:2 * c2, c2:2 * c2], 0.0).astype(bf16)
        akv = [_mm(pb["a_ak"], pb["v2b"]) for pb in pbs]
        yv = [_mm(pb["a_rk"], pb["v2b"]) for pb in pbs]
        sk = [_tn(pb["v2b"], pb["kh"]) for pb in pbs]
        hook()
        tinvs = _tri_inverse([-pb["a_ab"] for pb in pbs], eye, lvl, hook)
        wus = []
        for pb, tinv, av in zip(pbs, tinvs, akv):
            rhs = jnp.concatenate([pb["at2"], av], axis=1).astype(bf16)
            t1, t2 = _split2(tinv)
            wus.append(_mm(jnp.concatenate([t1, t2], axis=1), jnp.concatenate([rhs, rhs], axis=0)))
        for i, pb in enumerate(pbs):
            d, j = pb["d"], pb["slot"]
            srows = slice(j * c2, (j + 1) * c2)
            wt_s[slot, d, srows, :] = wus[i][:, :LANES].astype(bf16)
            uv_s[slot, d, srows, :] = wus[i][:, LANES:]
            yv_s[slot, d, srows, :] = yv[i]
            arb_s[slot, d, srows, :] = pb["a_rb"]
            rt_s[slot, d, srows, :] = pb["rt2"].astype(bf16)
            bh_s[slot, d, srows, :] = pb["bh"]
            sk_s[slot, d, srows, :] = sk[i]
            eg_s[slot, d, j * 8:(j + 1) * 8, :] = jnp.broadcast_to(jnp.exp(pb["gtot"]), (8, LANES))

    n_hooks = 2 * (len(lvl) - 1) + 2
    n_stages = seg * 2
    dirs = range(N_DIRS)

    def chunk_of(d, sg, j):
        c = sg * seg + j
        return c if d == 0 else n - 1 - c

    def prep_seg(sg, slot, hook):
        prep_chunks([(d, chunk_of(d, sg, j), j) for j in range(seg) for d in dirs], slot, hook)

    def scan_seg(sg, slot, st):
        for j in range(seg):
            srows = slice(j * c2, (j + 1) * c2)
            sb = [st[d].astype(bf16) for d in dirs]
            ws = [_nt(wt_s[slot, d, srows, :], sb[d]) for d in dirs]
            rs = [_nt(rt_s[slot, d, srows, :], sb[d]) for d in dirs]
            yield
            u2b = [(uv_s[slot, d, srows, :] + ws[d]).astype(bf16) for d in dirs]
            au = [_mm(arb_s[slot, d, srows, :], u2b[d]) for d in dirs]
            ub = [_tn(u2b[d], bh_s[slot, d, srows, :]) for d in dirs]
            yield
            for d in dirs:
                rows = pl.ds(pl.multiple_of(chunk_of(d, sg, j) * CHUNK, CHUNK), CHUNK)
                y2 = yv_s[slot, d, srows, :] + rs[d] + au[d]
                st[d] = st[d] * eg_s[slot, d, j * 8:(j + 1) * 8, :][0:1, :] + sk_s[slot, d, srows, :] + ub[d]
                yacc[rows, :] += y2[0:CHUNK, :] + y2[CHUNK:c2, :]

    yacc[...] = jnp.zeros(yacc.shape, f32)
    prep_seg(0, 0, _no_hook)

    def body(sg, states):
        st = list(states)
        gen = scan_seg(sg, sg % 2, st)
        prep_seg(sg + 1, (sg + 1) % 2, _stage_hook(gen, n_stages, n_hooks))
        for _ in gen:
            pass
        return tuple(st)

    s0 = jnp.zeros((LANES, LANES), f32)
    st = list(lax.fori_loop(0, nseg - 1, body, (s0, s0)))
    for _ in scan_seg(nseg - 1, (nseg - 1) % 2, st):
        pass

    tb = min(512, t)
    inv_hd = 1.0 / RWKV_HD
    for ci in range(t // tb):
        rows = slice(ci * tb, (ci + 1) * tb)
        y = yacc[rows, :]
        mean = _mm_rsel(y, bd) * inv_hd
        dlt = y - mean
        var = _mm_rsel(dlt * dlt, bd) * inv_hd
        yn = dlt * lax.rsqrt(var + RWKV_GN_EPS) * lnw_ref[...] + lnb_ref[...]
        k = k_ref[rows, :]
        kd_sum = k * (2.0 + (eta0_ref[rows, :] + eta1_ref[rows, :] - 2.0) * kaw)
        bonus = _mm_rsel(r_ref[rows, :] * kd_sum * rk_ref[...], bd) * v_ref[rows, :]
        o_ref[rows, :] = ((yn + bonus) * gate_ref[rows, :]).astype(o_ref.dtype)


def rwkv_branch(rkv, gate, lw, eta, k_k, k_a, r_k, ln_w, ln_b, seg=4):
    b, t, _ = gate.shape
    n = t // CHUNK
    seg = min(seg, n // 2)
    c2 = 2 * CHUNK
    blk = lambda off: pl.BlockSpec((None, t, LANES), lambda bi, p: (bi, 0, off + p))
    dblk = lambda d: pl.BlockSpec((None, None, t, LANES), lambda bi, p: (d, bi, 0, p))
    vec = pl.BlockSpec((1, LANES), lambda bi, p: (0, p))
    row = lambda x: x.reshape(1, RWKV_WIDTH)
    seg_f32 = pltpu.VMEM((2, N_DIRS, seg * c2, LANES), f32)
    seg_b16 = pltpu.VMEM((2, N_DIRS, seg * c2, LANES), bf16)
    return pl.pallas_call(
        functools.partial(_rwkv_kernel, t=t, seg=seg),
        out_shape=jax.ShapeDtypeStruct((b, t, RWKV_WIDTH), bf16),
        grid=(b, RWKV_PAIRS),
        in_specs=[blk(0), blk(RWKV_PAIRS), blk(2 * RWKV_PAIRS), blk(0),
                  dblk(0), dblk(1), dblk(0), dblk(1), vec, vec, vec, vec, vec],
        out_specs=blk(0),
        scratch_shapes=[seg_b16,
                        seg_f32,
                        seg_f32,
                        seg_b16,
                        seg_b16,
                        seg_b16,
                        seg_f32,
                        pltpu.VMEM((2, N_DIRS, seg * 8, LANES), f32),
                        pltpu.VMEM((t, LANES), f32),
                        pltpu.VMEM((t, LANES), f32),
                        pltpu.VMEM((N_DIRS, t, LANES), f32)],
        compiler_params=_cparams(2),
        name="rwkv_scan",
    )(rkv, rkv, rkv, gate, lw, lw, eta, eta, row(k_k), row(k_a), row(r_k), row(ln_w), row(ln_b))


def _tail_b_weights(w_in_t, layer):
    last = w_in_t[layer, MAIN_COLS + TAIL_A_COLS:, :]
    assert last.shape[0] == TAIL_B_USED
    return jnp.pad(last, ((0, TAIL_TN - TAIL_B_USED), (0, 0)))[None]


def kernel(x, norm_mix, w_in, w_out, na_rpb, na_gain, gdn_conv_w, gdn_a_log, gdn_dt_bias, gdn_gain, rwkv_mu, rwkv_w0, rwkv_w_up, rwkv_a0, rwkv_a_up, rwkv_g_up, rwkv_k_k, rwkv_k_a, rwkv_r_k, rwkv_ln_w, rwkv_ln_b, norm_ffn, ffn_w_gate, ffn_w_up, ffn_w_down, moe_router, moe_w_gate, moe_w_up, moe_w_down, norm_final):
    b, t, d = x.shape
    n = b * t
    depth = w_in.shape[0]
    h = x.reshape(n, d)
    w_in_t = jnp.swapaxes(w_in, 1, 2)
    for layer in range(depth):
        u = rmsnorm(h, norm_mix[layer], bf16)
        pm = proj(u, w_in_t, layer, MAIN_COLS, tn=512).reshape(b, t, MAIN_COLS)
        pta = proj(u, w_in_t, layer, TAIL_A_COLS, tn=TAIL_TN,
                   col_block0=MAIN_COLS // TAIL_TN).reshape(b, t, TAIL_A_COLS)
        ptb = proj(u, _tail_b_weights(w_in_t, layer), 0, TAIL_TN, tn=TAIL_TN).reshape(b, t, TAIL_TN)

        y_a = na_branch(pm, na_rpb[layer], na_gain[layer])
        qkv = gdn_prep(pm, gdn_conv_w, layer)
        y_b = gdn_branch(pm, pta, qkv, gdn_a_log[layer], gdn_dt_bias[layer], gdn_gain[layer])
        rkv, gate, lw, eta = rwkv_prep(pta, ptb, rwkv_mu[layer], rwkv_w0[layer], rwkv_w_up[layer],
                                       rwkv_a0[layer], rwkv_a_up[layer], rwkv_g_up[layer])
        y_c = rwkv_branch(rkv, gate, lw, eta, rwkv_k_k[layer], rwkv_k_a[layer], rwkv_r_k[layer],
                          rwkv_ln_w[layer], rwkv_ln_b[layer])
        h = outproj(y_a.reshape(n, -1), y_b.reshape(n, -1), y_c.reshape(n, -1), w_out, layer, h)

        i = layer // 2
        if layer % 2 == 0:
            u = rmsnorm(h, norm_ffn[layer], bf16)
            act = gateup(u, ffn_w_gate, ffn_w_up, i)
            half = ffn_w_down.shape[1] // 2
            assert half % LANES == 0
            for kb in range(2):
                h = down(act, ffn_w_down, i, h, kb, half)
        else:
            closing = layer == depth - 1
            h = moe_ffn(h, norm_ffn[layer], moe_router[i], moe_w_gate, moe_w_up, moe_w_down, i, norm_final, closing)
            if closing:
                return h.reshape(b, t, d)
    return rmsnorm(h, norm_final, f32).reshape(b, t, d)
```
